```python
import math
import jax
import jax.numpy as jnp
from jax import lax
import numpy as np


D_MODEL = 1024
BATCH = 8
SEQ = 4096
DEPTH = 2

CHUNK = 64
Q_BLOCK = 128
HEAD_DIM = 64
DIFF_HEADS = 4
DIFF_V_DIM = 2 * HEAD_DIM
FOX_HEADS = 8
D_DIFF = DIFF_HEADS * DIFF_V_DIM
D_FOX = FOX_HEADS * HEAD_DIM
D_MIX = D_DIFF + D_FOX
DQK_DIFF = DIFF_HEADS * 2 * HEAD_DIM
IN_COLS = 2 * DQK_DIFF + D_DIFF + 3 * D_FOX + FOX_HEADS
ROPE_THETA = 500000.0
ROPE_DIM = HEAD_DIM // 4
N_EXPERTS = 32
TOP_K = 4
D_FF = D_MODEL
SWIGLU_LIMIT = 7.0
SWIGLU_ALPHA = 1.702
EXPERT_BLOCK = 256
NORM_EPS = 1e-6
MAX_POS_OFFSET = 1024

kernel_name = 'hybrid_diffattn_fox_moe_adaln'


def rmsnorm(x, g):
    xf = x.astype(jnp.float32)
    y = xf * lax.rsqrt(jnp.mean(xf * xf, axis=-1, keepdims=True) + NORM_EPS)
    return (y * g.astype(jnp.float32)).astype(x.dtype)


def rope_tables(positions, dtype):
    inv_freq = ROPE_THETA ** (-jnp.arange(0, ROPE_DIM, 2, dtype=jnp.float32) / ROPE_DIM)
    ang = positions.astype(jnp.float32)[..., None] * inv_freq
    cos = jnp.cos(ang)[:, :, None, None, :].astype(dtype)
    sin = jnp.sin(ang)[:, :, None, None, :].astype(dtype)
    return cos, sin


def apply_partial_rope(x, cos, sin):
    half = ROPE_DIM // 2
    x1, x2, xp = x[..., :half], x[..., half:ROPE_DIM], x[..., ROPE_DIM:]
    return jnp.concatenate([x1 * cos - x2 * sin, x2 * cos + x1 * sin, xp], axis=-1)


def diff_attention(q, k, v, lam):
    B, H, _, S, _ = q.shape
    scale = HEAD_DIM ** -0.5
    k_chunk = jnp.arange(S) // CHUNK

    def one_block(blk):
        start = blk * Q_BLOCK
        qb = lax.dynamic_slice_in_dim(q, start, Q_BLOCK, axis=3)
        s = jnp.einsum('bhmqd,bhmkd->bhmqk', qb, k).astype(jnp.float32) * scale
        q_chunk = (start + jnp.arange(Q_BLOCK)) // CHUNK
        mask = k_chunk[None, :] <= q_chunk[:, None]
        p = jax.nn.softmax(jnp.where(mask, s, -jnp.inf), axis=-1)
        a = p[:, :, 0] - lam * p[:, :, 1]
        return jnp.einsum('bhqk,bhkd->bhqd', a.astype(v.dtype), v)

    out = lax.map(one_block, jnp.arange(S // Q_BLOCK))
    return out.transpose(1, 2, 0, 3, 4).reshape(B, H, S, v.shape[-1])


def forgetting_attention(q, k, v, log_f):
    B, H, S, _ = q.shape
    scale = HEAD_DIM ** -0.5
    cum = jnp.cumsum(log_f, axis=-1)
    k_pos = jnp.arange(S)

    def one_block(blk):
        start = blk * Q_BLOCK
        qb = lax.dynamic_slice_in_dim(q, start, Q_BLOCK, axis=2)
        cq = lax.dynamic_slice_in_dim(cum, start, Q_BLOCK, axis=2)
        s = jnp.einsum('bhqd,bhkd->bhqk', qb, k).astype(jnp.float32) * scale
        s = s + cq[..., :, None] - cum[..., None, :]
        q_pos = start + jnp.arange(Q_BLOCK)
        mask = k_pos[None, :] <= q_pos[:, None]
        p = jax.nn.softmax(jnp.where(mask, s, -jnp.inf), axis=-1)
        return jnp.einsum('bhqk,bhkd->bhqd', p.astype(v.dtype), v)

    out = lax.map(one_block, jnp.arange(S // Q_BLOCK))
    return out.transpose(1, 2, 0, 3, 4).reshape(B, H, S, HEAD_DIM)


def hybrid_mixer(h, cos, sin, w_in_l, b_forget_l, lq1, lk1, lq2, lk2, subln_g, w_out_l, lam_init):
    B, S, _ = h.shape
    proj = h @ w_in_l
    splits = list(np.cumsum([DQK_DIFF, DQK_DIFF, D_DIFF, D_FOX, D_FOX, D_FOX]))
    dq, dk, dv, fq, fk, fv, ff = jnp.split(proj, splits, axis=-1)

    dq = apply_partial_rope(dq.reshape(B, S, DIFF_HEADS, 2, HEAD_DIM), cos, sin)
    dk = apply_partial_rope(dk.reshape(B, S, DIFF_HEADS, 2, HEAD_DIM), cos, sin)
    dq = dq.transpose(0, 2, 3, 1, 4)
    dk = dk.transpose(0, 2, 3, 1, 4)
    dv = dv.reshape(B, S, DIFF_HEADS, DIFF_V_DIM).transpose(0, 2, 1, 3)
    f32 = jnp.float32
    lam = (jnp.exp(jnp.sum(lq1.astype(f32) * lk1.astype(f32)))
           - jnp.exp(jnp.sum(lq2.astype(f32) * lk2.astype(f32))) + lam_init)
    a_out = diff_attention(dq, dk, dv, lam)
    a_out = rmsnorm(a_out, subln_g) * (1.0 - lam_init)
    a_out = a_out.transpose(0, 2, 1, 3).reshape(B, S, D_DIFF)

    fq = fq.reshape(B, S, FOX_HEADS, HEAD_DIM).transpose(0, 2, 1, 3)
    fk = fk.reshape(B, S, FOX_HEADS, HEAD_DIM).transpose(0, 2, 1, 3)
    fv = fv.reshape(B, S, FOX_HEADS, HEAD_DIM).transpose(0, 2, 1, 3)
    log_f = jax.nn.log_sigmoid((ff + b_forget_l).astype(f32)).transpose(0, 2, 1)
    b_out = forgetting_attention(fq, fk, fv, log_f)
    b_out = b_out.transpose(0, 2, 1, 3).reshape(B, S, D_FOX)

    return jnp.concatenate([a_out, b_out], axis=-1) @ w_out_l


def moe_ffn(h, router_w, router_b, w1, b1, w2, b2):
    B, S, D = h.shape
    N = B * S
    NA = N * TOP_K
    xt = h.reshape(N, D)
    logits = (xt @ router_w + router_b).astype(jnp.float32)
    top_vals, top_idx = lax.top_k(logits, TOP_K)
    gates = jax.nn.softmax(top_vals, axis=-1)

    e_flat = top_idx.reshape(NA)
    tok = jnp.arange(NA) // TOP_K
    order = jnp.argsort(e_flat)
    e_sorted = e_flat[order]
    counts = jnp.zeros((N_EXPERTS,), jnp.int32).at[e_flat].add(1)
    padded = (counts + EXPERT_BLOCK - 1) // EXPERT_BLOCK * EXPERT_BLOCK
    pend = jnp.cumsum(padded)
    pstart = pend - padded
    gstart = jnp.cumsum(counts) - counts
    dest_sorted = pstart[e_sorted] + (jnp.arange(NA) - gstart[e_sorted])
    dest = jnp.zeros((NA,), jnp.int32).at[order].set(dest_sorted.astype(jnp.int32))
    R = NA + N_EXPERTS * EXPERT_BLOCK
    n_blocks = R // EXPERT_BLOCK
    x_buf = jnp.zeros((R, D), h.dtype).at[dest].set(xt[tok])
    block_expert = jnp.minimum(
        jnp.searchsorted(pend, jnp.arange(n_blocks) * EXPERT_BLOCK, side='right'),
        N_EXPERTS - 1)

    def expert_block(args):
        xb, e = args
        gu = xb @ w1[e] + b1[e]
        gate = jnp.minimum(gu[:, :D_FF], SWIGLU_LIMIT)
        up = jnp.clip(gu[:, D_FF:], -SWIGLU_LIMIT, SWIGLU_LIMIT)
        glu = gate * jax.nn.sigmoid(gate * SWIGLU_ALPHA)
        return ((up + 1.0) * glu) @ w2[e] + b2[e]

    y_buf = lax.map(expert_block, (x_buf.reshape(n_blocks, EXPERT_BLOCK, D), block_expert))
    y_assign = y_buf.reshape(R, D)[dest].reshape(N, TOP_K, D)
    y = jnp.einsum('nk,nkd->nd', gates.astype(h.dtype), y_assign)
    return y.reshape(B, S, D)


def setup_inputs(seed: int = 0) -> dict:
    key = jax.random.key(seed)
    ks = jax.random.split(key, 24)
    f32 = jnp.float32
    nrm = lambda k, shape, s: (jax.random.normal(k, shape, f32) * s)
    x = jax.random.normal(ks[0], (BATCH, SEQ, D_MODEL), f32)
    c = jax.random.normal(ks[1], (BATCH, D_MODEL), f32)
    offsets = jax.random.randint(ks[2], (BATCH, 1), 0, MAX_POS_OFFSET, dtype=jnp.int32)
    positions = (offsets + jnp.arange(SEQ, dtype=jnp.int32)[None, :]).astype(jnp.int32)
    return {
        'x': x,
        'c': c,
        'positions': positions,
        'ada_w': nrm(ks[3], (DEPTH, D_MODEL, 6 * D_MODEL), D_MODEL ** -0.5),
        'ada_b': nrm(ks[4], (DEPTH, 6 * D_MODEL), 0.01),
        'norm_mix': 1.0 + nrm(ks[5], (DEPTH, D_MODEL), 0.02),
        'norm_ffn': 1.0 + nrm(ks[6], (DEPTH, D_MODEL), 0.02),
        'w_in': nrm(ks[7], (DEPTH, D_MODEL, IN_COLS), D_MODEL ** -0.5),
        'b_forget': 2.0 + nrm(ks[8], (DEPTH, FOX_HEADS), 0.5),
        'lambda_q1': nrm(ks[9], (DEPTH, HEAD_DIM), 0.1),
        'lambda_k1': nrm(ks[10], (DEPTH, HEAD_DIM), 0.1),
        'lambda_q2': nrm(ks[11], (DEPTH, HEAD_DIM), 0.1),
        'lambda_k2': nrm(ks[12], (DEPTH, HEAD_DIM), 0.1),
        'diff_subln': 1.0 + nrm(ks[13], (DEPTH, DIFF_V_DIM), 0.02),
        'w_out': nrm(ks[14], (DEPTH, D_MIX, D_MODEL), D_MIX ** -0.5),
        'router_w': nrm(ks[15], (DEPTH, D_MODEL, N_EXPERTS), D_MODEL ** -0.5),
        'router_b': nrm(ks[16], (DEPTH, N_EXPERTS), 0.01),
        'exp_w1': nrm(ks[17], (DEPTH, N_EXPERTS, D_MODEL, 2 * D_FF), D_MODEL ** -0.5),
        'exp_b1': nrm(ks[18], (DEPTH, N_EXPERTS, 2 * D_FF), 0.01),
        'exp_w2': nrm(ks[19], (DEPTH, N_EXPERTS, D_FF, D_MODEL), D_FF ** -0.5),
        'exp_b2': nrm(ks[20], (DEPTH, N_EXPERTS, D_MODEL), 0.01),
        'final_norm': 1.0 + nrm(ks[21], (D_MODEL,), 0.02),
    }


def reference(x, c, positions, ada_w, ada_b, norm_mix, norm_ffn, w_in, b_forget,
              lambda_q1, lambda_k1, lambda_q2, lambda_k2, diff_subln, w_out,
              router_w, router_b, exp_w1, exp_b1, exp_w2, exp_b2, final_norm):
    cos, sin = rope_tables(positions, x.dtype)
    c_act = jax.nn.silu(c)
    for l in range(DEPTH):
        lam_init = 0.8 - 0.6 * math.exp(-0.3 * l)
        mod = (c_act @ ada_w[l] + ada_b[l])[:, None, :]
        sh1, sc1, g1, sh2, sc2, g2 = jnp.split(mod, 6, axis=-1)
        h = rmsnorm(x, norm_mix[l]) * (1.0 + sc1) + sh1
        x = x + g1 * hybrid_mixer(h, cos, sin, w_in[l], b_forget[l], lambda_q1[l], lambda_k1[l],
                                  lambda_q2[l], lambda_k2[l], diff_subln[l], w_out[l], lam_init)
        h = rmsnorm(x, norm_ffn[l]) * (1.0 + sc2) + sh2
        x = x + g2 * moe_ffn(h, router_w[l], router_b[l], exp_w1[l], exp_b1[l], exp_w2[l], exp_b2[l])
    return rmsnorm(x, final_norm)
```

```python
import functools
import math

import jax
import jax.numpy as jnp
from jax import lax
from jax.experimental import pallas as pl
from jax.experimental.pallas import tpu as pltpu

F32 = jnp.float32
BF16 = jnp.bfloat16
I32 = jnp.int32
HIGHEST = lax.Precision.HIGHEST

HEAD_DIM = 64
LANES = 128
CHUNK = 64
DIFF_HEADS = 4
FOX_HEADS = 8
D_DIFF = DIFF_HEADS * 2 * HEAD_DIM
D_FOX = FOX_HEADS * HEAD_DIM
ROPE_THETA = 500000.0
ROPE_DIM = HEAD_DIM // 4
ROPE_HALF = ROPE_DIM // 2
N_EXPERTS = 32
TOP_K = 4
SWIGLU_LIMIT = 7.0
SWIGLU_ALPHA = 1.702
NORM_EPS = 1e-6
NEG_BIG = -1e30
EXPERT_TILE = 512
VMEM_LIMIT = 56 * 1024 * 1024

_NT = (((1,), (1,)), ((), ()))


def _cparams(sem, vmem=None):
    return pltpu.CompilerParams(dimension_semantics=sem, vmem_limit_bytes=vmem or VMEM_LIMIT)


def _rms(x):
    return x * lax.rsqrt(jnp.mean(x * x, axis=-1, keepdims=True) + NORM_EPS)


def _ada_body(c_ref, w_ref, b_ref, o_ref):
    c = c_ref[...]
    ca = c * jax.nn.sigmoid(c)
    o_ref[0] = jnp.dot(ca, w_ref[0], preferred_element_type=F32, precision=HIGHEST) + b_ref[0]


def _ada_mod(c, ada_w, ada_b):
    depth, d, n6 = ada_w.shape
    b = c.shape[0]
    tn = 1536
    return pl.pallas_call(
        _ada_body,
        grid=(depth, n6 // tn),
        in_specs=[pl.BlockSpec((b, d), lambda l, j: (0, 0)),
                  pl.BlockSpec((1, d, tn), lambda l, j: (l, 0, j)),
                  pl.BlockSpec((1, 1, tn), lambda l, j: (l, 0, j))],
        out_specs=pl.BlockSpec((1, b, tn), lambda l, j: (l, 0, j)),
        out_shape=jax.ShapeDtypeStruct((depth, b, n6), F32),
        compiler_params=_cparams(("arbitrary", "arbitrary")),
        name="ada_mod",
    )(c, ada_w, ada_b.reshape(depth, 1, n6))


def _rope_body(pos_ref, invf_ref, cos_ref, sin_ref):
    ang = pos_ref[0].astype(F32) * invf_ref[...]
    j = lax.broadcasted_iota(I32, ang.shape, 1) & (HEAD_DIM - 1)
    c = jnp.cos(ang)
    s = jnp.sin(ang)
    cos_ref[0] = jnp.where(j < ROPE_DIM, c, 1.0)
    sin_ref[0] = jnp.where(j < ROPE_HALF, -s, jnp.where(j < ROPE_DIM, s, 0.0))


def _rope_tables(positions):
    b, s = positions.shape
    ts = min(512, s)
    inv_freq = ROPE_THETA ** (-jnp.arange(0, ROPE_DIM, 2, dtype=F32) / ROPE_DIM)
    invf = inv_freq[jnp.arange(LANES) % ROPE_HALF].reshape(1, LANES)
    out = jax.ShapeDtypeStruct((b, s, LANES), F32)
    return pl.pallas_call(
        _rope_body,
        grid=(b, s // ts),
        in_specs=[pl.BlockSpec((1, ts, 1), lambda i, j: (i, j, 0)),
                  pl.BlockSpec((1, LANES), lambda i, j: (0, 0))],
        out_specs=[pl.BlockSpec((1, ts, LANES), lambda i, j: (i, j, 0))] * 2,
        out_shape=[out, out],
        compiler_params=_cparams(("arbitrary", "arbitrary")),
        name="rope_tables",
    )(positions.reshape(b, s, 1), invf)


def _inproj_body(x_ref, sc_ref, sh_ref, g_ref, w_ref, bf_ref, cos_ref, sin_ref,
                 dq_ref, dk_ref, dv_ref, fq_ref, fk_ref, fv_ref, fl_ref):
    h = _rms(x_ref[0]) * g_ref[...] * (1.0 + sc_ref[0]) + sh_ref[0]
    hb = h.astype(BF16)
    cosf = cos_ref[0]
    sinf = sin_ref[0]
    lane = lax.broadcasted_iota(I32, cosf.shape, 1)
    first = (lane & (HEAD_DIM - 1)) < ROPE_HALF

    def rope(p):
        nxt = pltpu.roll(p, LANES - ROPE_HALF, 1)
        prv = pltpu.roll(p, ROPE_HALF, 1)
        return p * cosf + jnp.where(first, nxt, prv) * sinf

    def proj(c0):
        return jnp.dot(hb, w_ref[:, c0:c0 + 512], preferred_element_type=F32)

    scale = HEAD_DIM ** -0.5
    p = proj(0)
    for c in range(4):
        sl = slice(c * LANES, (c + 1) * LANES)
        dq_ref[0, :, sl] = (rope(p[:, sl]) * scale).astype(BF16)
    p = proj(512)
    for c in range(4):
        sl = slice(c * LANES, (c + 1) * LANES)
        dk_ref[0, :, sl] = rope(p[:, sl]).astype(BF16)
    dv_ref[0] = proj(1024).astype(BF16)
    fq_ref[0] = (proj(1536) * scale).astype(BF16)
    fk_ref[0] = proj(2048).astype(BF16)
    fv_ref[0] = proj(2560).astype(BF16)
    z = jnp.dot(hb, w_ref[:, 3072:3072 + LANES], preferred_element_type=F32) + bf_ref[...]
    fl_ref[0] = -(jnp.maximum(-z, 0.0) + jnp.log1p(jnp.exp(-jnp.abs(z))))


def _inproj(x, sc, sh, g, w_pad, bf_pad, cosf, sinf):
    b, s, d = x.shape
    tm = min(512, s)
    act = lambda w: jax.ShapeDtypeStruct((b, s, w), BF16)
    row = lambda w: pl.BlockSpec((1, tm, w), lambda i, j: (i, j, 0))
    vec = pl.BlockSpec((1, 1, d), lambda i, j: (i, 0, 0))
    return pl.pallas_call(
        _inproj_body,
        grid=(b, s // tm),
        in_specs=[row(d), vec, vec,
                  pl.BlockSpec((1, d), lambda i, j: (0, 0)),
                  pl.BlockSpec(w_pad.shape, lambda i, j: (0, 0)),
                  pl.BlockSpec((1, LANES), lambda i, j: (0, 0)),
                  row(LANES), row(LANES)],
        out_specs=[row(512)] * 6 + [row(LANES)],
        out_shape=[act(512)] * 6 + [jax.ShapeDtypeStruct((b, s, LANES), F32)],
        compiler_params=_cparams(("arbitrary", "arbitrary")),
        name="in_proj",
    )(x, sc, sh, g, w_pad, bf_pad, cosf, sinf)


def _cum_body(fl_ref, cum_ref, cumt_ref):
    x = fl_ref[0]
    s = x.shape[0]
    row = lax.broadcasted_iota(I32, x.shape, 0)
    d = 1
    while d < s:
        x = x + jnp.where(row >= d, pltpu.roll(x, d, 0), 0.0)
        d *= 2
    cum_ref[0] = x
    cumt_ref[0] = x.T[:FOX_HEADS, :]


def _forget_cumsum(flog):
    b, s, _ = flog.shape
    return pl.pallas_call(
        _cum_body,
        grid=(b,),
        in_specs=[pl.BlockSpec((1, s, LANES), lambda i: (i, 0, 0))],
        out_specs=[pl.BlockSpec((1, s, LANES), lambda i: (i, 0, 0)),
                   pl.BlockSpec((1, FOX_HEADS, s), lambda i: (i, 0, 0))],
        out_shape=[jax.ShapeDtypeStruct((b, s, LANES), F32),
                   jax.ShapeDtypeStruct((b, FOX_HEADS, s), F32)],
        compiler_params=_cparams(("arbitrary",)),
        name="forget_cumsum",
    )(flog)


def _softmax_step(s, vt, m, l, acc):
    m_new = jnp.maximum(m, jnp.max(s, axis=1, keepdims=True))
    alpha = jnp.exp(m - m_new)
    p = jnp.exp(s - m_new)
    l_new = alpha * l + jnp.sum(p, axis=1, keepdims=True)
    acc_new = alpha * acc + jnp.dot(p.astype(BF16), vt, preferred_element_type=F32)
    return m_new, l_new, acc_new


def _init_state(tq):
    return (jnp.full((tq, 1), NEG_BIG, F32), jnp.zeros((tq, 1), F32), jnp.zeros((tq, LANES), F32))


def _diff_body(q_ref, k_ref, v_ref, lq1_ref, lk1_ref, lq2_ref, lk2_ref, g_ref, o_ref, *, tq, lam_init):
    t = pl.program_id(2)
    q = q_ref[0]
    lane = lax.broadcasted_iota(I32, q.shape, 1)
    zero = jnp.zeros_like(q)
    qs = (jnp.where(lane < HEAD_DIM, q, zero), jnp.where(lane >= HEAD_DIM, q, zero))

    def tile(j, state, masked):
        off = pl.multiple_of(j * tq, tq)
        kt = k_ref[0, pl.ds(off, tq), :]
        vt = v_ref[0, pl.ds(off, tq), :]
        out = []
        for mp in range(2):
            s = lax.dot_general(qs[mp], kt, _NT, preferred_element_type=F32)
            if masked:
                r = lax.broadcasted_iota(I32, s.shape, 0)
                c = lax.broadcasted_iota(I32, s.shape, 1)
                shift = CHUNK.bit_length() - 1
                s = jnp.where((c >> shift) <= (r >> shift), s, NEG_BIG)
            out.append(_softmax_step(s, vt, *state[mp]))
        return tuple(out)

    state = (_init_state(tq), _init_state(tq))
    state = lax.fori_loop(0, t, lambda j, st: tile(j, st, False), state)
    (_, l0, a0), (_, l1, a1) = tile(t, state, True)

    lam = (jnp.exp(jnp.sum(lq1_ref[...] * lk1_ref[...], axis=1, keepdims=True))
           - jnp.exp(jnp.sum(lq2_ref[...] * lk2_ref[...], axis=1, keepdims=True)) + lam_init)
    out = a0 / l0 - lam * (a1 / l1)
    o_ref[0] = (_rms(out) * g_ref[...] * (1.0 - lam_init)).astype(o_ref.dtype)


def _diff_attention(dq, dk, dv, lq1, lk1, lq2, lk2, subln, lam_init):
    b, s, _ = dq.shape
    tq = min(256, s)
    qspec = pl.BlockSpec((1, tq, LANES), lambda i, h, t: (i, t, h))
    kvspec = pl.BlockSpec((1, s, LANES), lambda i, h, t: (i, 0, h))
    lspec = pl.BlockSpec((1, HEAD_DIM), lambda i, h, t: (0, 0))
    return pl.pallas_call(
        functools.partial(_diff_body, tq=tq, lam_init=lam_init),
        grid=(b, DIFF_HEADS, s // tq),
        in_specs=[qspec, kvspec, kvspec, lspec, lspec, lspec, lspec,
                  pl.BlockSpec((1, LANES), lambda i, h, t: (0, 0))],
        out_specs=qspec,
        out_shape=jax.ShapeDtypeStruct((b, s, D_DIFF), BF16),
        compiler_params=_cparams(("arbitrary", "arbitrary", "arbitrary")),
        name="diff_attention",
    )(dq, dk, dv, lq1, lk1, lq2, lk2, subln)


def _fox_body(q_ref, k_ref, v_ref, cq_ref, ckt_ref, o_ref, *, tq):
    pair = pl.program_id(1)
    t = pl.program_id(2)
    q = q_ref[0]
    cq_all = cq_ref[0]
    lane = lax.broadcasted_iota(I32, q.shape, 1)
    zero = jnp.zeros_like(q)
    outs = []
    for hh in range(2):
        h = 2 * pair + hh
        in_head = (lane >= hh * HEAD_DIM) & (lane < (hh + 1) * HEAD_DIM)
        qz = jnp.where(in_head, q, zero)
        cq = jnp.sum(jnp.where(lane == h, cq_all, 0.0), axis=1, keepdims=True)

        def tile(j, state, masked, qz=qz, cq=cq, h=h):
            off = pl.multiple_of(j * tq, tq)
            kt = k_ref[0, pl.ds(off, tq), :]
            vt = v_ref[0, pl.ds(off, tq), :]
            ck = ckt_ref[0, pl.ds(h, 1), pl.ds(off, tq)]
            s = lax.dot_general(qz, kt, _NT, preferred_element_type=F32) + cq - ck
            if masked:
                r = lax.broadcasted_iota(I32, s.shape, 0)
                c = lax.broadcasted_iota(I32, s.shape, 1)
                s = jnp.where(c <= r, s, NEG_BIG)
            return _softmax_step(s, vt, *state)

        state = lax.fori_loop(0, t, lambda j, st: tile(j, st, False), _init_state(tq))
        _, l, acc = tile(t, state, True)
        outs.append(acc / l)
    o_ref[0] = jnp.where(lane < HEAD_DIM, outs[0], outs[1]).astype(o_ref.dtype)


def _fox_attention(fq, fk, fv, cum, cumt):
    b, s, _ = fq.shape
    tq = min(256, s)
    qspec = pl.BlockSpec((1, tq, LANES), lambda i, p, t: (i, t, p))
    kvspec = pl.BlockSpec((1, s, LANES), lambda i, p, t: (i, 0, p))
    return pl.pallas_call(
        functools.partial(_fox_body, tq=tq),
        grid=(b, FOX_HEADS // 2, s // tq),
        in_specs=[qspec, kvspec, kvspec,
                  pl.BlockSpec((1, tq, LANES), lambda i, p, t: (i, t, 0)),
                  pl.BlockSpec((1, FOX_HEADS, s), lambda i, p, t: (i, 0, 0))],
        out_specs=qspec,
        out_shape=jax.ShapeDtypeStruct((b, s, D_FOX), BF16),
        compiler_params=_cparams(("arbitrary", "arbitrary", "arbitrary")),
        name="fox_attention",
    )(fq, fk, fv, cum, cumt)


def _outproj_body(x_ref, a_ref, b_ref, w_ref, g1_ref, sc_ref, sh_ref, ng_ref, rw_ref, rb_ref,
                  x1_ref, h2_ref, lg_ref):
    mix = (jnp.dot(a_ref[0], w_ref[:D_DIFF, :], preferred_element_type=F32)
           + jnp.dot(b_ref[0], w_ref[D_DIFF:, :], preferred_element_type=F32))
    x1 = x_ref[0] + g1_ref[0] * mix
    x1_ref[0] = x1
    h2 = _rms(x1) * ng_ref[...] * (1.0 + sc_ref[0]) + sh_ref[0]
    h2_ref[...] = h2
    lg_ref[...] = lax.dot_general(rw_ref[...], h2, _NT, preferred_element_type=F32,
                                  precision=HIGHEST) + rb_ref[...]


def _outproj(x, a_out, b_out, w_out, g1, sc2, sh2, ng, rwt, rb):
    b, s, d = x.shape
    n = b * s
    tm = min(512, s)
    nt = s // tm
    row = lambda w: pl.BlockSpec((1, tm, w), lambda i, j: (i, j, 0))
    vec = pl.BlockSpec((1, 1, d), lambda i, j: (i, 0, 0))
    const = lambda shp: pl.BlockSpec(shp, lambda i, j: (0,) * len(shp))
    return pl.pallas_call(
        _outproj_body,
        grid=(b, nt),
        in_specs=[row(d), row(D_DIFF), row(D_FOX), const(w_out.shape), vec, vec, vec,
                  const((1, d)), const(rwt.shape), const(rb.shape)],
        out_specs=[row(d),
                   pl.BlockSpec((tm, d), lambda i, j: (i * nt + j, 0)),
                   pl.BlockSpec((N_EXPERTS, tm), lambda i, j: (0, i * nt + j))],
        out_shape=[jax.ShapeDtypeStruct((b, s, d), F32),
                   jax.ShapeDtypeStruct((n, d), F32),
                   jax.ShapeDtypeStruct((N_EXPERTS, n), F32)],
        compiler_params=_cparams(("arbitrary", "arbitrary")),
        name="out_proj",
    )(x, a_out, b_out, w_out, g1, sc2, sh2, ng, rwt, rb)


def _route_body(lg_ref, idx_ref, gate_ref, rank_ref, cnt_ref, carry_ref):
    @pl.when(pl.program_id(0) == 0)
    def _():
        carry_ref[...] = jnp.zeros_like(carry_ref)

    work = lg_ref[...]
    tr = work.shape[1]
    eidx = lax.broadcasted_iota(I32, work.shape, 0)
    vals, hots = [], []
    for k in range(TOP_K):
        m = jnp.max(work, axis=0, keepdims=True)
        sel = jnp.min(jnp.where(work == m, eidx, N_EXPERTS), axis=0, keepdims=True)
        hot = eidx == sel
        vals.append(m)
        hots.append(hot)
        idx_ref[k:k + 1, :] = sel
        work = jnp.where(hot, -jnp.inf, work)
    ex = [jnp.exp(v - vals[0]) for v in vals]
    den = ex[0] + ex[1] + ex[2] + ex[3]
    for k in range(TOP_K):
        gate_ref[k:k + 1, :] = ex[k] / den

    chosen = hots[0] | hots[1] | hots[2] | hots[3]
    r = lax.broadcasted_iota(I32, (tr, tr), 0)
    c = lax.broadcasted_iota(I32, (tr, tr), 1)
    before = (r < c).astype(BF16)
    earlier = jnp.dot(chosen.astype(BF16), before, preferred_element_type=F32)
    base = carry_ref[:, 0:1] + earlier
    for k in range(TOP_K):
        rank_ref[k:k + 1, :] = jnp.sum(jnp.where(hots[k], base, 0.0), axis=0, keepdims=True).astype(I32)
    total = carry_ref[...] + jnp.sum(chosen.astype(F32), axis=1, keepdims=True)
    carry_ref[...] = total
    cnt_ref[...] = total


def _route(logits_t):
    e, n = logits_t.shape
    tr = min(512, n)
    tok = pl.BlockSpec((TOP_K, tr), lambda i: (0, i))
    return pl.pallas_call(
        _route_body,
        grid=(n // tr,),
        in_specs=[pl.BlockSpec((e, tr), lambda i: (0, i))],
        out_specs=[tok, tok, tok, pl.BlockSpec((e, LANES), lambda i: (0, 0))],
        out_shape=[jax.ShapeDtypeStruct((TOP_K, n), I32),
                   jax.ShapeDtypeStruct((TOP_K, n), F32),
                   jax.ShapeDtypeStruct((TOP_K, n), I32),
                   jax.ShapeDtypeStruct((e, LANES), F32)],
        scratch_shapes=[pltpu.VMEM((e, LANES), F32)],
        compiler_params=_cparams(("arbitrary",)),
        name="route_topk",
    )(logits_t)


def _dest_body(cnt_ref, idx_ref, rank_ref, dest_ref):
    cnt = cnt_ref[...].astype(I32)
    shift = EXPERT_TILE.bit_length() - 1
    padded = (((cnt + (EXPERT_TILE - 1)) >> shift) << shift).astype(F32)
    r = lax.broadcasted_iota(I32, (N_EXPERTS, N_EXPERTS), 0)
    c = lax.broadcasted_iota(I32, (N_EXPERTS, N_EXPERTS), 1)
    pstart = jnp.dot((c < r).astype(F32), padded, preferred_element_type=F32,
                     precision=HIGHEST)[:, 0:1]
    eidx = lax.broadcasted_iota(I32, (N_EXPERTS, idx_ref.shape[1]), 0)
    for k in range(TOP_K):
        hot = eidx == idx_ref[k:k + 1, :]
        base = jnp.sum(jnp.where(hot, pstart, 0.0), axis=0, keepdims=True)
        dest_ref[k:k + 1, :] = base.astype(I32) + rank_ref[k:k + 1, :]


def _dest_rows(counts, idx_t, rank_t):
    n = idx_t.shape[1]
    tr = min(2048, n)
    tok = pl.BlockSpec((TOP_K, tr), lambda i: (0, i))
    return pl.pallas_call(
        _dest_body,
        grid=(n // tr,),
        in_specs=[pl.BlockSpec(counts.shape, lambda i: (0, 0)), tok, tok],
        out_specs=tok,
        out_shape=jax.ShapeDtypeStruct((TOP_K, n), I32),
        compiler_params=_cparams(("arbitrary",)),
        name="dest_rows",
    )(counts, idx_t, rank_t)


def _dispatch_body(dest_ref, h_hbm, zero_hbm, xbuf_hbm, sem, *, td):
    del zero_hbm
    base = pl.program_id(0) * td

    def copy(a):
        t = a & (td - 1)
        return pltpu.make_async_copy(h_hbm.at[pl.ds(base + t, 1)],
                                     xbuf_hbm.at[pl.ds(dest_ref[0, 0, a], 1)], sem)

    def start(a, carry):
        copy(a).start()
        return carry

    def wait(a, carry):
        copy(a).wait()
        return carry

    lax.fori_loop(0, TOP_K * td, start, 0)
    lax.fori_loop(0, TOP_K * td, wait, 0)


def _tile_major(dest_t, td):
    n = dest_t.shape[1]
    return dest_t.reshape(TOP_K, n // td, td).transpose(1, 0, 2).reshape(n // td, 1, TOP_K * td)


def _dispatch(dest_t, h2, rows):
    n, d = h2.shape
    td = min(256, n)
    return pl.pallas_call(
        functools.partial(_dispatch_body, td=td),
        grid=(n // td,),
        in_specs=[pl.BlockSpec((1, 1, TOP_K * td), lambda i: (i, 0, 0), memory_space=pltpu.SMEM),
                  pl.BlockSpec(memory_space=pl.ANY),
                  pl.BlockSpec(memory_space=pl.ANY)],
        out_specs=pl.BlockSpec(memory_space=pl.ANY),
        out_shape=jax.ShapeDtypeStruct((rows, d), F32),
        scratch_shapes=[pltpu.SemaphoreType.DMA],
        input_output_aliases={2: 0},
        compiler_params=_cparams(("arbitrary",)),
        name="dispatch_rows",
    )(_tile_major(dest_t, td), h2, jnp.zeros((rows, d), F32))


def _expert_body(be_ref, x_ref, w1_ref, b1_ref, w2_ref, b2_ref, y_ref):
    del be_ref
    d_ff = w2_ref.shape[1]
    gu = jnp.dot(x_ref[...].astype(BF16), w1_ref[0], preferred_element_type=F32) + b1_ref[0]
    gate = jnp.minimum(gu[:, :d_ff], SWIGLU_LIMIT)
    up = jnp.clip(gu[:, d_ff:], -SWIGLU_LIMIT, SWIGLU_LIMIT)
    glu = gate * jax.nn.sigmoid(gate * SWIGLU_ALPHA)
    act = ((up + 1.0) * glu).astype(BF16)
    y_ref[...] = jnp.dot(act, w2_ref[0], preferred_element_type=F32) + b2_ref[0]


def _experts(block_expert, xbuf, w1, b1, w2, b2):
    rows, d = xbuf.shape
    e, _, f2 = w1.shape
    f = w2.shape[1]
    grid_spec = pltpu.PrefetchScalarGridSpec(
        num_scalar_prefetch=1,
        grid=(rows // EXPERT_TILE,),
        in_specs=[pl.BlockSpec((EXPERT_TILE, d), lambda i, be: (i, 0)),
                  pl.BlockSpec((1, d, f2), lambda i, be: (be[i], 0, 0)),
                  pl.BlockSpec((1, 1, f2), lambda i, be: (be[i], 0, 0)),
                  pl.BlockSpec((1, f, d), lambda i, be: (be[i], 0, 0)),
                  pl.BlockSpec((1, 1, d), lambda i, be: (be[i], 0, 0))],
        out_specs=pl.BlockSpec((EXPERT_TILE, d), lambda i, be: (i, 0)),
    )
    return pl.pallas_call(
        _expert_body,
        grid_spec=grid_spec,
        out_shape=jax.ShapeDtypeStruct((rows, d), F32),
        compiler_params=_cparams(("arbitrary",)),
        name="expert_swiglu",
    )(block_expert, xbuf, w1, b1.reshape(e, 1, f2), w2, b2.reshape(e, 1, d))


def _combine_body(dest_ref, x1_ref, g2_ref, gate_ref, fg_ref, ybuf_hbm, o_ref, buf, sem, *, tc, final):
    def copy(a):
        k = a // tc
        t = a & (tc - 1)
        return pltpu.make_async_copy(ybuf_hbm.at[pl.ds(dest_ref[0, 0, a], 1)],
                                     buf.at[k, pl.ds(t, 1)], sem)

    def start(a, carry):
        copy(a).start()
        return carry

    def wait(a, carry):
        copy(a).wait()
        return carry

    lax.fori_loop(0, TOP_K * tc, start, 0)
    lax.fori_loop(0, TOP_K * tc, wait, 0)
    gates = gate_ref[...]
    y = gates[:, 0:1] * buf[0]
    for k in range(1, TOP_K):
        y = y + gates[:, k:k + 1] * buf[k]
    out = x1_ref[...] + g2_ref[0] * y
    if final:
        out = _rms(out) * fg_ref[...]
    o_ref[...] = out


def _combine(dest_t, x1, g2, gates, final_g, ybuf, seq, final):
    n, d = x1.shape
    tc = min(256, n, seq)
    return pl.pallas_call(
        functools.partial(_combine_body, tc=tc, final=final),
        grid=(n // tc,),
        in_specs=[pl.BlockSpec((1, 1, TOP_K * tc), lambda i: (i, 0, 0), memory_space=pltpu.SMEM),
                  pl.BlockSpec((tc, d), lambda i: (i, 0)),
                  pl.BlockSpec((1, 1, d), lambda i: ((i * tc) // seq, 0, 0)),
                  pl.BlockSpec((tc, TOP_K), lambda i: (i, 0)),
                  pl.BlockSpec((1, d), lambda i: (0, 0)),
                  pl.BlockSpec(memory_space=pl.ANY)],
        out_specs=pl.BlockSpec((tc, d), lambda i: (i, 0)),
        out_shape=jax.ShapeDtypeStruct((n, d), F32),
        scratch_shapes=[pltpu.VMEM((TOP_K, tc, d), F32), pltpu.SemaphoreType.DMA],
        compiler_params=_cparams(("arbitrary",)),
        name="combine_rows",
    )(_tile_major(dest_t, tc), x1, g2, gates, final_g, ybuf)


def _moe(x1, h2, logits_t, g2, w1, b1, w2, b2, final_g, seq, final):
    n, d = h2.shape
    rows = n * TOP_K + N_EXPERTS * EXPERT_TILE
    n_blocks = rows // EXPERT_TILE
    idx_t, gate_t, rank_t, counts = _route(logits_t)
    dest_t = _dest_rows(counts, idx_t, rank_t)
    cnt = counts[:, 0].astype(I32)
    pend = jnp.cumsum((cnt + EXPERT_TILE - 1) // EXPERT_TILE * EXPERT_TILE)
    block_expert = jnp.minimum(
        jnp.searchsorted(pend, jnp.arange(n_blocks, dtype=I32) * EXPERT_TILE, side='right'),
        N_EXPERTS - 1).astype(I32)
    xbuf = _dispatch(dest_t, h2, rows)
    ybuf = _experts(block_expert, xbuf, w1, b1, w2, b2)
    return _combine(dest_t, x1, g2, gate_t.T, final_g, ybuf, seq, final)


def kernel(x, c, positions, ada_w, ada_b, norm_mix, norm_ffn, w_in, b_forget, lambda_q1, lambda_k1,
           lambda_q2, lambda_k2, diff_subln, w_out, router_w, router_b, exp_w1, exp_b1, exp_w2, exp_b2,
           final_norm):
    b, s, d = x.shape
    depth = ada_w.shape[0]
    mod = _ada_mod(c, ada_w, ada_b)
    cosf, sinf = _rope_tables(positions)
    n_in = w_in.shape[2]
    pad_cols = 3072 + LANES - n_in
    for l in range(depth):
        lam_init = 0.8 - 0.6 * math.exp(-0.3 * l)
        sh1, sc1, g1, sh2, sc2, g2 = [mod[l, :, i * d:(i + 1) * d].reshape(b, 1, d) for i in range(6)]
        w_pad = jnp.pad(w_in[l], ((0, 0), (0, pad_cols))).astype(BF16)
        bf_pad = jnp.pad(b_forget[l], (0, LANES - FOX_HEADS)).reshape(1, LANES)
        dq, dk, dv, fq, fk, fv, flog = _inproj(x, sc1, sh1, norm_mix[l].reshape(1, d), w_pad, bf_pad,
                                               cosf, sinf)
        cum, cumt = _forget_cumsum(flog)
        a_out = _diff_attention(dq, dk, dv, lambda_q1[l].reshape(1, -1), lambda_k1[l].reshape(1, -1),
                                lambda_q2[l].reshape(1, -1), lambda_k2[l].reshape(1, -1),
                                diff_subln[l].reshape(1, -1), lam_init)
        b_out = _fox_attention(fq, fk, fv, cum, cumt)
        x1, h2, logits_t = _outproj(x, a_out, b_out, w_out[l].astype(BF16), g1, sc2, sh2,
                                    norm_ffn[l].reshape(1, d), router_w[l].T,
                                    router_b[l].reshape(-1, 1))
        x = _moe(x1.reshape(b * s, d), h2, logits_t, g2,
                 exp_w1[l].astype(BF16), exp_b1[l], exp_w2[l].astype(BF16), exp_b2[l],
                 final_norm.reshape(1, d), s, l == depth - 1).reshape(b, s, d)
    return x
```

```python
import functools
import math

import jax
import jax.numpy as jnp
from jax import lax
from jax.experimental import pallas as pl
from jax.experimental.pallas import tpu as pltpu

F32 = jnp.float32
BF16 = jnp.bfloat16
I32 = jnp.int32
HIGHEST = lax.Precision.HIGHEST

HEAD_DIM = 64
LANES = 128
SUBLANES = 8
CHUNK = 64
DIFF_HEADS = 4
FOX_HEADS = 8
D_DIFF = DIFF_HEADS * 2 * HEAD_DIM
D_FOX = FOX_HEADS * HEAD_DIM
ROPE_THETA = 500000.0
ROPE_DIM = HEAD_DIM // 4
ROPE_HALF = ROPE_DIM // 2
N_EXPERTS = 32
TOP_K = 4
SWIGLU_LIMIT = 7.0
SWIGLU_ALPHA = 1.702
NORM_EPS = 1e-6
NEG_BIG = -1e30
ATTN_TILE = 512
EXPERT_TILE = 512
ROUTE_TILE = 256
RUN_ALIGN = SUBLANES
VMEM_LIMIT = 56 * 1024 * 1024

_NT = (((1,), (1,)), ((), ()))
_TN = (((0,), (0,)), ((), ()))


def _cparams(sem, vmem=None):
    return pltpu.CompilerParams(dimension_semantics=sem, vmem_limit_bytes=vmem or VMEM_LIMIT)


def _rms(x):
    return x * lax.rsqrt(jnp.mean(x * x, axis=-1, keepdims=True) + NORM_EPS)


def _round_up(x, m):
    return (x + m - 1) // m * m


def _ada_body(c_ref, w_ref, b_ref, o_ref):
    c = c_ref[...]
    ca = c * jax.nn.sigmoid(c)
    o_ref[0] = jnp.dot(ca, w_ref[0], preferred_element_type=F32, precision=HIGHEST) + b_ref[0]


def _ada_mod(c, ada_w, ada_b):
    depth, d, n6 = ada_w.shape
    b = c.shape[0]
    tn = 1536
    return pl.pallas_call(
        _ada_body,
        grid=(depth, n6 // tn),
        in_specs=[pl.BlockSpec((b, d), lambda l, j: (0, 0)),
                  pl.BlockSpec((1, d, tn), lambda l, j: (l, 0, j)),
                  pl.BlockSpec((1, 1, tn), lambda l, j: (l, 0, j))],
        out_specs=pl.BlockSpec((1, b, tn), lambda l, j: (l, 0, j)),
        out_shape=jax.ShapeDtypeStruct((depth, b, n6), F32),
        compiler_params=_cparams(("arbitrary", "arbitrary")),
        name="ada_mod",
    )(c, ada_w, ada_b.reshape(depth, 1, n6))


def _rope_body(pos_ref, invf_ref, cos_ref, sin_ref):
    ang = pos_ref[0].astype(F32) * invf_ref[...]
    j = lax.broadcasted_iota(I32, ang.shape, 1) & (HEAD_DIM - 1)
    c = jnp.cos(ang)
    s = jnp.sin(ang)
    cos_ref[0] = jnp.where(j < ROPE_DIM, c, 1.0)
    sin_ref[0] = jnp.where(j < ROPE_HALF, -s, jnp.where(j < ROPE_DIM, s, 0.0))


def _rope_tables(positions):
    b, s = positions.shape
    ts = min(512, s)
    inv_freq = ROPE_THETA ** (-jnp.arange(0, ROPE_DIM, 2, dtype=F32) / ROPE_DIM)
    invf = inv_freq[jnp.arange(LANES) % ROPE_HALF].reshape(1, LANES)
    out = jax.ShapeDtypeStruct((b, s, LANES), F32)
    return pl.pallas_call(
        _rope_body,
        grid=(b, s // ts),
        in_specs=[pl.BlockSpec((1, ts, 1), lambda i, j: (i, j, 0)),
                  pl.BlockSpec((1, LANES), lambda i, j: (0, 0))],
        out_specs=[pl.BlockSpec((1, ts, LANES), lambda i, j: (i, j, 0))] * 2,
        out_shape=[out, out],
        compiler_params=_cparams(("arbitrary", "arbitrary")),
        name="rope_tables",
    )(positions.reshape(b, s, 1), invf)


def _inproj_body(x_ref, sc_ref, sh_ref, g_ref, w_ref, bf_ref, cos_ref, sin_ref,
                 dq_ref, dk_ref, dv_ref, fq_ref, fk_ref, fv_ref, fl_ref):
    h = _rms(x_ref[0]) * g_ref[...] * (1.0 + sc_ref[0]) + sh_ref[0]
    hb = h.astype(BF16)
    cosf = cos_ref[0]
    sinf = sin_ref[0]
    lane = lax.broadcasted_iota(I32, cosf.shape, 1)
    first = (lane & (HEAD_DIM - 1)) < ROPE_HALF

    def rope(p):
        nxt = pltpu.roll(p, LANES - ROPE_HALF, 1)
        prv = pltpu.roll(p, ROPE_HALF, 1)
        return p * cosf + jnp.where(first, nxt, prv) * sinf

    def proj(c0):
        return jnp.dot(hb, w_ref[:, c0:c0 + 512], preferred_element_type=F32)

    scale = HEAD_DIM ** -0.5
    p = proj(0)
    for c in range(4):
        sl = slice(c * LANES, (c + 1) * LANES)
        dq_ref[0, :, sl] = (rope(p[:, sl]) * scale).astype(BF16)
    p = proj(512)
    for c in range(4):
        sl = slice(c * LANES, (c + 1) * LANES)
        dk_ref[0, :, sl] = rope(p[:, sl]).astype(BF16)
    dv_ref[0] = proj(1024).astype(BF16)
    fq_ref[0] = (proj(1536) * scale).astype(BF16)
    fk_ref[0] = proj(2048).astype(BF16)
    fv_ref[0] = proj(2560).astype(BF16)
    z = jnp.dot(hb, w_ref[:, 3072:3072 + LANES], preferred_element_type=F32) + bf_ref[...]
    fl_ref[0] = -(jnp.maximum(-z, 0.0) + jnp.log1p(jnp.exp(-jnp.abs(z))))


def _inproj(x, sc, sh, g, w_pad, bf_pad, cosf, sinf):
    b, s, d = x.shape
    tm = min(512, s)
    act = lambda w: jax.ShapeDtypeStruct((b, s, w), BF16)
    row = lambda w: pl.BlockSpec((1, tm, w), lambda i, j: (i, j, 0))
    vec = pl.BlockSpec((1, 1, d), lambda i, j: (i, 0, 0))
    return pl.pallas_call(
        _inproj_body,
        grid=(b, s // tm),
        in_specs=[row(d), vec, vec,
                  pl.BlockSpec((1, d), lambda i, j: (0, 0)),
                  pl.BlockSpec(w_pad.shape, lambda i, j: (0, 0)),
                  pl.BlockSpec((1, LANES), lambda i, j: (0, 0)),
                  row(LANES), row(LANES)],
        out_specs=[row(512)] * 6 + [row(LANES)],
        out_shape=[act(512)] * 6 + [jax.ShapeDtypeStruct((b, s, LANES), F32)],
        compiler_params=_cparams(("arbitrary", "arbitrary")),
        name="in_proj",
    )(x, sc, sh, g, w_pad, bf_pad, cosf, sinf)


def _cum_body(fl_ref, cum_ref, cumt_ref):
    x = fl_ref[0]
    s = x.shape[0]
    row = lax.broadcasted_iota(I32, x.shape, 0)
    d = 1
    while d < s:
        x = x + jnp.where(row >= d, pltpu.roll(x, d, 0), 0.0)
        d *= 2
    cum_ref[0] = x
    cumt_ref[0] = x.T[:FOX_HEADS, :]


def _forget_cumsum(flog):
    b, s, _ = flog.shape
    return pl.pallas_call(
        _cum_body,
        grid=(b,),
        in_specs=[pl.BlockSpec((1, s, LANES), lambda i: (i, 0, 0))],
        out_specs=[pl.BlockSpec((1, s, LANES), lambda i: (i, 0, 0)),
                   pl.BlockSpec((1, FOX_HEADS, s), lambda i: (i, 0, 0))],
        out_shape=[jax.ShapeDtypeStruct((b, s, LANES), F32),
                   jax.ShapeDtypeStruct((b, FOX_HEADS, s), F32)],
        compiler_params=_cparams(("arbitrary",)),
        name="forget_cumsum",
    )(flog)


def _softmax_step(s, vt, m, l, acc):
    m_new = jnp.maximum(m, jnp.max(s, axis=1, keepdims=True))
    alpha = jnp.exp(m - m_new)
    p = jnp.exp(s - m_new)
    l_new = alpha * l + jnp.sum(p, axis=1, keepdims=True)
    acc_new = alpha * acc + jnp.dot(p.astype(BF16), vt, preferred_element_type=F32)
    return m_new, l_new, acc_new


def _init_state(tq):
    return (jnp.full((tq, 1), NEG_BIG, F32), jnp.zeros((tq, 1), F32), jnp.zeros((tq, LANES), F32))


def _diff_body(q_ref, k_ref, v_ref, lq1_ref, lk1_ref, lq2_ref, lk2_ref, g_ref, o_ref, *, tq, lam_init):
    t = pl.program_id(2)
    q = q_ref[0]
    lane = lax.broadcasted_iota(I32, q.shape, 1)
    zero = jnp.zeros_like(q)
    qs = (jnp.where(lane < HEAD_DIM, q, zero), jnp.where(lane >= HEAD_DIM, q, zero))

    def tile(j, state, masked):
        off = pl.multiple_of(j * tq, tq)
        kt = k_ref[0, pl.ds(off, tq), :]
        vt = v_ref[0, pl.ds(off, tq), :]
        out = []
        for mp in range(2):
            s = lax.dot_general(qs[mp], kt, _NT, preferred_element_type=F32)
            if masked:
                r = lax.broadcasted_iota(I32, s.shape, 0)
                c = lax.broadcasted_iota(I32, s.shape, 1)
                shift = CHUNK.bit_length() - 1
                s = jnp.where((c >> shift) <= (r >> shift), s, NEG_BIG)
            out.append(_softmax_step(s, vt, *state[mp]))
        return tuple(out)

    state = (_init_state(tq), _init_state(tq))
    state = lax.fori_loop(0, t, lambda j, st: tile(j, st, False), state)
    (_, l0, a0), (_, l1, a1) = tile(t, state, True)

    lam = (jnp.exp(jnp.sum(lq1_ref[...] * lk1_ref[...], axis=1, keepdims=True))
           - jnp.exp(jnp.sum(lq2_ref[...] * lk2_ref[...], axis=1, keepdims=True)) + lam_init)
    out = a0 / l0 - lam * (a1 / l1)
    o_ref[0] = (_rms(out) * g_ref[...] * (1.0 - lam_init)).astype(o_ref.dtype)


def _diff_attention(dq, dk, dv, lq1, lk1, lq2, lk2, subln, lam_init):
    b, s, _ = dq.shape
    tq = min(ATTN_TILE, s)
    qspec = pl.BlockSpec((1, tq, LANES), lambda i, h, t: (i, t, h))
    kvspec = pl.BlockSpec((1, s, LANES), lambda i, h, t: (i, 0, h))
    lspec = pl.BlockSpec((1, HEAD_DIM), lambda i, h, t: (0, 0))
    return pl.pallas_call(
        functools.partial(_diff_body, tq=tq, lam_init=lam_init),
        grid=(b, DIFF_HEADS, s // tq),
        in_specs=[qspec, kvspec, kvspec, lspec, lspec, lspec, lspec,
                  pl.BlockSpec((1, LANES), lambda i, h, t: (0, 0))],
        out_specs=qspec,
        out_shape=jax.ShapeDtypeStruct((b, s, D_DIFF), BF16),
        compiler_params=_cparams(("arbitrary", "arbitrary", "arbitrary")),
        name="diff_attention",
    )(dq, dk, dv, lq1, lk1, lq2, lk2, subln)


def _fox_body(q_ref, k_ref, v_ref, cq_ref, ckt_ref, o_ref, *, tq):
    pair = pl.program_id(1)
    t = pl.program_id(2)
    q = q_ref[0]
    cq_all = cq_ref[0]
    lane = lax.broadcasted_iota(I32, q.shape, 1)
    zero = jnp.zeros_like(q)
    heads = (2 * pair, 2 * pair + 1)
    qs = (jnp.where(lane < HEAD_DIM, q, zero), jnp.where(lane >= HEAD_DIM, q, zero))
    cqs = [jnp.sum(jnp.where(lane == h, cq_all, 0.0), axis=1, keepdims=True) for h in heads]

    def tile(j, state, masked):
        off = pl.multiple_of(j * tq, tq)
        kt = k_ref[0, pl.ds(off, tq), :]
        vt = v_ref[0, pl.ds(off, tq), :]
        out = []
        for hh in range(2):
            ck = ckt_ref[0, pl.ds(heads[hh], 1), pl.ds(off, tq)]
            s = lax.dot_general(qs[hh], kt, _NT, preferred_element_type=F32) + cqs[hh] - ck
            if masked:
                r = lax.broadcasted_iota(I32, s.shape, 0)
                c = lax.broadcasted_iota(I32, s.shape, 1)
                s = jnp.where(c <= r, s, NEG_BIG)
            out.append(_softmax_step(s, vt, *state[hh]))
        return tuple(out)

    state = (_init_state(tq), _init_state(tq))
    state = lax.fori_loop(0, t, lambda j, st: tile(j, st, False), state)
    (_, l0, a0), (_, l1, a1) = tile(t, state, True)
    o_ref[0] = jnp.where(lane < HEAD_DIM, a0 / l0, a1 / l1).astype(o_ref.dtype)


def _fox_attention(fq, fk, fv, cum, cumt):
    b, s, _ = fq.shape
    tq = min(ATTN_TILE, s)
    qspec = pl.BlockSpec((1, tq, LANES), lambda i, p, t: (i, t, p))
    kvspec = pl.BlockSpec((1, s, LANES), lambda i, p, t: (i, 0, p))
    return pl.pallas_call(
        functools.partial(_fox_body, tq=tq),
        grid=(b, FOX_HEADS // 2, s // tq),
        in_specs=[qspec, kvspec, kvspec,
                  pl.BlockSpec((1, tq, LANES), lambda i, p, t: (i, t, 0)),
                  pl.BlockSpec((1, FOX_HEADS, s), lambda i, p, t: (i, 0, 0))],
        out_specs=qspec,
        out_shape=jax.ShapeDtypeStruct((b, s, D_FOX), BF16),
        compiler_params=_cparams(("arbitrary", "arbitrary", "arbitrary")),
        name="fox_attention",
    )(fq, fk, fv, cum, cumt)


def _outproj_body(x_ref, a_ref, b_ref, w_ref, g1_ref, sc_ref, sh_ref, ng_ref, rw_ref, rb_ref,
                  x1_ref, h2_ref, lg_ref):
    mix = (jnp.dot(a_ref[0], w_ref[:D_DIFF, :], preferred_element_type=F32)
           + jnp.dot(b_ref[0], w_ref[D_DIFF:, :], preferred_element_type=F32))
    x1 = x_ref[0] + g1_ref[0] * mix
    x1_ref[0] = x1
    h2 = _rms(x1) * ng_ref[...] * (1.0 + sc_ref[0]) + sh_ref[0]
    h2_ref[...] = h2.astype(h2_ref.dtype)
    lg_ref[...] = lax.dot_general(rw_ref[...], h2, _NT, preferred_element_type=F32,
                                  precision=HIGHEST) + rb_ref[...]


def _outproj(x, a_out, b_out, w_out, g1, sc2, sh2, ng, rwt, rb):
    b, s, d = x.shape
    n = b * s
    tm = min(512, s)
    nt = s // tm
    row = lambda w: pl.BlockSpec((1, tm, w), lambda i, j: (i, j, 0))
    vec = pl.BlockSpec((1, 1, d), lambda i, j: (i, 0, 0))
    const = lambda shp: pl.BlockSpec(shp, lambda i, j: (0,) * len(shp))
    return pl.pallas_call(
        _outproj_body,
        grid=(b, nt),
        in_specs=[row(d), row(D_DIFF), row(D_FOX), const(w_out.shape), vec, vec, vec,
                  const((1, d)), const(rwt.shape), const(rb.shape)],
        out_specs=[row(d),
                   pl.BlockSpec((tm, d), lambda i, j: (i * nt + j, 0)),
                   pl.BlockSpec((N_EXPERTS, tm), lambda i, j: (0, i * nt + j))],
        out_shape=[jax.ShapeDtypeStruct((b, s, d), F32),
                   jax.ShapeDtypeStruct((n, d), BF16),
                   jax.ShapeDtypeStruct((N_EXPERTS, n), F32)],
        compiler_params=_cparams(("arbitrary", "arbitrary")),
        name="out_proj",
    )(x, a_out, b_out, w_out, g1, sc2, sh2, ng, rwt, rb)


def _strict_lower(n):
    r = lax.broadcasted_iota(I32, (n, n), 0)
    c = lax.broadcasted_iota(I32, (n, n), 1)
    return (c < r).astype(F32)


def _align_up(x_f32, m):
    shift = m.bit_length() - 1
    return (((x_f32.astype(I32) + (m - 1)) >> shift) << shift).astype(F32)


def _route_body(lg_ref, pos_ref, gate_ref, cnt_ref):
    i = pl.program_id(0)

    @pl.when(i == 0)
    def _():
        cnt_ref[...] = jnp.zeros_like(cnt_ref)

    work = lg_ref[...]
    td = work.shape[1]
    eidx = lax.broadcasted_iota(I32, work.shape, 0)
    vals, hots = [], []
    for k in range(TOP_K):
        m = jnp.max(work, axis=0, keepdims=True)
        sel = jnp.min(jnp.where(work == m, eidx, N_EXPERTS), axis=0, keepdims=True)
        hot = eidx == sel
        vals.append(m)
        hots.append(hot)
        work = jnp.where(hot, -jnp.inf, work)
    ex = [jnp.exp(v - vals[0]) for v in vals]
    den = ex[0] + ex[1] + ex[2] + ex[3]
    for k in range(TOP_K):
        gate_ref[k:k + 1, :] = ex[k] / den

    chosen = hots[0] | hots[1] | hots[2] | hots[3]
    r = lax.broadcasted_iota(I32, (td, td), 0)
    c = lax.broadcasted_iota(I32, (td, td), 1)
    before = (r < c).astype(BF16)
    earlier = jnp.dot(chosen.astype(BF16), before, preferred_element_type=F32)
    cnt = jnp.sum(chosen.astype(F32), axis=1, keepdims=True)
    run = _align_up(cnt, RUN_ALIGN)
    start = jnp.dot(_strict_lower(N_EXPERTS), jnp.broadcast_to(run, (N_EXPERTS, LANES)),
                    preferred_element_type=F32, precision=HIGHEST)[:, 0:1]
    slot = start + earlier
    for k in range(TOP_K):
        pos_ref[k:k + 1, :] = jnp.sum(jnp.where(hots[k], slot, 0.0), axis=0, keepdims=True).astype(I32)
    lane = lax.broadcasted_iota(I32, cnt_ref.shape, 1)
    cnt_ref[...] = jnp.where(lane == i, run, cnt_ref[...])


def _route(logits_t):
    e, n = logits_t.shape
    td = min(ROUTE_TILE, n)
    ntp = _round_up(n // td, LANES)
    tok = pl.BlockSpec((TOP_K, td), lambda i: (0, i))
    return pl.pallas_call(
        _route_body,
        grid=(n // td,),
        in_specs=[pl.BlockSpec((e, td), lambda i: (0, i))],
        out_specs=[tok, tok, pl.BlockSpec((e, ntp), lambda i: (0, 0))],
        out_shape=[jax.ShapeDtypeStruct((TOP_K, n), I32),
                   jax.ShapeDtypeStruct((TOP_K, n), F32),
                   jax.ShapeDtypeStruct((e, ntp), F32)],
        compiler_params=_cparams(("arbitrary",)),
        name="route_topk",
    )(logits_t)


def _layout_body(run_ref, base_ref, be_ref, nu_ref):
    run = run_ref[...]
    ntp = run.shape[1]
    region = _align_up(jnp.sum(run, axis=1, keepdims=True), EXPERT_TILE)
    pstart = jnp.dot(_strict_lower(N_EXPERTS), jnp.broadcast_to(region, (N_EXPERTS, LANES)),
                     preferred_element_type=F32, precision=HIGHEST)[:, 0:1]
    ti = lax.broadcasted_iota(I32, (ntp, ntp), 0)
    tj = lax.broadcasted_iota(I32, (ntp, ntp), 1)
    within = jnp.dot(run, (ti < tj).astype(F32), preferred_element_type=F32, precision=HIGHEST)
    base_ref[...] = (pstart + within).astype(I32)
    pend = pstart + region
    blk = (lax.broadcasted_iota(I32, (N_EXPERTS, be_ref.shape[1]), 1) * EXPERT_TILE).astype(F32)
    be = jnp.sum((pend <= blk).astype(I32), axis=0, keepdims=True)
    be_ref[...] = jnp.minimum(be, N_EXPERTS - 1)
    used = jnp.sum(region, axis=0, keepdims=True).astype(I32) >> (EXPERT_TILE.bit_length() - 1)
    nu_ref[...] = jnp.broadcast_to(used, nu_ref.shape)


def _layout(run_t, n_blocks):
    e, ntp = run_t.shape
    nbp = _round_up(n_blocks, LANES)
    full = lambda shp: pl.BlockSpec(shp, lambda: (0,) * len(shp))
    return pl.pallas_call(
        _layout_body,
        in_specs=[full((e, ntp))],
        out_specs=[full((e, ntp)), full((1, nbp)), full((1, LANES))],
        out_shape=[jax.ShapeDtypeStruct((e, ntp), I32),
                   jax.ShapeDtypeStruct((1, nbp), I32),
                   jax.ShapeDtypeStruct((1, LANES), I32)],
        name="expert_layout",
    )(run_t)


_RUN_LEVELS = tuple(1 << b for b in range(ROUTE_TILE.bit_length() - 1, RUN_ALIGN.bit_length() - 2, -1))


def _for_each_run(run_tbl, base_tbl, tile, make_copy, start):
    def body(e, off):
        cnt = run_tbl[tile * N_EXPERTS + e]
        base = base_tbl[tile * N_EXPERTS + e]
        done = jnp.int32(0)
        for lvl in _RUN_LEVELS:
            bit = cnt & lvl

            @pl.when(bit != 0)
            def _(done=done, lvl=lvl):
                cp = make_copy(pl.multiple_of(off + done, RUN_ALIGN), pl.multiple_of(base + done, RUN_ALIGN), lvl)
                if start:
                    cp.start()
                else:
                    cp.wait()
            done = done + bit
        return off + cnt

    lax.fori_loop(0, N_EXPERTS, body, jnp.int32(0))


def _slot_onehot(pos, sb):
    j = lax.broadcasted_iota(I32, (sb, pos.shape[1]), 0)
    hits = [j == pos[k:k + 1, :] for k in range(TOP_K)]
    return hits, (hits[0] | hits[1] | hits[2] | hits[3])


def _sorted_rows(td):
    return _round_up(TOP_K * td + N_EXPERTS * (RUN_ALIGN - 1), LANES)


def _dispatch_body(run_tbl, base_tbl, h_ref, pos_ref, gate_ref, zero_hbm, xbuf_hbm, sorted_ref, sem, *, d):
    del zero_hbm
    i = pl.program_id(0)
    last = pl.num_programs(0) - 1
    slot = i % 2
    sb = sorted_ref.shape[1]
    hits, any_hit = _slot_onehot(pos_ref[...], sb)
    perm = jnp.where(any_hit, 1.0, 0.0).astype(BF16)
    sorted_ref[slot, :, :d] = jnp.dot(perm, h_ref[...], preferred_element_type=F32)
    gates = gate_ref[...]
    gsel = jnp.where(hits[0], gates[0:1, :], 0.0)
    for k in range(1, TOP_K):
        gsel = gsel + jnp.where(hits[k], gates[k:k + 1, :], 0.0)
    sorted_ref[slot, :, d:] = jnp.broadcast_to(jnp.sum(gsel, axis=1, keepdims=True), (sb, LANES))

    def copies(slot):
        return lambda off, base, rows: pltpu.make_async_copy(
            sorted_ref.at[slot, pl.ds(off, rows)], xbuf_hbm.at[pl.ds(base, rows)], sem.at[slot])

    _for_each_run(run_tbl, base_tbl, i, copies(slot), True)

    @pl.when(i > 0)
    def _():
        _for_each_run(run_tbl, base_tbl, i - 1, copies(1 - slot), False)

    @pl.when(i == last)
    def _():
        _for_each_run(run_tbl, base_tbl, i, copies(slot), False)


def _dispatch(run_tbl, base_tbl, h2, pos_t, gate_t, rows):
    n, d = h2.shape
    td = min(ROUTE_TILE, n)
    sb = _sorted_rows(td)
    dx = d + LANES
    tok = pl.BlockSpec((TOP_K, td), lambda i, *_: (0, i))
    grid_spec = pltpu.PrefetchScalarGridSpec(
        num_scalar_prefetch=2,
        grid=(n // td,),
        in_specs=[pl.BlockSpec((td, d), lambda i, *_: (i, 0)), tok, tok,
                  pl.BlockSpec(memory_space=pl.ANY)],
        out_specs=pl.BlockSpec(memory_space=pl.ANY),
        scratch_shapes=[pltpu.VMEM((2, sb, dx), F32), pltpu.SemaphoreType.DMA((2,))],
    )
    return pl.pallas_call(
        functools.partial(_dispatch_body, d=d),
        grid_spec=grid_spec,
        out_shape=jax.ShapeDtypeStruct((rows, dx), F32),
        input_output_aliases={5: 0},
        compiler_params=_cparams(("arbitrary",)),
        name="dispatch_rows",
    )(run_tbl, base_tbl, h2, pos_t, gate_t, jnp.zeros((rows, dx), F32))


def _expert_body(be_ref, nu_ref, x_ref, w1_ref, b1_ref, w2_ref, b2_ref, y_ref):
    del be_ref
    d_ff, d = w2_ref.shape[1], w2_ref.shape[2]

    @pl.when(pl.program_id(0) < nu_ref[0])
    def _():
        gu = jnp.dot(x_ref[:, :d].astype(BF16), w1_ref[0], preferred_element_type=F32) + b1_ref[0]
        gate = jnp.minimum(gu[:, :d_ff], SWIGLU_LIMIT)
        up = jnp.clip(gu[:, d_ff:], -SWIGLU_LIMIT, SWIGLU_LIMIT)
        glu = gate * jax.nn.sigmoid(gate * SWIGLU_ALPHA)
        act = ((up + 1.0) * glu).astype(BF16)
        y = jnp.dot(act, w2_ref[0], preferred_element_type=F32) + b2_ref[0]
        y_ref[...] = x_ref[:, d:d + 1] * y

    @pl.when(pl.program_id(0) >= nu_ref[0])
    def _():
        y_ref[...] = jnp.zeros_like(y_ref)


def _experts(block_expert, n_used, xbuf, w1, b1, w2, b2):
    rows, dx = xbuf.shape
    e, d, f2 = w1.shape
    f = w2.shape[1]
    blk = lambda i, be, nu: (jnp.minimum(i, nu[0] - 1), 0)
    wsel = lambda i, be, nu: (be[i], 0, 0)
    grid_spec = pltpu.PrefetchScalarGridSpec(
        num_scalar_prefetch=2,
        grid=(rows // EXPERT_TILE,),
        in_specs=[pl.BlockSpec((EXPERT_TILE, dx), blk),
                  pl.BlockSpec((1, d, f2), wsel),
                  pl.BlockSpec((1, 1, f2), wsel),
                  pl.BlockSpec((1, f, d), wsel),
                  pl.BlockSpec((1, 1, d), wsel)],
        out_specs=pl.BlockSpec((EXPERT_TILE, d), lambda i, be, nu: (i, 0)),
    )
    return pl.pallas_call(
        _expert_body,
        grid_spec=grid_spec,
        out_shape=jax.ShapeDtypeStruct((rows, d), F32),
        compiler_params=_cparams(("arbitrary",)),
        name="expert_swiglu",
    )(block_expert, n_used, xbuf, w1, b1.reshape(e, 1, f2), w2, b2.reshape(e, 1, d))


def _combine_body(run_tbl, base_tbl, x1_ref, g2_ref, pos_ref, fg_ref, ybuf_hbm, o_ref, ys_ref, sem, *, final):
    i = pl.program_id(0)
    last = pl.num_programs(0) - 1
    slot = i % 2
    sb = ys_ref.shape[1]

    def copies(slot):
        return lambda off, base, rows: pltpu.make_async_copy(
            ybuf_hbm.at[pl.ds(base, rows)], ys_ref.at[slot, pl.ds(off, rows)], sem.at[slot])

    @pl.when(i == 0)
    def _():
        ys_ref[...] = jnp.zeros_like(ys_ref)
        _for_each_run(run_tbl, base_tbl, 0, copies(0), True)

    @pl.when(i < last)
    def _():
        _for_each_run(run_tbl, base_tbl, i + 1, copies(1 - slot), True)

    _for_each_run(run_tbl, base_tbl, i, copies(slot), False)

    _, any_hit = _slot_onehot(pos_ref[...], sb)
    perm = jnp.where(any_hit, 1.0, 0.0).astype(BF16)
    y = ys_ref[slot]
    y_hi = y.astype(BF16)
    y_lo = (y - y_hi.astype(F32)).astype(BF16)
    moe = (lax.dot_general(perm, y_hi, _TN, preferred_element_type=F32)
           + lax.dot_general(perm, y_lo, _TN, preferred_element_type=F32))
    out = x1_ref[...] + g2_ref[0] * moe
    if final:
        out = _rms(out) * fg_ref[...]
    o_ref[...] = out


def _combine(run_tbl, base_tbl, x1, g2, pos_t, final_g, ybuf, seq, final):
    n, d = x1.shape
    td = min(ROUTE_TILE, n)
    sb = _sorted_rows(td)
    grid_spec = pltpu.PrefetchScalarGridSpec(
        num_scalar_prefetch=2,
        grid=(n // td,),
        in_specs=[pl.BlockSpec((td, d), lambda i, *_: (i, 0)),
                  pl.BlockSpec((1, 1, d), lambda i, *_: ((i * td) // seq, 0, 0)),
                  pl.BlockSpec((TOP_K, td), lambda i, *_: (0, i)),
                  pl.BlockSpec((1, d), lambda i, *_: (0, 0)),
                  pl.BlockSpec(memory_space=pl.ANY)],
        out_specs=pl.BlockSpec((td, d), lambda i, *_: (i, 0)),
        scratch_shapes=[pltpu.VMEM((2, sb, d), F32), pltpu.SemaphoreType.DMA((2,))],
    )
    return pl.pallas_call(
        functools.partial(_combine_body, final=final),
        grid_spec=grid_spec,
        out_shape=jax.ShapeDtypeStruct((n, d), F32),
        compiler_params=_cparams(("arbitrary",)),
        name="combine_rows",
    )(run_tbl, base_tbl, x1, g2, pos_t, final_g, ybuf)


def _moe(x1, h2, logits_t, g2, w1, b1, w2, b2, final_g, seq, final):
    n, d = h2.shape
    td = min(ROUTE_TILE, n)
    n_tiles = n // td
    assert seq % td == 0
    rows = (_round_up(n * TOP_K + n_tiles * N_EXPERTS * (RUN_ALIGN - 1), EXPERT_TILE)
            + N_EXPERTS * EXPERT_TILE)
    n_blocks = rows // EXPERT_TILE
    pos_t, gate_t, run_t = _route(logits_t)
    base_t, be, nu = _layout(run_t, n_blocks)
    run_tbl = run_t[:, :n_tiles].T.astype(I32).reshape(-1)
    base_tbl = base_t[:, :n_tiles].T.reshape(-1)
    xbuf = _dispatch(run_tbl, base_tbl, h2, pos_t, gate_t, rows)
    ybuf = _experts(be[0, :n_blocks], nu[0, :1], xbuf, w1, b1, w2, b2)
    return _combine(run_tbl, base_tbl, x1, g2, pos_t, final_g, ybuf, seq, final)


def kernel(x, c, positions, ada_w, ada_b, norm_mix, norm_ffn, w_in, b_forget, lambda_q1, lambda_k1,
           lambda_q2, lambda_k2, diff_subln, w_out, router_w, router_b, exp_w1, exp_b1, exp_w2, exp_b2,
           final_norm):
    b, s, d = x.shape
    depth = ada_w.shape[0]
    mod = _ada_mod(c, ada_w, ada_b)
    cosf, sinf = _rope_tables(positions)
    n_in = w_in.shape[2]
    pad_cols = 3072 + LANES - n_in
    for l in range(depth):
        lam_init = 0.8 - 0.6 * math.exp(-0.3 * l)
        sh1, sc1, g1, sh2, sc2, g2 = [mod[l, :, i * d:(i + 1) * d].reshape(b, 1, d) for i in range(6)]
        w_pad = jnp.pad(w_in[l], ((0, 0), (0, pad_cols))).astype(BF16)
        bf_pad = jnp.pad(b_forget[l], (0, LANES - FOX_HEADS)).reshape(1, LANES)
        dq, dk, dv, fq, fk, fv, flog = _inproj(x, sc1, sh1, norm_mix[l].reshape(1, d), w_pad, bf_pad,
                                               cosf, sinf)
        cum, cumt = _forget_cumsum(flog)
        a_out = _diff_attention(dq, dk, dv, lambda_q1[l].reshape(1, -1), lambda_k1[l].reshape(1, -1),
                                lambda_q2[l].reshape(1, -1), lambda_k2[l].reshape(1, -1),
                                diff_subln[l].reshape(1, -1), lam_init)
        b_out = _fox_attention(fq, fk, fv, cum, cumt)
        x1, h2, logits_t = _outproj(x, a_out, b_out, w_out[l].astype(BF16), g1, sc2, sh2,
                                    norm_ffn[l].reshape(1, d), router_w[l].T,
                                    router_b[l].reshape(-1, 1))
        x = _moe(x1.reshape(b * s, d), h2, logits_t, g2,
                 exp_w1[l].astype(BF16), exp_b1[l], exp_w2[l].astype(BF16), exp_b2[l],
                 final_norm.reshape(1, d), s, l == depth - 1).reshape(b, s, d)
    return x
```

```python
import functools
import math

import jax
import jax.numpy as jnp
from jax import lax
from jax.experimental import pallas as pl
from jax.experimental.pallas import tpu as pltpu

F32 = jnp.float32
BF16 = jnp.bfloat16
I32 = jnp.int32
HIGHEST = lax.Precision.HIGHEST

HEAD_DIM = 64
LANES = 128
SUBLANES = 8
CHUNK = 64
DIFF_HEADS = 4
FOX_HEADS = 8
D_DIFF = DIFF_HEADS * 2 * HEAD_DIM
D_FOX = FOX_HEADS * HEAD_DIM
ROPE_THETA = 500000.0
ROPE_DIM = HEAD_DIM // 4
ROPE_HALF = ROPE_DIM // 2
N_EXPERTS = 32
TOP_K = 4
SWIGLU_LIMIT = 7.0
SWIGLU_ALPHA = 1.702
NORM_EPS = 1e-6
NEG_BIG = -1e30
ATTN_TILE = 512
EXPERT_TILE = 512
ROUTE_TILE = 256
RUN_ALIGN = SUBLANES
VMEM_LIMIT = 56 * 1024 * 1024

_NT = (((1,), (1,)), ((), ()))
_TN = (((0,), (0,)), ((), ()))


def _cparams(sem, vmem=None):
    return pltpu.CompilerParams(dimension_semantics=sem, vmem_limit_bytes=vmem or VMEM_LIMIT)


def _rms(x):
    return x * lax.rsqrt(jnp.mean(x * x, axis=-1, keepdims=True) + NORM_EPS)


def _round_up(x, m):
    return (x + m - 1) // m * m


def _ada_body(c_ref, w_ref, b_ref, o_ref):
    c = c_ref[...]
    ca = c * jax.nn.sigmoid(c)
    o_ref[0] = jnp.dot(ca, w_ref[0], preferred_element_type=F32, precision=HIGHEST) + b_ref[0]


def _ada_mod(c, ada_w, ada_b):
    depth, d, n6 = ada_w.shape
    b = c.shape[0]
    tn = 1536
    return pl.pallas_call(
        _ada_body,
        grid=(depth, n6 // tn),
        in_specs=[pl.BlockSpec((b, d), lambda l, j: (0, 0)),
                  pl.BlockSpec((1, d, tn), lambda l, j: (l, 0, j)),
                  pl.BlockSpec((1, 1, tn), lambda l, j: (l, 0, j))],
        out_specs=pl.BlockSpec((1, b, tn), lambda l, j: (l, 0, j)),
        out_shape=jax.ShapeDtypeStruct((depth, b, n6), F32),
        compiler_params=_cparams(("arbitrary", "arbitrary")),
        name="ada_mod",
    )(c, ada_w, ada_b.reshape(depth, 1, n6))


def _rope_body(pos_ref, invf_ref, cos_ref, sin_ref):
    ang = pos_ref[0].astype(F32) * invf_ref[...]
    j = lax.broadcasted_iota(I32, ang.shape, 1) & (HEAD_DIM - 1)
    c = jnp.cos(ang)
    s = jnp.sin(ang)
    cos_ref[0] = jnp.where(j < ROPE_DIM, c, 1.0)
    sin_ref[0] = jnp.where(j < ROPE_HALF, -s, jnp.where(j < ROPE_DIM, s, 0.0))


def _rope_tables(positions):
    b, s = positions.shape
    ts = min(512, s)
    inv_freq = ROPE_THETA ** (-jnp.arange(0, ROPE_DIM, 2, dtype=F32) / ROPE_DIM)
    invf = inv_freq[jnp.arange(LANES) % ROPE_HALF].reshape(1, LANES)
    out = jax.ShapeDtypeStruct((b, s, LANES), F32)
    return pl.pallas_call(
        _rope_body,
        grid=(b, s // ts),
        in_specs=[pl.BlockSpec((1, ts, 1), lambda i, j: (i, j, 0)),
                  pl.BlockSpec((1, LANES), lambda i, j: (0, 0))],
        out_specs=[pl.BlockSpec((1, ts, LANES), lambda i, j: (i, j, 0))] * 2,
        out_shape=[out, out],
        compiler_params=_cparams(("arbitrary", "arbitrary")),
        name="rope_tables",
    )(positions.reshape(b, s, 1), invf)


def _inproj_body(x_ref, sc_ref, sh_ref, g_ref, w_ref, bf_ref, cos_ref, sin_ref,
                 dq_ref, dk_ref, dv_ref, fq_ref, fk_ref, fv_ref, fl_ref):
    h = _rms(x_ref[0]) * g_ref[...] * (1.0 + sc_ref[0]) + sh_ref[0]
    hb = h.astype(BF16)
    cosf = cos_ref[0]
    sinf = sin_ref[0]
    lane = lax.broadcasted_iota(I32, cosf.shape, 1)
    first = (lane & (HEAD_DIM - 1)) < ROPE_HALF

    def rope(p):
        nxt = pltpu.roll(p, LANES - ROPE_HALF, 1)
        prv = pltpu.roll(p, ROPE_HALF, 1)
        return p * cosf + jnp.where(first, nxt, prv) * sinf

    def proj(c0):
        return jnp.dot(hb, w_ref[:, c0:c0 + 512], preferred_element_type=F32)

    scale = HEAD_DIM ** -0.5
    p = proj(0)
    for c in range(4):
        sl = slice(c * LANES, (c + 1) * LANES)
        dq_ref[0, :, sl] = (rope(p[:, sl]) * scale).astype(BF16)
    p = proj(512)
    for c in range(4):
        sl = slice(c * LANES, (c + 1) * LANES)
        dk_ref[0, :, sl] = rope(p[:, sl]).astype(BF16)
    dv_ref[0] = proj(1024).astype(BF16)
    fq_ref[0] = (proj(1536) * scale).astype(BF16)
    fk_ref[0] = proj(2048).astype(BF16)
    fv_ref[0] = proj(2560).astype(BF16)
    z = jnp.dot(hb, w_ref[:, 3072:3072 + LANES], preferred_element_type=F32) + bf_ref[...]
    fl_ref[0] = -(jnp.maximum(-z, 0.0) + jnp.log1p(jnp.exp(-jnp.abs(z))))


def _inproj(x, sc, sh, g, w_pad, bf_pad, cosf, sinf):
    b, s, d = x.shape
    tm = min(512, s)
    act = lambda w: jax.ShapeDtypeStruct((b, s, w), BF16)
    row = lambda w: pl.BlockSpec((1, tm, w), lambda i, j: (i, j, 0))
    vec = pl.BlockSpec((1, 1, d), lambda i, j: (i, 0, 0))
    return pl.pallas_call(
        _inproj_body,
        grid=(b, s // tm),
        in_specs=[row(d), vec, vec,
                  pl.BlockSpec((1, d), lambda i, j: (0, 0)),
                  pl.BlockSpec(w_pad.shape, lambda i, j: (0, 0)),
                  pl.BlockSpec((1, LANES), lambda i, j: (0, 0)),
                  row(LANES), row(LANES)],
        out_specs=[row(512)] * 6 + [row(LANES)],
        out_shape=[act(512)] * 6 + [jax.ShapeDtypeStruct((b, s, LANES), F32)],
        compiler_params=_cparams(("arbitrary", "arbitrary")),
        name="in_proj",
    )(x, sc, sh, g, w_pad, bf_pad, cosf, sinf)


def _cum_body(fl_ref, cum_ref, cumt_ref):
    x = fl_ref[0]
    s = x.shape[0]
    row = lax.broadcasted_iota(I32, x.shape, 0)
    d = 1
    while d < s:
        x = x + jnp.where(row >= d, pltpu.roll(x, d, 0), 0.0)
        d *= 2
    cum_ref[0] = x
    cumt_ref[0] = x.T[:FOX_HEADS, :]


def _forget_cumsum(flog):
    b, s, _ = flog.shape
    return pl.pallas_call(
        _cum_body,
        grid=(b,),
        in_specs=[pl.BlockSpec((1, s, LANES), lambda i: (i, 0, 0))],
        out_specs=[pl.BlockSpec((1, s, LANES), lambda i: (i, 0, 0)),
                   pl.BlockSpec((1, FOX_HEADS, s), lambda i: (i, 0, 0))],
        out_shape=[jax.ShapeDtypeStruct((b, s, LANES), F32),
                   jax.ShapeDtypeStruct((b, FOX_HEADS, s), F32)],
        compiler_params=_cparams(("arbitrary",)),
        name="forget_cumsum",
    )(flog)


def _softmax_step(s, vt, m, l, acc):
    m_new = jnp.maximum(m, jnp.max(s, axis=1, keepdims=True))
    alpha = jnp.exp(m - m_new)
    p = jnp.exp(s - m_new)
    l_new = alpha * l + jnp.sum(p, axis=1, keepdims=True)
    acc_new = alpha * acc + jnp.dot(p.astype(BF16), vt, preferred_element_type=F32)
    return m_new, l_new, acc_new


def _init_state(tq):
    return (jnp.full((tq, 1), NEG_BIG, F32), jnp.zeros((tq, 1), F32), jnp.zeros((tq, LANES), F32))


def _diff_body(q_ref, k_ref, v_ref, lq1_ref, lk1_ref, lq2_ref, lk2_ref, g_ref, o_ref, *, tq, lam_init):
    t = pl.program_id(2)
    q = q_ref[0]
    lane = lax.broadcasted_iota(I32, q.shape, 1)
    zero = jnp.zeros_like(q)
    qs = (jnp.where(lane < HEAD_DIM, q, zero), jnp.where(lane >= HEAD_DIM, q, zero))

    def tile(j, state, masked):
        off = pl.multiple_of(j * tq, tq)
        kt = k_ref[0, pl.ds(off, tq), :]
        vt = v_ref[0, pl.ds(off, tq), :]
        out = []
        for mp in range(2):
            s = lax.dot_general(qs[mp], kt, _NT, preferred_element_type=F32)
            if masked:
                r = lax.broadcasted_iota(I32, s.shape, 0)
                c = lax.broadcasted_iota(I32, s.shape, 1)
                shift = CHUNK.bit_length() - 1
                s = jnp.where((c >> shift) <= (r >> shift), s, NEG_BIG)
            out.append(_softmax_step(s, vt, *state[mp]))
        return tuple(out)

    state = (_init_state(tq), _init_state(tq))
    state = lax.fori_loop(0, t, lambda j, st: tile(j, st, False), state)
    (_, l0, a0), (_, l1, a1) = tile(t, state, True)

    lam = (jnp.exp(jnp.sum(lq1_ref[...] * lk1_ref[...], axis=1, keepdims=True))
           - jnp.exp(jnp.sum(lq2_ref[...] * lk2_ref[...], axis=1, keepdims=True)) + lam_init)
    out = a0 / l0 - lam * (a1 / l1)
    o_ref[0] = (_rms(out) * g_ref[...] * (1.0 - lam_init)).astype(o_ref.dtype)


def _diff_attention(dq, dk, dv, lq1, lk1, lq2, lk2, subln, lam_init):
    b, s, _ = dq.shape
    tq = min(ATTN_TILE, s)
    qspec = pl.BlockSpec((1, tq, LANES), lambda i, h, t: (i, t, h))
    kvspec = pl.BlockSpec((1, s, LANES), lambda i, h, t: (i, 0, h))
    lspec = pl.BlockSpec((1, HEAD_DIM), lambda i, h, t: (0, 0))
    return pl.pallas_call(
        functools.partial(_diff_body, tq=tq, lam_init=lam_init),
        grid=(b, DIFF_HEADS, s // tq),
        in_specs=[qspec, kvspec, kvspec, lspec, lspec, lspec, lspec,
                  pl.BlockSpec((1, LANES), lambda i, h, t: (0, 0))],
        out_specs=qspec,
        out_shape=jax.ShapeDtypeStruct((b, s, D_DIFF), BF16),
        compiler_params=_cparams(("arbitrary", "arbitrary", "arbitrary")),
        name="diff_attention",
    )(dq, dk, dv, lq1, lk1, lq2, lk2, subln)


def _fox_body(q_ref, k_ref, v_ref, cq_ref, ckt_ref, o_ref, *, tq):
    pair = pl.program_id(1)
    t = pl.program_id(2)
    q = q_ref[0]
    cq_all = cq_ref[0]
    lane = lax.broadcasted_iota(I32, q.shape, 1)
    zero = jnp.zeros_like(q)
    heads = (2 * pair, 2 * pair + 1)
    qs = (jnp.where(lane < HEAD_DIM, q, zero), jnp.where(lane >= HEAD_DIM, q, zero))
    cqs = [jnp.sum(jnp.where(lane == h, cq_all, 0.0), axis=1, keepdims=True) for h in heads]

    def tile(j, state, masked):
        off = pl.multiple_of(j * tq, tq)
        kt = k_ref[0, pl.ds(off, tq), :]
        vt = v_ref[0, pl.ds(off, tq), :]
        out = []
        for hh in range(2):
            ck = ckt_ref[0, pl.ds(heads[hh], 1), pl.ds(off, tq)]
            s = lax.dot_general(qs[hh], kt, _NT, preferred_element_type=F32) + cqs[hh] - ck
            if masked:
                r = lax.broadcasted_iota(I32, s.shape, 0)
                c = lax.broadcasted_iota(I32, s.shape, 1)
                s = jnp.where(c <= r, s, NEG_BIG)
            out.append(_softmax_step(s, vt, *state[hh]))
        return tuple(out)

    state = (_init_state(tq), _init_state(tq))
    state = lax.fori_loop(0, t, lambda j, st: tile(j, st, False), state)
    (_, l0, a0), (_, l1, a1) = tile(t, state, True)
    o_ref[0] = jnp.where(lane < HEAD_DIM, a0 / l0, a1 / l1).astype(o_ref.dtype)


def _fox_attention(fq, fk, fv, cum, cumt):
    b, s, _ = fq.shape
    tq = min(ATTN_TILE, s)
    qspec = pl.BlockSpec((1, tq, LANES), lambda i, p, t: (i, t, p))
    kvspec = pl.BlockSpec((1, s, LANES), lambda i, p, t: (i, 0, p))
    return pl.pallas_call(
        functools.partial(_fox_body, tq=tq),
        grid=(b, FOX_HEADS // 2, s // tq),
        in_specs=[qspec, kvspec, kvspec,
                  pl.BlockSpec((1, tq, LANES), lambda i, p, t: (i, t, 0)),
                  pl.BlockSpec((1, FOX_HEADS, s), lambda i, p, t: (i, 0, 0))],
        out_specs=qspec,
        out_shape=jax.ShapeDtypeStruct((b, s, D_FOX), BF16),
        compiler_params=_cparams(("arbitrary", "arbitrary", "arbitrary")),
        name="fox_attention",
    )(fq, fk, fv, cum, cumt)


def _outproj_body(x_ref, a_ref, b_ref, w_ref, g1_ref, sc_ref, sh_ref, ng_ref, rw_ref, rb_ref,
                  x1_ref, h2_ref, lg_ref):
    mix = (jnp.dot(a_ref[0], w_ref[:D_DIFF, :], preferred_element_type=F32)
           + jnp.dot(b_ref[0], w_ref[D_DIFF:, :], preferred_element_type=F32))
    x1 = x_ref[0] + g1_ref[0] * mix
    x1_ref[0] = x1
    h2 = _rms(x1) * ng_ref[...] * (1.0 + sc_ref[0]) + sh_ref[0]
    h2_ref[...] = h2.astype(h2_ref.dtype)
    lg_ref[...] = lax.dot_general(rw_ref[...], h2, _NT, preferred_element_type=F32,
                                  precision=HIGHEST) + rb_ref[...]


def _outproj(x, a_out, b_out, w_out, g1, sc2, sh2, ng, rwt, rb):
    b, s, d = x.shape
    n = b * s
    tm = min(512, s)
    nt = s // tm
    row = lambda w: pl.BlockSpec((1, tm, w), lambda i, j: (i, j, 0))
    vec = pl.BlockSpec((1, 1, d), lambda i, j: (i, 0, 0))
    const = lambda shp: pl.BlockSpec(shp, lambda i, j: (0,) * len(shp))
    return pl.pallas_call(
        _outproj_body,
        grid=(b, nt),
        in_specs=[row(d), row(D_DIFF), row(D_FOX), const(w_out.shape), vec, vec, vec,
                  const((1, d)), const(rwt.shape), const(rb.shape)],
        out_specs=[row(d),
                   pl.BlockSpec((tm, d), lambda i, j: (i * nt + j, 0)),
                   pl.BlockSpec((N_EXPERTS, tm), lambda i, j: (0, i * nt + j))],
        out_shape=[jax.ShapeDtypeStruct((b, s, d), F32),
                   jax.ShapeDtypeStruct((n, d), BF16),
                   jax.ShapeDtypeStruct((N_EXPERTS, n), F32)],
        compiler_params=_cparams(("arbitrary", "arbitrary")),
        name="out_proj",
    )(x, a_out, b_out, w_out, g1, sc2, sh2, ng, rwt, rb)


def _strict_lower(n):
    r = lax.broadcasted_iota(I32, (n, n), 0)
    c = lax.broadcasted_iota(I32, (n, n), 1)
    return (c < r).astype(F32)


def _align_up(x_f32, m):
    shift = m.bit_length() - 1
    return (((x_f32.astype(I32) + (m - 1)) >> shift) << shift).astype(F32)


def _route_body(lg_ref, pos_ref, gate_ref, cnt_ref):
    i = pl.program_id(0)

    @pl.when(i == 0)
    def _():
        cnt_ref[...] = jnp.zeros_like(cnt_ref)

    work = lg_ref[...]
    td = work.shape[1]
    eidx = lax.broadcasted_iota(I32, work.shape, 0)
    vals, hots = [], []
    for k in range(TOP_K):
        m = jnp.max(work, axis=0, keepdims=True)
        sel = jnp.min(jnp.where(work == m, eidx, N_EXPERTS), axis=0, keepdims=True)
        hot = eidx == sel
        vals.append(m)
        hots.append(hot)
        work = jnp.where(hot, -jnp.inf, work)
    ex = [jnp.exp(v - vals[0]) for v in vals]
    den = ex[0] + ex[1] + ex[2] + ex[3]
    for k in range(TOP_K):
        gate_ref[k:k + 1, :] = ex[k] / den

    chosen = hots[0] | hots[1] | hots[2] | hots[3]
    r = lax.broadcasted_iota(I32, (td, td), 0)
    c = lax.broadcasted_iota(I32, (td, td), 1)
    before = (r < c).astype(BF16)
    earlier = jnp.dot(chosen.astype(BF16), before, preferred_element_type=F32)
    cnt = jnp.sum(chosen.astype(F32), axis=1, keepdims=True)
    run = _align_up(cnt, RUN_ALIGN)
    start = jnp.dot(_strict_lower(N_EXPERTS), jnp.broadcast_to(run, (N_EXPERTS, LANES)),
                    preferred_element_type=F32, precision=HIGHEST)[:, 0:1]
    slot = start + earlier
    for k in range(TOP_K):
        pos_ref[k:k + 1, :] = jnp.sum(jnp.where(hots[k], slot, 0.0), axis=0, keepdims=True).astype(I32)
    lane = lax.broadcasted_iota(I32, cnt_ref.shape, 1)
    cnt_ref[...] = jnp.where(lane == i, run, cnt_ref[...])


def _route(logits_t):
    e, n = logits_t.shape
    td = min(ROUTE_TILE, n)
    ntp = _round_up(n // td, LANES)
    tok = pl.BlockSpec((TOP_K, td), lambda i: (0, i))
    return pl.pallas_call(
        _route_body,
        grid=(n // td,),
        in_specs=[pl.BlockSpec((e, td), lambda i: (0, i))],
        out_specs=[tok, tok, pl.BlockSpec((e, ntp), lambda i: (0, 0))],
        out_shape=[jax.ShapeDtypeStruct((TOP_K, n), I32),
                   jax.ShapeDtypeStruct((TOP_K, n), F32),
                   jax.ShapeDtypeStruct((e, ntp), F32)],
        compiler_params=_cparams(("arbitrary",)),
        name="route_topk",
    )(logits_t)


def _layout_body(run_ref, base_ref, be_ref, nu_ref, tail_ref):
    run = run_ref[...]
    ntp = run.shape[1]
    total = jnp.sum(run, axis=1, keepdims=True)
    region = _align_up(total, EXPERT_TILE)
    pstart = jnp.dot(_strict_lower(N_EXPERTS), jnp.broadcast_to(region, (N_EXPERTS, LANES)),
                     preferred_element_type=F32, precision=HIGHEST)[:, 0:1]
    ti = lax.broadcasted_iota(I32, (ntp, ntp), 0)
    tj = lax.broadcasted_iota(I32, (ntp, ntp), 1)
    within = jnp.dot(run, (ti < tj).astype(F32), preferred_element_type=F32, precision=HIGHEST)
    base_ref[...] = (pstart + within).astype(I32)
    pend = pstart + region
    blk = (lax.broadcasted_iota(I32, (N_EXPERTS, be_ref.shape[1]), 1) * EXPERT_TILE).astype(F32)
    be = jnp.sum((pend <= blk).astype(I32), axis=0, keepdims=True)
    be_ref[...] = jnp.minimum(be, N_EXPERTS - 1)
    used = jnp.sum(region, axis=0, keepdims=True).astype(I32) >> (EXPERT_TILE.bit_length() - 1)
    nu_ref[...] = jnp.broadcast_to(used, nu_ref.shape)
    lane = lax.broadcasted_iota(I32, tail_ref.shape, 1)
    tail_ref[...] = jnp.where(lane == 0, pstart + total, jnp.where(lane == 1, region - total, 0.0)).astype(I32)


def _layout(run_t, n_blocks):
    e, ntp = run_t.shape
    nbp = _round_up(n_blocks, LANES)
    full = lambda shp: pl.BlockSpec(shp, lambda: (0,) * len(shp))
    return pl.pallas_call(
        _layout_body,
        in_specs=[full((e, ntp))],
        out_specs=[full((e, ntp)), full((1, nbp)), full((1, LANES)), full((e, LANES))],
        out_shape=[jax.ShapeDtypeStruct((e, ntp), I32),
                   jax.ShapeDtypeStruct((1, nbp), I32),
                   jax.ShapeDtypeStruct((1, LANES), I32),
                   jax.ShapeDtypeStruct((e, LANES), I32)],
        name="expert_layout",
    )(run_t)


_RUN_LEVELS = tuple(1 << b for b in range(ROUTE_TILE.bit_length() - 1, RUN_ALIGN.bit_length() - 2, -1))


def _for_each_run(run_tbl, base_tbl, tile, make_copy, start):
    def body(e, off):
        cnt = run_tbl[tile * N_EXPERTS + e]
        base = base_tbl[tile * N_EXPERTS + e]
        done = jnp.int32(0)
        for lvl in _RUN_LEVELS:
            bit = cnt & lvl

            @pl.when(bit != 0)
            def _(done=done, lvl=lvl):
                cp = make_copy(pl.multiple_of(off + done, RUN_ALIGN), pl.multiple_of(base + done, RUN_ALIGN), lvl)
                if start:
                    cp.start()
                else:
                    cp.wait()
            done = done + bit
        return off + cnt

    lax.fori_loop(0, N_EXPERTS, body, jnp.int32(0))


def _slot_onehot(pos, sb):
    j = lax.broadcasted_iota(I32, (sb, pos.shape[1]), 0)
    hits = [j == pos[k:k + 1, :] for k in range(TOP_K)]
    return hits, (hits[0] | hits[1] | hits[2] | hits[3])


def _sorted_rows(td):
    return _round_up(TOP_K * td + N_EXPERTS * (RUN_ALIGN - 1), LANES)


def _dispatch_body(run_tbl, base_tbl, tail_len, tail_start, h_ref, pos_ref, gate_ref, xbuf_hbm,
                   sorted_ref, zero_ref, sem, zsem, *, d):
    i = pl.program_id(0)
    last = pl.num_programs(0) - 1
    slot = i % 2
    sb = sorted_ref.shape[1]

    def zero_copies(off, base, rows):
        del off
        return pltpu.make_async_copy(zero_ref.at[pl.ds(0, rows)], xbuf_hbm.at[pl.ds(base, rows)], zsem)

    @pl.when(i == 0)
    def _():
        zero_ref[...] = jnp.zeros_like(zero_ref)
        _for_each_run(tail_len, tail_start, 0, zero_copies, True)

    hits, any_hit = _slot_onehot(pos_ref[...], sb)
    perm = jnp.where(any_hit, 1.0, 0.0).astype(BF16)
    sorted_ref[slot, :, :d] = jnp.dot(perm, h_ref[...], preferred_element_type=F32)
    gates = gate_ref[...]
    gsel = jnp.where(hits[0], gates[0:1, :], 0.0)
    for k in range(1, TOP_K):
        gsel = gsel + jnp.where(hits[k], gates[k:k + 1, :], 0.0)
    sorted_ref[slot, :, d:] = jnp.broadcast_to(jnp.sum(gsel, axis=1, keepdims=True), (sb, LANES))

    def copies(slot):
        return lambda off, base, rows: pltpu.make_async_copy(
            sorted_ref.at[slot, pl.ds(off, rows)], xbuf_hbm.at[pl.ds(base, rows)], sem.at[slot])

    _for_each_run(run_tbl, base_tbl, i, copies(slot), True)

    @pl.when(i > 0)
    def _():
        _for_each_run(run_tbl, base_tbl, i - 1, copies(1 - slot), False)

    @pl.when(i == last)
    def _():
        _for_each_run(run_tbl, base_tbl, i, copies(slot), False)
        _for_each_run(tail_len, tail_start, 0, zero_copies, False)


def _dispatch(run_tbl, base_tbl, tail_len, tail_start, h2, pos_t, gate_t, rows):
    n, d = h2.shape
    td = min(ROUTE_TILE, n)
    sb = _sorted_rows(td)
    dx = d + LANES
    assert EXPERT_TILE <= 2 * _RUN_LEVELS[0]
    tok = pl.BlockSpec((TOP_K, td), lambda i, *_: (0, i))
    grid_spec = pltpu.PrefetchScalarGridSpec(
        num_scalar_prefetch=4,
        grid=(n // td,),
        in_specs=[pl.BlockSpec((td, d), lambda i, *_: (i, 0)), tok, tok],
        out_specs=pl.BlockSpec(memory_space=pl.ANY),
        scratch_shapes=[pltpu.VMEM((2, sb, dx), F32), pltpu.VMEM((_RUN_LEVELS[0], dx), F32),
                        pltpu.SemaphoreType.DMA((2,)), pltpu.SemaphoreType.DMA],
    )
    return pl.pallas_call(
        functools.partial(_dispatch_body, d=d),
        grid_spec=grid_spec,
        out_shape=jax.ShapeDtypeStruct((rows, dx), F32),
        compiler_params=_cparams(("arbitrary",)),
        name="dispatch_rows",
    )(run_tbl, base_tbl, tail_len, tail_start, h2, pos_t, gate_t)


def _expert_body(be_ref, nu_ref, x_ref, w1_ref, b1_ref, w2_ref, b2_ref, y_ref, w1b_ref, w2b_ref):
    i = pl.program_id(0)
    d_ff, d = w2_ref.shape[1], w2_ref.shape[2]

    @pl.when((i == 0) | (be_ref[i] != be_ref[jnp.maximum(i - 1, 0)]))
    def _():
        w1b_ref[...] = w1_ref[0].astype(BF16)
        w2b_ref[...] = w2_ref[0].astype(BF16)

    @pl.when(i < nu_ref[0])
    def _():
        gu = jnp.dot(x_ref[:, :d].astype(BF16), w1b_ref[...], preferred_element_type=F32) + b1_ref[0]
        gate = jnp.minimum(gu[:, :d_ff], SWIGLU_LIMIT)
        up = jnp.clip(gu[:, d_ff:], -SWIGLU_LIMIT, SWIGLU_LIMIT)
        glu = gate * jax.nn.sigmoid(gate * SWIGLU_ALPHA)
        act = ((up + 1.0) * glu).astype(BF16)
        y = jnp.dot(act, w2b_ref[...], preferred_element_type=F32) + b2_ref[0]
        y_ref[...] = x_ref[:, d:d + 1] * y

    @pl.when(i >= nu_ref[0])
    def _():
        y_ref[...] = jnp.zeros_like(y_ref)


def _experts(layer, block_expert, n_used, xbuf, w1, b1, w2, b2):
    rows, dx = xbuf.shape
    de, d, f2 = w1.shape
    f = w2.shape[1]
    blk = lambda i, be, nu: (jnp.maximum(jnp.minimum(i, nu[0] - 1), 0), 0)
    wsel = lambda i, be, nu: (layer * N_EXPERTS + be[i], 0, 0)
    grid_spec = pltpu.PrefetchScalarGridSpec(
        num_scalar_prefetch=2,
        grid=(rows // EXPERT_TILE,),
        in_specs=[pl.BlockSpec((EXPERT_TILE, dx), blk),
                  pl.BlockSpec((1, d, f2), wsel),
                  pl.BlockSpec((1, 1, f2), wsel),
                  pl.BlockSpec((1, f, d), wsel),
                  pl.BlockSpec((1, 1, d), wsel)],
        out_specs=pl.BlockSpec((EXPERT_TILE, d), lambda i, be, nu: (i, 0)),
        scratch_shapes=[pltpu.VMEM((d, f2), BF16), pltpu.VMEM((f, d), BF16)],
    )
    return pl.pallas_call(
        _expert_body,
        grid_spec=grid_spec,
        out_shape=jax.ShapeDtypeStruct((rows, d), F32),
        compiler_params=_cparams(("arbitrary",)),
        name="expert_swiglu",
    )(block_expert, n_used, xbuf, w1, b1.reshape(de, 1, f2), w2, b2.reshape(de, 1, d))


def _combine_body(run_tbl, base_tbl, x1_ref, g2_ref, pos_ref, fg_ref, ybuf_hbm, o_ref, ys_ref, sem, *, final):
    i = pl.program_id(0)
    last = pl.num_programs(0) - 1
    slot = i % 2
    sb = ys_ref.shape[1]

    def copies(slot):
        return lambda off, base, rows: pltpu.make_async_copy(
            ybuf_hbm.at[pl.ds(base, rows)], ys_ref.at[slot, pl.ds(off, rows)], sem.at[slot])

    @pl.when(i == 0)
    def _():
        ys_ref[...] = jnp.zeros_like(ys_ref)
        _for_each_run(run_tbl, base_tbl, 0, copies(0), True)

    @pl.when(i < last)
    def _():
        _for_each_run(run_tbl, base_tbl, i + 1, copies(1 - slot), True)

    _for_each_run(run_tbl, base_tbl, i, copies(slot), False)

    _, any_hit = _slot_onehot(pos_ref[...], sb)
    perm = jnp.where(any_hit, 1.0, 0.0).astype(BF16)
    y = ys_ref[slot]
    y_hi = y.astype(BF16)
    y_lo = (y - y_hi.astype(F32)).astype(BF16)
    moe = (lax.dot_general(perm, y_hi, _TN, preferred_element_type=F32)
           + lax.dot_general(perm, y_lo, _TN, preferred_element_type=F32))
    out = x1_ref[...] + g2_ref[0] * moe
    if final:
        out = _rms(out) * fg_ref[...]
    o_ref[...] = out


def _combine(run_tbl, base_tbl, x1, g2, pos_t, final_g, ybuf, seq, final):
    n, d = x1.shape
    td = min(ROUTE_TILE, n)
    sb = _sorted_rows(td)
    grid_spec = pltpu.PrefetchScalarGridSpec(
        num_scalar_prefetch=2,
        grid=(n // td,),
        in_specs=[pl.BlockSpec((td, d), lambda i, *_: (i, 0)),
                  pl.BlockSpec((1, 1, d), lambda i, *_: ((i * td) // seq, 0, 0)),
                  pl.BlockSpec((TOP_K, td), lambda i, *_: (0, i)),
                  pl.BlockSpec((1, d), lambda i, *_: (0, 0)),
                  pl.BlockSpec(memory_space=pl.ANY)],
        out_specs=pl.BlockSpec((td, d), lambda i, *_: (i, 0)),
        scratch_shapes=[pltpu.VMEM((2, sb, d), F32), pltpu.SemaphoreType.DMA((2,))],
    )
    return pl.pallas_call(
        functools.partial(_combine_body, final=final),
        grid_spec=grid_spec,
        out_shape=jax.ShapeDtypeStruct((n, d), F32),
        compiler_params=_cparams(("arbitrary",)),
        name="combine_rows",
    )(run_tbl, base_tbl, x1, g2, pos_t, final_g, ybuf)


def _moe(layer, x1, h2, logits_t, g2, w1, b1, w2, b2, final_g, seq, final):
    n, d = h2.shape
    td = min(ROUTE_TILE, n)
    n_tiles = n // td
    assert seq % td == 0
    rows = (_round_up(n * TOP_K + n_tiles * N_EXPERTS * (RUN_ALIGN - 1), EXPERT_TILE)
            + N_EXPERTS * EXPERT_TILE)
    n_blocks = rows // EXPERT_TILE
    pos_t, gate_t, run_t = _route(logits_t)
    base_t, be, nu, tail = _layout(run_t, n_blocks)
    run_tbl = run_t[:, :n_tiles].T.astype(I32).reshape(-1)
    base_tbl = base_t[:, :n_tiles].T.reshape(-1)
    xbuf = _dispatch(run_tbl, base_tbl, tail[:, 1], tail[:, 0], h2, pos_t, gate_t, rows)
    ybuf = _experts(layer, be[0, :n_blocks], nu[0, :1], xbuf, w1, b1, w2, b2)
    return _combine(run_tbl, base_tbl, x1, g2, pos_t, final_g, ybuf, seq, final)


def kernel(x, c, positions, ada_w, ada_b, norm_mix, norm_ffn, w_in, b_forget, lambda_q1, lambda_k1,
           lambda_q2, lambda_k2, diff_subln, w_out, router_w, router_b, exp_w1, exp_b1, exp_w2, exp_b2,
           final_norm):
    b, s, d = x.shape
    depth = ada_w.shape[0]
    mod = _ada_mod(c, ada_w, ada_b)
    cosf, sinf = _rope_tables(positions)
    n_in = w_in.shape[2]
    pad_cols = 3072 + LANES - n_in
    flat = lambda a: a.reshape((a.shape[0] * a.shape[1],) + a.shape[2:])
    w1_all, b1_all, w2_all, b2_all = flat(exp_w1), flat(exp_b1), flat(exp_w2), flat(exp_b2)
    for l in range(depth):
        lam_init = 0.8 - 0.6 * math.exp(-0.3 * l)
        sh1, sc1, g1, sh2, sc2, g2 = [mod[l, :, i * d:(i + 1) * d].reshape(b, 1, d) for i in range(6)]
        w_pad = jnp.pad(w_in[l], ((0, 0), (0, pad_cols))).astype(BF16)
        bf_pad = jnp.pad(b_forget[l], (0, LANES - FOX_HEADS)).reshape(1, LANES)
        dq, dk, dv, fq, fk, fv, flog = _inproj(x, sc1, sh1, norm_mix[l].reshape(1, d), w_pad, bf_pad,
                                               cosf, sinf)
        cum, cumt = _forget_cumsum(flog)
        a_out = _diff_attention(dq, dk, dv, lambda_q1[l].reshape(1, -1), lambda_k1[l].reshape(1, -1),
                                lambda_q2[l].reshape(1, -1), lambda_k2[l].reshape(1, -1),
                                diff_subln[l].reshape(1, -1), lam_init)
        b_out = _fox_attention(fq, fk, fv, cum, cumt)
        x1, h2, logits_t = _outproj(x, a_out, b_out, w_out[l].astype(BF16), g1, sc2, sh2,
                                    norm_ffn[l].reshape(1, d), router_w[l].T,
                                    router_b[l].reshape(-1, 1))
        x = _moe(l, x1.reshape(b * s, d), h2, logits_t, g2, w1_all, b1_all, w2_all, b2_all,
                 final_norm.reshape(1, d), s, l == depth - 1).reshape(b, s, d)
    return x
```

```python
import functools
import math

import jax
import jax.numpy as jnp
from jax import lax
from jax.experimental import pallas as pl
from jax.experimental.pallas import tpu as pltpu

F32 = jnp.float32
BF16 = jnp.bfloat16
I32 = jnp.int32
HIGHEST = lax.Precision.HIGHEST

HEAD_DIM = 64
LANES = 128
SUBLANES = 8
CHUNK = 64
DIFF_HEADS = 4
FOX_HEADS = 8
D_DIFF = DIFF_HEADS * 2 * HEAD_DIM
D_FOX = FOX_HEADS * HEAD_DIM
ROPE_THETA = 500000.0
ROPE_DIM = HEAD_DIM // 4
ROPE_HALF = ROPE_DIM // 2
N_EXPERTS = 32
TOP_K = 4
SWIGLU_LIMIT = 7.0
SWIGLU_ALPHA = 1.702
NORM_EPS = 1e-6
NEG_BIG = -1e30
LOG2E = math.log2(math.e)
ATTN_TILE = 512
EXPERT_TILE = 512
ROUTE_TILE = 256
RUN_ALIGN = SUBLANES
VMEM_LIMIT = 56 * 1024 * 1024

_NT = (((1,), (1,)), ((), ()))
_TN = (((0,), (0,)), ((), ()))


def _cparams(sem, vmem=None):
    return pltpu.CompilerParams(dimension_semantics=sem, vmem_limit_bytes=vmem or VMEM_LIMIT)


def _rms(x):
    return x * lax.rsqrt(jnp.mean(x * x, axis=-1, keepdims=True) + NORM_EPS)


def _round_up(x, m):
    return (x + m - 1) // m * m


def _ada_body(c_ref, w_ref, b_ref, o_ref):
    c = c_ref[...]
    ca = c * jax.nn.sigmoid(c)
    o_ref[0] = jnp.dot(ca, w_ref[0], preferred_element_type=F32, precision=HIGHEST) + b_ref[0]


def _ada_mod(c, ada_w, ada_b):
    depth, d, n6 = ada_w.shape
    b = c.shape[0]
    tn = 1536
    return pl.pallas_call(
        _ada_body,
        grid=(depth, n6 // tn),
        in_specs=[pl.BlockSpec((b, d), lambda l, j: (0, 0)),
                  pl.BlockSpec((1, d, tn), lambda l, j: (l, 0, j)),
                  pl.BlockSpec((1, 1, tn), lambda l, j: (l, 0, j))],
        out_specs=pl.BlockSpec((1, b, tn), lambda l, j: (l, 0, j)),
        out_shape=jax.ShapeDtypeStruct((depth, b, n6), F32),
        compiler_params=_cparams(("arbitrary", "arbitrary")),
        name="ada_mod",
    )(c, ada_w, ada_b.reshape(depth, 1, n6))


def _rope_body(pos_ref, invf_ref, cos_ref, sin_ref):
    ang = pos_ref[0].astype(F32) * invf_ref[...]
    j = lax.broadcasted_iota(I32, ang.shape, 1) & (HEAD_DIM - 1)
    c = jnp.cos(ang)
    s = jnp.sin(ang)
    cos_ref[0] = jnp.where(j < ROPE_DIM, c, 1.0)
    sin_ref[0] = jnp.where(j < ROPE_HALF, -s, jnp.where(j < ROPE_DIM, s, 0.0))


def _rope_tables(positions):
    b, s = positions.shape
    ts = min(512, s)
    inv_freq = ROPE_THETA ** (-jnp.arange(0, ROPE_DIM, 2, dtype=F32) / ROPE_DIM)
    invf = inv_freq[jnp.arange(LANES) % ROPE_HALF].reshape(1, LANES)
    out = jax.ShapeDtypeStruct((b, s, LANES), F32)
    return pl.pallas_call(
        _rope_body,
        grid=(b, s // ts),
        in_specs=[pl.BlockSpec((1, ts, 1), lambda i, j: (i, j, 0)),
                  pl.BlockSpec((1, LANES), lambda i, j: (0, 0))],
        out_specs=[pl.BlockSpec((1, ts, LANES), lambda i, j: (i, j, 0))] * 2,
        out_shape=[out, out],
        compiler_params=_cparams(("arbitrary", "arbitrary")),
        name="rope_tables",
    )(positions.reshape(b, s, 1), invf)


def _inproj_body(x_ref, sc_ref, sh_ref, g_ref, w_ref, bf_ref, cos_ref, sin_ref,
                 dq_ref, dk_ref, dv_ref, fq_ref, fk_ref, fv_ref, fl_ref):
    h = _rms(x_ref[0]) * g_ref[...] * (1.0 + sc_ref[0]) + sh_ref[0]
    hb = h.astype(BF16)
    cosf = cos_ref[0]
    sinf = sin_ref[0]
    lane = lax.broadcasted_iota(I32, cosf.shape, 1)
    first = (lane & (HEAD_DIM - 1)) < ROPE_HALF

    def rope(p):
        nxt = pltpu.roll(p, LANES - ROPE_HALF, 1)
        prv = pltpu.roll(p, ROPE_HALF, 1)
        return p * cosf + jnp.where(first, nxt, prv) * sinf

    def proj(c0):
        return jnp.dot(hb, w_ref[:, c0:c0 + 512], preferred_element_type=F32)

    scale = HEAD_DIM ** -0.5 * LOG2E
    p = proj(0)
    for c in range(4):
        sl = slice(c * LANES, (c + 1) * LANES)
        dq_ref[0, :, sl] = (rope(p[:, sl]) * scale).astype(BF16)
    p = proj(512)
    for c in range(4):
        sl = slice(c * LANES, (c + 1) * LANES)
        dk_ref[0, :, sl] = rope(p[:, sl]).astype(BF16)
    dv_ref[0] = proj(1024).astype(BF16)
    fq_ref[0] = (proj(1536) * scale).astype(BF16)
    fk_ref[0] = proj(2048).astype(BF16)
    fv_ref[0] = proj(2560).astype(BF16)
    z = jnp.dot(hb, w_ref[:, 3072:3072 + LANES], preferred_element_type=F32) + bf_ref[...]
    fl_ref[0] = -(jnp.maximum(-z, 0.0) + jnp.log1p(jnp.exp(-jnp.abs(z))))


def _inproj(x, sc, sh, g, w_pad, bf_pad, cosf, sinf):
    b, s, d = x.shape
    tm = min(512, s)
    act = lambda w: jax.ShapeDtypeStruct((b, s, w), BF16)
    row = lambda w: pl.BlockSpec((1, tm, w), lambda i, j: (i, j, 0))
    vec = pl.BlockSpec((1, 1, d), lambda i, j: (i, 0, 0))
    return pl.pallas_call(
        _inproj_body,
        grid=(b, s // tm),
        in_specs=[row(d), vec, vec,
                  pl.BlockSpec((1, d), lambda i, j: (0, 0)),
                  pl.BlockSpec(w_pad.shape, lambda i, j: (0, 0)),
                  pl.BlockSpec((1, LANES), lambda i, j: (0, 0)),
                  row(LANES), row(LANES)],
        out_specs=[row(512)] * 6 + [row(LANES)],
        out_shape=[act(512)] * 6 + [jax.ShapeDtypeStruct((b, s, LANES), F32)],
        compiler_params=_cparams(("arbitrary", "arbitrary")),
        name="in_proj",
    )(x, sc, sh, g, w_pad, bf_pad, cosf, sinf)


def _cum_body(fl_ref, cum_ref):
    x = fl_ref[0]
    s = x.shape[0]
    row = lax.broadcasted_iota(I32, x.shape, 0)
    d = 1
    while d < s:
        x = x + jnp.where(row >= d, pltpu.roll(x, d, 0), 0.0)
        d *= 2
    cum_ref[0] = x * LOG2E


def _forget_cumsum(flog):
    b, s, _ = flog.shape
    spec = pl.BlockSpec((1, s, LANES), lambda i: (i, 0, 0))
    return pl.pallas_call(
        _cum_body,
        grid=(b,),
        in_specs=[spec],
        out_specs=spec,
        out_shape=jax.ShapeDtypeStruct((b, s, LANES), F32),
        compiler_params=_cparams(("arbitrary",)),
        name="forget_cumsum",
    )(flog)


def _causal_sweep(t, tile, state):
    def two(i, st):
        return tile(2 * i + 1, tile(2 * i, st, False), False)

    state = lax.fori_loop(0, t // 2, two, state)
    state = lax.fori_loop(t - (t & 1), t, lambda j, st: tile(j, st, False), state)
    return tile(t, state, True)


def _diff_body(q_ref, k_ref, v_ref, lq1_ref, lk1_ref, lq2_ref, lk2_ref, g_ref, o_ref, *, tq, lam_init):
    t = pl.program_id(2)
    q = q_ref[0]
    lane = lax.broadcasted_iota(I32, q.shape, 1)
    zero = jnp.zeros_like(q)
    qs = (jnp.where(lane < HEAD_DIM, q, zero), jnp.where(lane >= HEAD_DIM, q, zero))

    def tile(j, state, masked):
        off = pl.multiple_of(j * tq, tq)
        kt = k_ref[0, pl.ds(off, tq), :]
        vt = v_ref[0, pl.ds(off, tq), :]
        out = []
        for mp in range(2):
            s = lax.dot_general(qs[mp], kt, _NT, preferred_element_type=F32)
            if masked:
                r = lax.broadcasted_iota(I32, s.shape, 0)
                c = lax.broadcasted_iota(I32, s.shape, 1)
                shift = CHUNK.bit_length() - 1
                s = jnp.where((c >> shift) <= (r >> shift), s, NEG_BIG)
            m, l, acc = state[mp]
            m_new = jnp.maximum(m, jnp.max(s, axis=1, keepdims=True))
            alpha = jnp.exp2(m - m_new)
            p = jnp.exp2(s - m_new)
            l = alpha * l + jnp.sum(p, axis=1, keepdims=True)
            acc = alpha * acc + jnp.dot(p.astype(BF16), vt, preferred_element_type=F32)
            out.append((m_new, l, acc))
        return tuple(out)

    init = (jnp.full((tq, 1), NEG_BIG, F32), jnp.zeros((tq, 1), F32), jnp.zeros((tq, LANES), F32))
    (_, l0, a0), (_, l1, a1) = _causal_sweep(t, tile, (init, init))

    lam = (jnp.exp(jnp.sum(lq1_ref[...] * lk1_ref[...], axis=1, keepdims=True))
           - jnp.exp(jnp.sum(lq2_ref[...] * lk2_ref[...], axis=1, keepdims=True)) + lam_init)
    out = a0 / l0 - lam * (a1 / l1)
    o_ref[0] = (_rms(out) * g_ref[...] * (1.0 - lam_init)).astype(o_ref.dtype)


def _diff_attention(dq, dk, dv, lq1, lk1, lq2, lk2, subln, lam_init):
    b, s, _ = dq.shape
    tq = min(ATTN_TILE, s)
    qspec = pl.BlockSpec((1, tq, LANES), lambda i, h, t: (i, t, h))
    kvspec = pl.BlockSpec((1, s, LANES), lambda i, h, t: (i, 0, h))
    lspec = pl.BlockSpec((1, HEAD_DIM), lambda i, h, t: (0, 0))
    return pl.pallas_call(
        functools.partial(_diff_body, tq=tq, lam_init=lam_init),
        grid=(b, DIFF_HEADS, s // tq),
        in_specs=[qspec, kvspec, kvspec, lspec, lspec, lspec, lspec,
                  pl.BlockSpec((1, LANES), lambda i, h, t: (0, 0))],
        out_specs=qspec,
        out_shape=jax.ShapeDtypeStruct((b, s, D_DIFF), BF16),
        compiler_params=_cparams(("arbitrary", "arbitrary", "arbitrary")),
        name="diff_attention",
    )(dq, dk, dv, lq1, lk1, lq2, lk2, subln)


def _split3(c):
    hi = c.astype(BF16).astype(F32)
    r = c - hi
    mid = r.astype(BF16).astype(F32)
    return hi, mid, r - mid


def _bias_lanes(c, lane, first):
    hi, mid, lo = _split3(c)
    ones = jnp.where((lane >= HEAD_DIM) & (lane < HEAD_DIM + 6), 1.0, 0.0)
    return jnp.where(lane == first, hi, jnp.where(lane == first + 1, mid, jnp.where(lane == first + 2, lo, ones)))


def _head_column(c_all, h, lane):
    return jnp.broadcast_to(jnp.sum(jnp.where(lane == h, c_all, 0.0), axis=1, keepdims=True), c_all.shape)


def _fox_body(q_ref, k_ref, v_ref, cq_ref, ck_ref, o_ref, ka_ref, va_ref, *, tq):
    pair = pl.program_id(1)
    t = pl.program_id(2)
    heads = (2 * pair, 2 * pair + 1)
    lane = lax.broadcasted_iota(I32, (tq, LANES), 1)

    def per_head(x_f32, hh):
        return x_f32 if hh == 0 else pltpu.roll(x_f32, HEAD_DIM, 1)

    @pl.when(t == 0)
    def _():
        def chunk(ci, carry):
            rows = pl.ds(pl.multiple_of(ci * tq, tq), tq)
            kf = k_ref[0, rows, :].astype(F32)
            vf = v_ref[0, rows, :].astype(F32)
            c_all = ck_ref[0, rows, :]
            for hh in range(2):
                bias = _bias_lanes(-_head_column(c_all, heads[hh], lane), lane, HEAD_DIM + 3)
                ka_ref[hh, rows, :] = jnp.where(lane < HEAD_DIM, per_head(kf, hh), bias).astype(BF16)
                va_ref[hh, rows, :] = jnp.where(lane < HEAD_DIM, per_head(vf, hh), 1.0).astype(BF16)
            return carry

        lax.fori_loop(0, k_ref.shape[1] // tq, chunk, 0)

    qf = q_ref[0].astype(F32)
    cq_all = cq_ref[0]
    qs = [jnp.where(lane < HEAD_DIM, per_head(qf, hh),
                    _bias_lanes(_head_column(cq_all, heads[hh], lane), lane, HEAD_DIM)).astype(BF16)
          for hh in range(2)]

    def tile(j, state, masked):
        off = pl.multiple_of(j * tq, tq)
        out = []
        for hh in range(2):
            s = lax.dot_general(qs[hh], ka_ref[hh, pl.ds(off, tq), :], _NT, preferred_element_type=F32)
            if masked:
                r = lax.broadcasted_iota(I32, s.shape, 0)
                c = lax.broadcasted_iota(I32, s.shape, 1)
                s = jnp.where(c <= r, s, NEG_BIG)
            m, acc = state[hh]
            m_new = jnp.maximum(m, jnp.max(s, axis=1, keepdims=True))
            p = jnp.exp2(s - m_new).astype(BF16)
            acc = jnp.exp2(m - m_new) * acc + jnp.dot(p, va_ref[hh, pl.ds(off, tq), :],
                                                      preferred_element_type=F32)
            out.append((m_new, acc))
        return tuple(out)

    init = (jnp.full((tq, 1), NEG_BIG, F32), jnp.zeros((tq, LANES), F32))
    (_, a0), (_, a1) = _causal_sweep(t, tile, (init, init))
    o0 = a0 / pltpu.roll(a0, HEAD_DIM, 1)
    o1 = a1 / pltpu.roll(a1, HEAD_DIM, 1)
    o_ref[0] = jnp.where(lane < HEAD_DIM, o0, pltpu.roll(o1, HEAD_DIM, 1)).astype(o_ref.dtype)


def _fox_attention(fq, fk, fv, cum):
    b, s, _ = fq.shape
    tq = min(ATTN_TILE, s)
    qspec = pl.BlockSpec((1, tq, LANES), lambda i, p, t: (i, t, p))
    kvspec = pl.BlockSpec((1, s, LANES), lambda i, p, t: (i, 0, p))
    return pl.pallas_call(
        functools.partial(_fox_body, tq=tq),
        grid=(b, FOX_HEADS // 2, s // tq),
        in_specs=[qspec, kvspec, kvspec,
                  pl.BlockSpec((1, tq, LANES), lambda i, p, t: (i, t, 0)),
                  pl.BlockSpec((1, s, LANES), lambda i, p, t: (i, 0, 0))],
        out_specs=qspec,
        out_shape=jax.ShapeDtypeStruct((b, s, D_FOX), BF16),
        scratch_shapes=[pltpu.VMEM((2, s, LANES), BF16), pltpu.VMEM((2, s, LANES), BF16)],
        compiler_params=_cparams(("arbitrary", "arbitrary", "arbitrary")),
        name="fox_attention",
    )(fq, fk, fv, cum, cum)


def _outproj_body(x_ref, a_ref, b_ref, w_ref, g1_ref, sc_ref, sh_ref, ng_ref, rw_ref, rb_ref,
                  x1_ref, h2_ref, lg_ref):
    mix = (jnp.dot(a_ref[0], w_ref[:D_DIFF, :], preferred_element_type=F32)
           + jnp.dot(b_ref[0], w_ref[D_DIFF:, :], preferred_element_type=F32))
    x1 = x_ref[0] + g1_ref[0] * mix
    x1_ref[0] = x1
    h2 = _rms(x1) * ng_ref[...] * (1.0 + sc_ref[0]) + sh_ref[0]
    h2_ref[...] = h2.astype(h2_ref.dtype)
    lg_ref[...] = lax.dot_general(rw_ref[...], h2, _NT, preferred_element_type=F32,
                                  precision=HIGHEST) + rb_ref[...]


def _outproj(x, a_out, b_out, w_out, g1, sc2, sh2, ng, rwt, rb):
    b, s, d = x.shape
    n = b * s
    tm = min(512, s)
    nt = s // tm
    row = lambda w: pl.BlockSpec((1, tm, w), lambda i, j: (i, j, 0))
    vec = pl.BlockSpec((1, 1, d), lambda i, j: (i, 0, 0))
    const = lambda shp: pl.BlockSpec(shp, lambda i, j: (0,) * len(shp))
    return pl.pallas_call(
        _outproj_body,
        grid=(b, nt),
        in_specs=[row(d), row(D_DIFF), row(D_FOX), const(w_out.shape), vec, vec, vec,
                  const((1, d)), const(rwt.shape), const(rb.shape)],
        out_specs=[row(d),
                   pl.BlockSpec((tm, d), lambda i, j: (i * nt + j, 0)),
                   pl.BlockSpec((N_EXPERTS, tm), lambda i, j: (0, i * nt + j))],
        out_shape=[jax.ShapeDtypeStruct((b, s, d), F32),
                   jax.ShapeDtypeStruct((n, d), BF16),
                   jax.ShapeDtypeStruct((N_EXPERTS, n), F32)],
        compiler_params=_cparams(("arbitrary", "arbitrary")),
        name="out_proj",
    )(x, a_out, b_out, w_out, g1, sc2, sh2, ng, rwt, rb)


def _strict_lower(n):
    r = lax.broadcasted_iota(I32, (n, n), 0)
    c = lax.broadcasted_iota(I32, (n, n), 1)
    return (c < r).astype(F32)


def _align_up(x_f32, m):
    shift = m.bit_length() - 1
    return (((x_f32.astype(I32) + (m - 1)) >> shift) << shift).astype(F32)


def _route_body(lg_ref, pos_ref, gate_ref, cnt_ref):
    i = pl.program_id(0)

    @pl.when(i == 0)
    def _():
        cnt_ref[...] = jnp.zeros_like(cnt_ref)

    work = lg_ref[...]
    td = work.shape[1]
    eidx = lax.broadcasted_iota(I32, work.shape, 0)
    vals, hots = [], []
    for k in range(TOP_K):
        m = jnp.max(work, axis=0, keepdims=True)
        sel = jnp.min(jnp.where(work == m, eidx, N_EXPERTS), axis=0, keepdims=True)
        hot = eidx == sel
        vals.append(m)
        hots.append(hot)
        work = jnp.where(hot, -jnp.inf, work)
    ex = [jnp.exp(v - vals[0]) for v in vals]
    den = ex[0] + ex[1] + ex[2] + ex[3]
    for k in range(TOP_K):
        gate_ref[k:k + 1, :] = ex[k] / den

    chosen = hots[0] | hots[1] | hots[2] | hots[3]
    r = lax.broadcasted_iota(I32, (td, td), 0)
    c = lax.broadcasted_iota(I32, (td, td), 1)
    before = (r < c).astype(BF16)
    earlier = jnp.dot(chosen.astype(BF16), before, preferred_element_type=F32)
    cnt = jnp.sum(chosen.astype(F32), axis=1, keepdims=True)
    run = _align_up(cnt, RUN_ALIGN)
    start = jnp.dot(_strict_lower(N_EXPERTS), jnp.broadcast_to(run, (N_EXPERTS, LANES)),
                    preferred_element_type=F32, precision=HIGHEST)[:, 0:1]
    slot = start + earlier
    for k in range(TOP_K):
        pos_ref[k:k + 1, :] = jnp.sum(jnp.where(hots[k], slot, 0.0), axis=0, keepdims=True).astype(I32)
    lane = lax.broadcasted_iota(I32, cnt_ref.shape, 1)
    cnt_ref[...] = jnp.where(lane == i, run, cnt_ref[...])


def _route(logits_t):
    e, n = logits_t.shape
    td = min(ROUTE_TILE, n)
    ntp = _round_up(n // td, LANES)
    tok = pl.BlockSpec((TOP_K, td), lambda i: (0, i))
    return pl.pallas_call(
        _route_body,
        grid=(n // td,),
        in_specs=[pl.BlockSpec((e, td), lambda i: (0, i))],
        out_specs=[tok, tok, pl.BlockSpec((e, ntp), lambda i: (0, 0))],
        out_shape=[jax.ShapeDtypeStruct((TOP_K, n), I32),
                   jax.ShapeDtypeStruct((TOP_K, n), F32),
                   jax.ShapeDtypeStruct((e, ntp), F32)],
        compiler_params=_cparams(("arbitrary",)),
        name="route_topk",
    )(logits_t)


def _layout_body(run_ref, base_ref, be_ref, nu_ref, tail_ref):
    run = run_ref[...]
    ntp = run.shape[1]
    total = jnp.sum(run, axis=1, keepdims=True)
    region = _align_up(total, EXPERT_TILE)
    pstart = jnp.dot(_strict_lower(N_EXPERTS), jnp.broadcast_to(region, (N_EXPERTS, LANES)),
                     preferred_element_type=F32, precision=HIGHEST)[:, 0:1]
    ti = lax.broadcasted_iota(I32, (ntp, ntp), 0)
    tj = lax.broadcasted_iota(I32, (ntp, ntp), 1)
    within = jnp.dot(run, (ti < tj).astype(F32), preferred_element_type=F32, precision=HIGHEST)
    base_ref[...] = (pstart + within).astype(I32)
    pend = pstart + region
    blk = (lax.broadcasted_iota(I32, (N_EXPERTS, be_ref.shape[1]), 1) * EXPERT_TILE).astype(F32)
    be = jnp.sum((pend <= blk).astype(I32), axis=0, keepdims=True)
    be_ref[...] = jnp.minimum(be, N_EXPERTS - 1)
    used = jnp.sum(region, axis=0, keepdims=True).astype(I32) >> (EXPERT_TILE.bit_length() - 1)
    nu_ref[...] = jnp.broadcast_to(used, nu_ref.shape)
    lane = lax.broadcasted_iota(I32, tail_ref.shape, 1)
    tail_ref[...] = jnp.where(lane == 0, pstart + total, jnp.where(lane == 1, region - total, 0.0)).astype(I32)


def _layout(run_t, n_blocks):
    e, ntp = run_t.shape
    nbp = _round_up(n_blocks, LANES)
    full = lambda shp: pl.BlockSpec(shp, lambda: (0,) * len(shp))
    return pl.pallas_call(
        _layout_body,
        in_specs=[full((e, ntp))],
        out_specs=[full((e, ntp)), full((1, nbp)), full((1, LANES)), full((e, LANES))],
        out_shape=[jax.ShapeDtypeStruct((e, ntp), I32),
                   jax.ShapeDtypeStruct((1, nbp), I32),
                   jax.ShapeDtypeStruct((1, LANES), I32),
                   jax.ShapeDtypeStruct((e, LANES), I32)],
        name="expert_layout",
    )(run_t)


_RUN_LEVELS = tuple(1 << b for b in range(ROUTE_TILE.bit_length() - 1, RUN_ALIGN.bit_length() - 2, -1))


def _for_each_run(run_tbl, base_tbl, tile, make_copy, start):
    def body(e, off):
        cnt = run_tbl[tile * N_EXPERTS + e]
        base = base_tbl[tile * N_EXPERTS + e]
        done = jnp.int32(0)
        for lvl in _RUN_LEVELS:
            bit = cnt & lvl

            @pl.when(bit != 0)
            def _(done=done, lvl=lvl):
                cp = make_copy(pl.multiple_of(off + done, RUN_ALIGN), pl.multiple_of(base + done, RUN_ALIGN), lvl)
                if start:
                    cp.start()
                else:
                    cp.wait()
            done = done + bit
        return off + cnt

    lax.fori_loop(0, N_EXPERTS, body, jnp.int32(0))


def _slot_onehot(pos, sb):
    j = lax.broadcasted_iota(I32, (sb, pos.shape[1]), 0)
    hits = [j == pos[k:k + 1, :] for k in range(TOP_K)]
    return hits, (hits[0] | hits[1] | hits[2] | hits[3])


def _sorted_rows(td):
    return _round_up(TOP_K * td + N_EXPERTS * (RUN_ALIGN - 1), LANES)


def _dispatch_body(run_tbl, base_tbl, tail_len, tail_start, h_ref, pos_ref, gate_ref, xbuf_hbm,
                   sorted_ref, zero_ref, sem, zsem, *, d):
    i = pl.program_id(0)
    last = pl.num_programs(0) - 1
    slot = i % 2
    sb = sorted_ref.shape[1]

    def zero_copies(off, base, rows):
        del off
        return pltpu.make_async_copy(zero_ref.at[pl.ds(0, rows)], xbuf_hbm.at[pl.ds(base, rows)], zsem)

    @pl.when(i == 0)
    def _():
        zero_ref[...] = jnp.zeros_like(zero_ref)
        _for_each_run(tail_len, tail_start, 0, zero_copies, True)

    hits, any_hit = _slot_onehot(pos_ref[...], sb)
    perm = jnp.where(any_hit, 1.0, 0.0).astype(BF16)
    sorted_ref[slot, :, :d] = jnp.dot(perm, h_ref[...], preferred_element_type=F32)
    gates = gate_ref[...]
    gsel = jnp.where(hits[0], gates[0:1, :], 0.0)
    for k in range(1, TOP_K):
        gsel = gsel + jnp.where(hits[k], gates[k:k + 1, :], 0.0)
    sorted_ref[slot, :, d:] = jnp.broadcast_to(jnp.sum(gsel, axis=1, keepdims=True), (sb, LANES))

    def copies(slot):
        return lambda off, base, rows: pltpu.make_async_copy(
            sorted_ref.at[slot, pl.ds(off, rows)], xbuf_hbm.at[pl.ds(base, rows)], sem.at[slot])

    _for_each_run(run_tbl, base_tbl, i, copies(slot), True)

    @pl.when(i > 0)
    def _():
        _for_each_run(run_tbl, base_tbl, i - 1, copies(1 - slot), False)

    @pl.when(i == last)
    def _():
        _for_each_run(run_tbl, base_tbl, i, copies(slot), False)
        _for_each_run(tail_len, tail_start, 0, zero_copies, False)


def _dispatch(run_tbl, base_tbl, tail_len, tail_start, h2, pos_t, gate_t, rows):
    n, d = h2.shape
    td = min(ROUTE_TILE, n)
    sb = _sorted_rows(td)
    dx = d + LANES
    assert EXPERT_TILE <= 2 * _RUN_LEVELS[0]
    tok = pl.BlockSpec((TOP_K, td), lambda i, *_: (0, i))
    grid_spec = pltpu.PrefetchScalarGridSpec(
        num_scalar_prefetch=4,
        grid=(n // td,),
        in_specs=[pl.BlockSpec((td, d), lambda i, *_: (i, 0)), tok, tok],
        out_specs=pl.BlockSpec(memory_space=pl.ANY),
        scratch_shapes=[pltpu.VMEM((2, sb, dx), F32), pltpu.VMEM((_RUN_LEVELS[0], dx), F32),
                        pltpu.SemaphoreType.DMA((2,)), pltpu.SemaphoreType.DMA],
    )
    return pl.pallas_call(
        functools.partial(_dispatch_body, d=d),
        grid_spec=grid_spec,
        out_shape=jax.ShapeDtypeStruct((rows, dx), F32),
        compiler_params=_cparams(("arbitrary",)),
        name="dispatch_rows",
    )(run_tbl, base_tbl, tail_len, tail_start, h2, pos_t, gate_t)


def _expert_body(be_ref, nu_ref, x_ref, w1_ref, b1_ref, w2_ref, b2_ref, y_ref, w1b_ref, w2b_ref):
    i = pl.program_id(0)
    d_ff, d = w2_ref.shape[1], w2_ref.shape[2]

    @pl.when((i == 0) | (be_ref[i] != be_ref[jnp.maximum(i - 1, 0)]))
    def _():
        w1b_ref[...] = w1_ref[0].astype(BF16)
        w2b_ref[...] = w2_ref[0].astype(BF16)

    @pl.when(i < nu_ref[0])
    def _():
        gu = jnp.dot(x_ref[:, :d].astype(BF16), w1b_ref[...], preferred_element_type=F32) + b1_ref[0]
        gate = jnp.minimum(gu[:, :d_ff], SWIGLU_LIMIT)
        up = jnp.clip(gu[:, d_ff:], -SWIGLU_LIMIT, SWIGLU_LIMIT)
        glu = gate * jax.nn.sigmoid(gate * SWIGLU_ALPHA)
        act = ((up + 1.0) * glu).astype(BF16)
        y = jnp.dot(act, w2b_ref[...], preferred_element_type=F32) + b2_ref[0]
        y_ref[...] = x_ref[:, d:d + 1] * y

    @pl.when(i >= nu_ref[0])
    def _():
        y_ref[...] = jnp.zeros_like(y_ref)


def _experts(layer, block_expert, n_used, xbuf, w1, b1, w2, b2):
    rows, dx = xbuf.shape
    de, d, f2 = w1.shape
    f = w2.shape[1]
    blk = lambda i, be, nu: (jnp.maximum(jnp.minimum(i, nu[0] - 1), 0), 0)
    wsel = lambda i, be, nu: (layer * N_EXPERTS + be[i], 0, 0)
    grid_spec = pltpu.PrefetchScalarGridSpec(
        num_scalar_prefetch=2,
        grid=(rows // EXPERT_TILE,),
        in_specs=[pl.BlockSpec((EXPERT_TILE, dx), blk),
                  pl.BlockSpec((1, d, f2), wsel),
                  pl.BlockSpec((1, 1, f2), wsel),
                  pl.BlockSpec((1, f, d), wsel),
                  pl.BlockSpec((1, 1, d), wsel)],
        out_specs=pl.BlockSpec((EXPERT_TILE, d), lambda i, be, nu: (i, 0)),
        scratch_shapes=[pltpu.VMEM((d, f2), BF16), pltpu.VMEM((f, d), BF16)],
    )
    return pl.pallas_call(
        _expert_body,
        grid_spec=grid_spec,
        out_shape=jax.ShapeDtypeStruct((rows, d), F32),
        compiler_params=_cparams(("arbitrary",)),
        name="expert_swiglu",
    )(block_expert, n_used, xbuf, w1, b1.reshape(de, 1, f2), w2, b2.reshape(de, 1, d))


def _combine_body(run_tbl, base_tbl, x1_ref, g2_ref, pos_ref, fg_ref, ybuf_hbm, o_ref, ys_ref, sem, *, final):
    i = pl.program_id(0)
    last = pl.num_programs(0) - 1
    slot = i % 2
    sb = ys_ref.shape[1]

    def copies(slot):
        return lambda off, base, rows: pltpu.make_async_copy(
            ybuf_hbm.at[pl.ds(base, rows)], ys_ref.at[slot, pl.ds(off, rows)], sem.at[slot])

    @pl.when(i == 0)
    def _():
        ys_ref[...] = jnp.zeros_like(ys_ref)
        _for_each_run(run_tbl, base_tbl, 0, copies(0), True)

    @pl.when(i < last)
    def _():
        _for_each_run(run_tbl, base_tbl, i + 1, copies(1 - slot), True)

    _for_each_run(run_tbl, base_tbl, i, copies(slot), False)

    _, any_hit = _slot_onehot(pos_ref[...], sb)
    perm = jnp.where(any_hit, 1.0, 0.0).astype(BF16)
    y = ys_ref[slot]
    y_hi = y.astype(BF16)
    y_lo = (y - y_hi.astype(F32)).astype(BF16)
    moe = (lax.dot_general(perm, y_hi, _TN, preferred_element_type=F32)
           + lax.dot_general(perm, y_lo, _TN, preferred_element_type=F32))
    out = x1_ref[...] + g2_ref[0] * moe
    if final:
        out = _rms(out) * fg_ref[...]
    o_ref[...] = out


def _combine(run_tbl, base_tbl, x1, g2, pos_t, final_g, ybuf, seq, final):
    n, d = x1.shape
    td = min(ROUTE_TILE, n)
    sb = _sorted_rows(td)
    grid_spec = pltpu.PrefetchScalarGridSpec(
        num_scalar_prefetch=2,
        grid=(n // td,),
        in_specs=[pl.BlockSpec((td, d), lambda i, *_: (i, 0)),
                  pl.BlockSpec((1, 1, d), lambda i, *_: ((i * td) // seq, 0, 0)),
                  pl.BlockSpec((TOP_K, td), lambda i, *_: (0, i)),
                  pl.BlockSpec((1, d), lambda i, *_: (0, 0)),
                  pl.BlockSpec(memory_space=pl.ANY)],
        out_specs=pl.BlockSpec((td, d), lambda i, *_: (i, 0)),
        scratch_shapes=[pltpu.VMEM((2, sb, d), F32), pltpu.SemaphoreType.DMA((2,))],
    )
    return pl.pallas_call(
        functools.partial(_combine_body, final=final),
        grid_spec=grid_spec,
        out_shape=jax.ShapeDtypeStruct((n, d), F32),
        compiler_params=_cparams(("arbitrary",)),
        name="combine_rows",
    )(run_tbl, base_tbl, x1, g2, pos_t, final_g, ybuf)


def _moe(layer, x1, h2, logits_t, g2, w1, b1, w2, b2, final_g, seq, final):
    n, d = h2.shape
    td = min(ROUTE_TILE, n)
    n_tiles = n // td
    assert seq % td == 0
    rows = (_round_up(n * TOP_K + n_tiles * N_EXPERTS * (RUN_ALIGN - 1), EXPERT_TILE)
            + N_EXPERTS * EXPERT_TILE)
    n_blocks = rows // EXPERT_TILE
    pos_t, gate_t, run_t = _route(logits_t)
    base_t, be, nu, tail = _layout(run_t, n_blocks)
    run_tbl = run_t[:, :n_tiles].T.astype(I32).reshape(-1)
    base_tbl = base_t[:, :n_tiles].T.reshape(-1)
    xbuf = _dispatch(run_tbl, base_tbl, tail[:, 1], tail[:, 0], h2, pos_t, gate_t, rows)
    ybuf = _experts(layer, be[0, :n_blocks], nu[0, :1], xbuf, w1, b1, w2, b2)
    return _combine(run_tbl, base_tbl, x1, g2, pos_t, final_g, ybuf, seq, final)


def kernel(x, c, positions, ada_w, ada_b, norm_mix, norm_ffn, w_in, b_forget, lambda_q1, lambda_k1,
           lambda_q2, lambda_k2, diff_subln, w_out, router_w, router_b, exp_w1, exp_b1, exp_w2, exp_b2,
           final_norm):
    b, s, d = x.shape
    depth = ada_w.shape[0]
    mod = _ada_mod(c, ada_w, ada_b)
    cosf, sinf = _rope_tables(positions)
    n_in = w_in.shape[2]
    pad_cols = 3072 + LANES - n_in
    flat = lambda a: a.reshape((a.shape[0] * a.shape[1],) + a.shape[2:])
    w1_all, b1_all, w2_all, b2_all = flat(exp_w1), flat(exp_b1), flat(exp_w2), flat(exp_b2)
    for l in range(depth):
        lam_init = 0.8 - 0.6 * math.exp(-0.3 * l)
        sh1, sc1, g1, sh2, sc2, g2 = [mod[l, :, i * d:(i + 1) * d].reshape(b, 1, d) for i in range(6)]
        w_pad = jnp.pad(w_in[l], ((0, 0), (0, pad_cols))).astype(BF16)
        bf_pad = jnp.pad(b_forget[l], (0, LANES - FOX_HEADS)).reshape(1, LANES)
        dq, dk, dv, fq, fk, fv, flog = _inproj(x, sc1, sh1, norm_mix[l].reshape(1, d), w_pad, bf_pad,
                                               cosf, sinf)
        cum = _forget_cumsum(flog)
        a_out = _diff_attention(dq, dk, dv, lambda_q1[l].reshape(1, -1), lambda_k1[l].reshape(1, -1),
                                lambda_q2[l].reshape(1, -1), lambda_k2[l].reshape(1, -1),
                                diff_subln[l].reshape(1, -1), lam_init)
        b_out = _fox_attention(fq, fk, fv, cum)
        x1, h2, logits_t = _outproj(x, a_out, b_out, w_out[l].astype(BF16), g1, sc2, sh2,
                                    norm_ffn[l].reshape(1, d), router_w[l].T,
                                    router_b[l].reshape(-1, 1))
        x = _moe(l, x1.reshape(b * s, d), h2, logits_t, g2, w1_all, b1_all, w2_all, b2_all,
                 final_norm.reshape(1, d), s, l == depth - 1).reshape(b, s, d)
    return x
```

```python
import functools
import math

import jax
import jax.numpy as jnp
from jax import lax
from jax.experimental import pallas as pl
from jax.experimental.pallas import tpu as pltpu

F32 = jnp.float32
BF16 = jnp.bfloat16
I32 = jnp.int32
HIGHEST = lax.Precision.HIGHEST

HEAD_DIM = 64
LANES = 128
SUBLANES = 8
CHUNK = 64
DIFF_HEADS = 4
FOX_HEADS = 8
D_DIFF = DIFF_HEADS * 2 * HEAD_DIM
D_FOX = FOX_HEADS * HEAD_DIM
ROPE_THETA = 500000.0
ROPE_DIM = HEAD_DIM // 4
ROPE_HALF = ROPE_DIM // 2
N_EXPERTS = 32
TOP_K = 4
SWIGLU_LIMIT = 7.0
SWIGLU_ALPHA = 1.702
NORM_EPS = 1e-6
NEG_BIG = -1e30
LOG2E = math.log2(math.e)
ATTN_TILE = 512
EXPERT_TILE = 512
ROUTE_TILE = 256
RUN_ALIGN = SUBLANES
VMEM_LIMIT = 56 * 1024 * 1024

_NT = (((1,), (1,)), ((), ()))
_TN = (((0,), (0,)), ((), ()))


def _cparams(sem, vmem=None):
    return pltpu.CompilerParams(dimension_semantics=sem, vmem_limit_bytes=vmem or VMEM_LIMIT)


def _rms(x):
    return x * lax.rsqrt(jnp.mean(x * x, axis=-1, keepdims=True) + NORM_EPS)


def _round_up(x, m):
    return (x + m - 1) // m * m


def _ada_body(c_ref, w_ref, b_ref, o_ref):
    c = c_ref[...]
    ca = c * jax.nn.sigmoid(c)
    o_ref[0] = jnp.dot(ca, w_ref[0], preferred_element_type=F32, precision=HIGHEST) + b_ref[0]


def _ada_mod(c, ada_w, ada_b):
    depth, d, n6 = ada_w.shape
    b = c.shape[0]
    tn = 1536
    return pl.pallas_call(
        _ada_body,
        grid=(depth, n6 // tn),
        in_specs=[pl.BlockSpec((b, d), lambda l, j: (0, 0)),
                  pl.BlockSpec((1, d, tn), lambda l, j: (l, 0, j)),
                  pl.BlockSpec((1, 1, tn), lambda l, j: (l, 0, j))],
        out_specs=pl.BlockSpec((1, b, tn), lambda l, j: (l, 0, j)),
        out_shape=jax.ShapeDtypeStruct((depth, b, n6), F32),
        compiler_params=_cparams(("arbitrary", "arbitrary")),
        name="ada_mod",
    )(c, ada_w, ada_b.reshape(depth, 1, n6))


def _rope_body(pos_ref, invf_ref, cos_ref, sin_ref):
    ang = pos_ref[0].astype(F32) * invf_ref[...]
    j = lax.broadcasted_iota(I32, ang.shape, 1) & (HEAD_DIM - 1)
    c = jnp.cos(ang)
    s = jnp.sin(ang)
    cos_ref[0] = jnp.where(j < ROPE_DIM, c, 1.0)
    sin_ref[0] = jnp.where(j < ROPE_HALF, -s, jnp.where(j < ROPE_DIM, s, 0.0))


def _rope_tables(positions):
    b, s = positions.shape
    ts = min(512, s)
    inv_freq = ROPE_THETA ** (-jnp.arange(0, ROPE_DIM, 2, dtype=F32) / ROPE_DIM)
    invf = inv_freq[jnp.arange(LANES) % ROPE_HALF].reshape(1, LANES)
    out = jax.ShapeDtypeStruct((b, s, LANES), F32)
    return pl.pallas_call(
        _rope_body,
        grid=(b, s // ts),
        in_specs=[pl.BlockSpec((1, ts, 1), lambda i, j: (i, j, 0)),
                  pl.BlockSpec((1, LANES), lambda i, j: (0, 0))],
        out_specs=[pl.BlockSpec((1, ts, LANES), lambda i, j: (i, j, 0))] * 2,
        out_shape=[out, out],
        compiler_params=_cparams(("arbitrary", "arbitrary")),
        name="rope_tables",
    )(positions.reshape(b, s, 1), invf)


def _inproj_body(x_ref, sc_ref, sh_ref, g_ref, w_ref, bf_ref, cos_ref, sin_ref,
                 dq_ref, dk_ref, dv_ref, fq_ref, fk_ref, fv_ref, fl_ref):
    h = _rms(x_ref[0]) * g_ref[...] * (1.0 + sc_ref[0]) + sh_ref[0]
    hb = h.astype(BF16)
    cosf = cos_ref[0]
    sinf = sin_ref[0]
    lane = lax.broadcasted_iota(I32, cosf.shape, 1)
    first = (lane & (HEAD_DIM - 1)) < ROPE_HALF

    def rope(p):
        nxt = pltpu.roll(p, LANES - ROPE_HALF, 1)
        prv = pltpu.roll(p, ROPE_HALF, 1)
        return p * cosf + jnp.where(first, nxt, prv) * sinf

    def proj(c0):
        return jnp.dot(hb, w_ref[:, c0:c0 + 512], preferred_element_type=F32)

    scale = HEAD_DIM ** -0.5 * LOG2E
    p = proj(0)
    for c in range(4):
        sl = slice(c * LANES, (c + 1) * LANES)
        dq_ref[0, :, sl] = (rope(p[:, sl]) * scale).astype(BF16)
    p = proj(512)
    for c in range(4):
        sl = slice(c * LANES, (c + 1) * LANES)
        dk_ref[0, :, sl] = rope(p[:, sl]).astype(BF16)
    dv_ref[0] = proj(1024).astype(BF16)
    fq_ref[0] = (proj(1536) * scale).astype(BF16)
    fk_ref[0] = proj(2048).astype(BF16)
    fv_ref[0] = proj(2560).astype(BF16)
    z = jnp.dot(hb, w_ref[:, 3072:3072 + LANES], preferred_element_type=F32) + bf_ref[...]
    fl_ref[0] = -(jnp.maximum(-z, 0.0) + jnp.log1p(jnp.exp(-jnp.abs(z))))


def _inproj(x, sc, sh, g, w_pad, bf_pad, cosf, sinf):
    b, s, d = x.shape
    tm = min(512, s)
    act = lambda w: jax.ShapeDtypeStruct((b, s, w), BF16)
    row = lambda w: pl.BlockSpec((1, tm, w), lambda i, j: (i, j, 0))
    vec = pl.BlockSpec((1, 1, d), lambda i, j: (i, 0, 0))
    return pl.pallas_call(
        _inproj_body,
        grid=(b, s // tm),
        in_specs=[row(d), vec, vec,
                  pl.BlockSpec((1, d), lambda i, j: (0, 0)),
                  pl.BlockSpec(w_pad.shape, lambda i, j: (0, 0)),
                  pl.BlockSpec((1, LANES), lambda i, j: (0, 0)),
                  row(LANES), row(LANES)],
        out_specs=[row(512)] * 6 + [row(LANES)],
        out_shape=[act(512)] * 6 + [jax.ShapeDtypeStruct((b, s, LANES), F32)],
        compiler_params=_cparams(("arbitrary", "arbitrary")),
        name="in_proj",
    )(x, sc, sh, g, w_pad, bf_pad, cosf, sinf)


def _cum_body(fl_ref, cum_ref):
    x = fl_ref[0]
    s = x.shape[0]
    row = lax.broadcasted_iota(I32, x.shape, 0)
    d = 1
    while d < s:
        x = x + jnp.where(row >= d, pltpu.roll(x, d, 0), 0.0)
        d *= 2
    cum_ref[0] = x * LOG2E


def _forget_cumsum(flog):
    b, s, _ = flog.shape
    spec = pl.BlockSpec((1, s, LANES), lambda i: (i, 0, 0))
    return pl.pallas_call(
        _cum_body,
        grid=(b,),
        in_specs=[spec],
        out_specs=spec,
        out_shape=jax.ShapeDtypeStruct((b, s, LANES), F32),
        compiler_params=_cparams(("arbitrary",)),
        name="forget_cumsum",
    )(flog)


def _causal_sweep(t, tile, state):
    def two(i, st):
        return tile(2 * i + 1, tile(2 * i, st, False), False)

    state = lax.fori_loop(0, t // 2, two, state)
    state = lax.fori_loop(t - (t & 1), t, lambda j, st: tile(j, st, False), state)
    return tile(t, state, True)


def _diff_body(q_ref, k_ref, v_ref, lq1_ref, lk1_ref, lq2_ref, lk2_ref, g_ref, o_ref, *, tq, lam_init):
    t = pl.program_id(2)
    q = q_ref[0]
    lane = lax.broadcasted_iota(I32, q.shape, 1)
    zero = jnp.zeros_like(q)
    qs = (jnp.where(lane < HEAD_DIM, q, zero), jnp.where(lane >= HEAD_DIM, q, zero))

    def tile(j, state, masked):
        off = pl.multiple_of(j * tq, tq)
        kt = k_ref[0, pl.ds(off, tq), :]
        vt = v_ref[0, pl.ds(off, tq), :]
        out = []
        for mp in range(2):
            s = lax.dot_general(qs[mp], kt, _NT, preferred_element_type=F32)
            if masked:
                r = lax.broadcasted_iota(I32, s.shape, 0)
                c = lax.broadcasted_iota(I32, s.shape, 1)
                shift = CHUNK.bit_length() - 1
                s = jnp.where((c >> shift) <= (r >> shift), s, NEG_BIG)
            m, l, acc = state[mp]
            m_new = jnp.maximum(m, jnp.max(s, axis=1, keepdims=True))
            alpha = jnp.exp2(m - m_new)
            p = jnp.exp2(s - m_new)
            l = alpha * l + jnp.sum(p, axis=1, keepdims=True)
            acc = alpha * acc + jnp.dot(p.astype(BF16), vt, preferred_element_type=F32)
            out.append((m_new, l, acc))
        return tuple(out)

    init = (jnp.full((tq, 1), NEG_BIG, F32), jnp.zeros((tq, 1), F32), jnp.zeros((tq, LANES), F32))
    (_, l0, a0), (_, l1, a1) = _causal_sweep(t, tile, (init, init))

    lam = (jnp.exp(jnp.sum(lq1_ref[...] * lk1_ref[...], axis=1, keepdims=True))
           - jnp.exp(jnp.sum(lq2_ref[...] * lk2_ref[...], axis=1, keepdims=True)) + lam_init)
    out = a0 / l0 - lam * (a1 / l1)
    o_ref[0] = (_rms(out) * g_ref[...] * (1.0 - lam_init)).astype(o_ref.dtype)


def _diff_attention(dq, dk, dv, lq1, lk1, lq2, lk2, subln, lam_init):
    b, s, _ = dq.shape
    tq = min(ATTN_TILE, s)
    qspec = pl.BlockSpec((1, tq, LANES), lambda i, h, t: (i, t, h))
    kvspec = pl.BlockSpec((1, s, LANES), lambda i, h, t: (i, 0, h))
    lspec = pl.BlockSpec((1, HEAD_DIM), lambda i, h, t: (0, 0))
    return pl.pallas_call(
        functools.partial(_diff_body, tq=tq, lam_init=lam_init),
        grid=(b, DIFF_HEADS, s // tq),
        in_specs=[qspec, kvspec, kvspec, lspec, lspec, lspec, lspec,
                  pl.BlockSpec((1, LANES), lambda i, h, t: (0, 0))],
        out_specs=qspec,
        out_shape=jax.ShapeDtypeStruct((b, s, D_DIFF), BF16),
        compiler_params=_cparams(("arbitrary", "arbitrary", "arbitrary")),
        name="diff_attention",
    )(dq, dk, dv, lq1, lk1, lq2, lk2, subln)


def _split3(c):
    hi = c.astype(BF16).astype(F32)
    r = c - hi
    mid = r.astype(BF16).astype(F32)
    return hi, mid, r - mid


def _bias_lanes(c, lane, first):
    hi, mid, lo = _split3(c)
    ones = jnp.where((lane >= HEAD_DIM) & (lane < HEAD_DIM + 6), 1.0, 0.0)
    return jnp.where(lane == first, hi, jnp.where(lane == first + 1, mid, jnp.where(lane == first + 2, lo, ones)))


def _head_column(c_all, h, lane):
    return jnp.broadcast_to(jnp.sum(jnp.where(lane == h, c_all, 0.0), axis=1, keepdims=True), c_all.shape)


def _fox_body(q_ref, k_ref, v_ref, cq_ref, ck_ref, o_ref, ka_ref, va_ref, *, tq):
    pair = pl.program_id(1)
    t = pl.program_id(2)
    heads = (2 * pair, 2 * pair + 1)
    lane = lax.broadcasted_iota(I32, (tq, LANES), 1)

    def per_head(x_f32, hh):
        return x_f32 if hh == 0 else pltpu.roll(x_f32, HEAD_DIM, 1)

    @pl.when(t == 0)
    def _():
        def chunk(ci, carry):
            rows = pl.ds(pl.multiple_of(ci * tq, tq), tq)
            kf = k_ref[0, rows, :].astype(F32)
            vf = v_ref[0, rows, :].astype(F32)
            c_all = ck_ref[0, rows, :]
            for hh in range(2):
                bias = _bias_lanes(-_head_column(c_all, heads[hh], lane), lane, HEAD_DIM + 3)
                ka_ref[hh, rows, :] = jnp.where(lane < HEAD_DIM, per_head(kf, hh), bias).astype(BF16)
                va_ref[hh, rows, :] = jnp.where(lane < HEAD_DIM, per_head(vf, hh), 1.0).astype(BF16)
            return carry

        lax.fori_loop(0, k_ref.shape[1] // tq, chunk, 0)

    qf = q_ref[0].astype(F32)
    cq_all = cq_ref[0]
    qs = [jnp.where(lane < HEAD_DIM, per_head(qf, hh),
                    _bias_lanes(_head_column(cq_all, heads[hh], lane), lane, HEAD_DIM)).astype(BF16)
          for hh in range(2)]

    def tile(j, state, masked):
        off = pl.multiple_of(j * tq, tq)
        out = []
        for hh in range(2):
            s = lax.dot_general(qs[hh], ka_ref[hh, pl.ds(off, tq), :], _NT, preferred_element_type=F32)
            if masked:
                r = lax.broadcasted_iota(I32, s.shape, 0)
                c = lax.broadcasted_iota(I32, s.shape, 1)
                s = jnp.where(c <= r, s, NEG_BIG)
            m, acc = state[hh]
            m_new = jnp.maximum(m, jnp.max(s, axis=1, keepdims=True))
            p = jnp.exp2(s - m_new).astype(BF16)
            acc = jnp.exp2(m - m_new) * acc + jnp.dot(p, va_ref[hh, pl.ds(off, tq), :],
                                                      preferred_element_type=F32)
            out.append((m_new, acc))
        return tuple(out)

    init = (jnp.full((tq, 1), NEG_BIG, F32), jnp.zeros((tq, LANES), F32))
    (_, a0), (_, a1) = _causal_sweep(t, tile, (init, init))
    o0 = a0 / pltpu.roll(a0, HEAD_DIM, 1)
    o1 = a1 / pltpu.roll(a1, HEAD_DIM, 1)
    o_ref[0] = jnp.where(lane < HEAD_DIM, o0, pltpu.roll(o1, HEAD_DIM, 1)).astype(o_ref.dtype)


def _fox_attention(fq, fk, fv, cum):
    b, s, _ = fq.shape
    tq = min(ATTN_TILE, s)
    qspec = pl.BlockSpec((1, tq, LANES), lambda i, p, t: (i, t, p))
    kvspec = pl.BlockSpec((1, s, LANES), lambda i, p, t: (i, 0, p))
    return pl.pallas_call(
        functools.partial(_fox_body, tq=tq),
        grid=(b, FOX_HEADS // 2, s // tq),
        in_specs=[qspec, kvspec, kvspec,
                  pl.BlockSpec((1, tq, LANES), lambda i, p, t: (i, t, 0)),
                  pl.BlockSpec((1, s, LANES), lambda i, p, t: (i, 0, 0))],
        out_specs=qspec,
        out_shape=jax.ShapeDtypeStruct((b, s, D_FOX), BF16),
        scratch_shapes=[pltpu.VMEM((2, s, LANES), BF16), pltpu.VMEM((2, s, LANES), BF16)],
        compiler_params=_cparams(("arbitrary", "arbitrary", "arbitrary")),
        name="fox_attention",
    )(fq, fk, fv, cum, cum)


def _outproj_body(x_ref, a_ref, b_ref, w_ref, g1_ref, sc_ref, sh_ref, ng_ref, rw_ref, rb_ref,
                  x1_ref, h2_ref, lg_ref):
    mix = (jnp.dot(a_ref[0], w_ref[:D_DIFF, :], preferred_element_type=F32)
           + jnp.dot(b_ref[0], w_ref[D_DIFF:, :], preferred_element_type=F32))
    x1 = x_ref[0] + g1_ref[0] * mix
    x1_ref[0] = x1
    h2 = _rms(x1) * ng_ref[...] * (1.0 + sc_ref[0]) + sh_ref[0]
    h_hi = h2.astype(BF16)
    h2_ref[...] = h_hi
    h_lo = (h2 - h_hi.astype(F32)).astype(BF16)
    rw = rw_ref[...]
    w_hi = rw.astype(BF16)
    w_lo = (rw - w_hi.astype(F32)).astype(BF16)
    nt = lambda a, b: lax.dot_general(a, b, _NT, preferred_element_type=F32)
    lg_ref[...] = nt(w_hi, h_hi) + (nt(w_hi, h_lo) + nt(w_lo, h_hi)) + rb_ref[...]


def _outproj(x, a_out, b_out, w_out, g1, sc2, sh2, ng, rwt, rb):
    b, s, d = x.shape
    n = b * s
    tm = min(512, s)
    nt = s // tm
    row = lambda w: pl.BlockSpec((1, tm, w), lambda i, j: (i, j, 0))
    vec = pl.BlockSpec((1, 1, d), lambda i, j: (i, 0, 0))
    const = lambda shp: pl.BlockSpec(shp, lambda i, j: (0,) * len(shp))
    return pl.pallas_call(
        _outproj_body,
        grid=(b, nt),
        in_specs=[row(d), row(D_DIFF), row(D_FOX), const(w_out.shape), vec, vec, vec,
                  const((1, d)), const(rwt.shape), const(rb.shape)],
        out_specs=[row(d),
                   pl.BlockSpec((tm, d), lambda i, j: (i * nt + j, 0)),
                   pl.BlockSpec((N_EXPERTS, tm), lambda i, j: (0, i * nt + j))],
        out_shape=[jax.ShapeDtypeStruct((b, s, d), F32),
                   jax.ShapeDtypeStruct((n, d), BF16),
                   jax.ShapeDtypeStruct((N_EXPERTS, n), F32)],
        compiler_params=_cparams(("arbitrary", "arbitrary")),
        name="out_proj",
    )(x, a_out, b_out, w_out, g1, sc2, sh2, ng, rwt, rb)


def _strict_lower(n):
    r = lax.broadcasted_iota(I32, (n, n), 0)
    c = lax.broadcasted_iota(I32, (n, n), 1)
    return (c < r).astype(F32)


def _align_up(x_f32, m):
    shift = m.bit_length() - 1
    return (((x_f32.astype(I32) + (m - 1)) >> shift) << shift).astype(F32)


def _route_body(lg_ref, pos_ref, gate_ref, cnt_ref):
    i = pl.program_id(0)

    @pl.when(i == 0)
    def _():
        cnt_ref[...] = jnp.zeros_like(cnt_ref)

    work = lg_ref[...]
    td = work.shape[1]
    eidx = lax.broadcasted_iota(I32, work.shape, 0)
    vals, hots = [], []
    for k in range(TOP_K):
        m = jnp.max(work, axis=0, keepdims=True)
        sel = jnp.min(jnp.where(work == m, eidx, N_EXPERTS), axis=0, keepdims=True)
        hot = eidx == sel
        vals.append(m)
        hots.append(hot)
        work = jnp.where(hot, -jnp.inf, work)
    ex = [jnp.exp(v - vals[0]) for v in vals]
    den = ex[0] + ex[1] + ex[2] + ex[3]
    for k in range(TOP_K):
        gate_ref[k:k + 1, :] = ex[k] / den

    chosen = hots[0] | hots[1] | hots[2] | hots[3]
    r = lax.broadcasted_iota(I32, (td, td), 0)
    c = lax.broadcasted_iota(I32, (td, td), 1)
    before = (r < c).astype(BF16)
    earlier = jnp.dot(chosen.astype(BF16), before, preferred_element_type=F32)
    cnt = jnp.sum(chosen.astype(F32), axis=1, keepdims=True)
    run = _align_up(cnt, RUN_ALIGN)
    start = jnp.dot(_strict_lower(N_EXPERTS), jnp.broadcast_to(run, (N_EXPERTS, LANES)),
                    preferred_element_type=F32, precision=HIGHEST)[:, 0:1]
    slot = start + earlier
    for k in range(TOP_K):
        pos_ref[k:k + 1, :] = jnp.sum(jnp.where(hots[k], slot, 0.0), axis=0, keepdims=True).astype(I32)
    lane = lax.broadcasted_iota(I32, cnt_ref.shape, 1)
    cnt_ref[...] = jnp.where(lane == i, run, cnt_ref[...])


def _route(logits_t):
    e, n = logits_t.shape
    td = min(ROUTE_TILE, n)
    ntp = _round_up(n // td, LANES)
    tok = pl.BlockSpec((TOP_K, td), lambda i: (0, i))
    return pl.pallas_call(
        _route_body,
        grid=(n // td,),
        in_specs=[pl.BlockSpec((e, td), lambda i: (0, i))],
        out_specs=[tok, tok, pl.BlockSpec((e, ntp), lambda i: (0, 0))],
        out_shape=[jax.ShapeDtypeStruct((TOP_K, n), I32),
                   jax.ShapeDtypeStruct((TOP_K, n), F32),
                   jax.ShapeDtypeStruct((e, ntp), F32)],
        compiler_params=_cparams(("arbitrary",)),
        name="route_topk",
    )(logits_t)


def _layout_body(run_ref, base_ref, be_ref, nu_ref, tail_ref):
    run = run_ref[...]
    ntp = run.shape[1]
    total = jnp.sum(run, axis=1, keepdims=True)
    region = _align_up(total, EXPERT_TILE)
    pstart = jnp.dot(_strict_lower(N_EXPERTS), jnp.broadcast_to(region, (N_EXPERTS, LANES)),
                     preferred_element_type=F32, precision=HIGHEST)[:, 0:1]
    ti = lax.broadcasted_iota(I32, (ntp, ntp), 0)
    tj = lax.broadcasted_iota(I32, (ntp, ntp), 1)
    within = jnp.dot(run, (ti < tj).astype(F32), preferred_element_type=F32, precision=HIGHEST)
    base_ref[...] = (pstart + within).astype(I32)
    pend = pstart + region
    blk = (lax.broadcasted_iota(I32, (N_EXPERTS, be_ref.shape[1]), 1) * EXPERT_TILE).astype(F32)
    be = jnp.sum((pend <= blk).astype(I32), axis=0, keepdims=True)
    be_ref[...] = jnp.minimum(be, N_EXPERTS - 1)
    used = jnp.sum(region, axis=0, keepdims=True).astype(I32) >> (EXPERT_TILE.bit_length() - 1)
    nu_ref[...] = jnp.broadcast_to(used, nu_ref.shape)
    lane = lax.broadcasted_iota(I32, tail_ref.shape, 1)
    tail_ref[...] = jnp.where(lane == 0, pstart + total, jnp.where(lane == 1, region - total, 0.0)).astype(I32)


def _layout(run_t, n_blocks):
    e, ntp = run_t.shape
    nbp = _round_up(n_blocks, LANES)
    full = lambda shp: pl.BlockSpec(shp, lambda: (0,) * len(shp))
    return pl.pallas_call(
        _layout_body,
        in_specs=[full((e, ntp))],
        out_specs=[full((e, ntp)), full((1, nbp)), full((1, LANES)), full((e, LANES))],
        out_shape=[jax.ShapeDtypeStruct((e, ntp), I32),
                   jax.ShapeDtypeStruct((1, nbp), I32),
                   jax.ShapeDtypeStruct((1, LANES), I32),
                   jax.ShapeDtypeStruct((e, LANES), I32)],
        name="expert_layout",
    )(run_t)


_RUN_LEVELS = tuple(1 << b for b in range(ROUTE_TILE.bit_length() - 1, RUN_ALIGN.bit_length() - 2, -1))


def _for_each_run(run_tbl, base_tbl, tile, make_copy, start):
    def body(e, off):
        cnt = run_tbl[tile * N_EXPERTS + e]
        base = base_tbl[tile * N_EXPERTS + e]
        done = jnp.int32(0)
        for lvl in _RUN_LEVELS:
            bit = cnt & lvl

            @pl.when(bit != 0)
            def _(done=done, lvl=lvl):
                cp = make_copy(pl.multiple_of(off + done, RUN_ALIGN), pl.multiple_of(base + done, RUN_ALIGN), lvl)
                if start:
                    cp.start()
                else:
                    cp.wait()
            done = done + bit
        return off + cnt

    lax.fori_loop(0, N_EXPERTS, body, jnp.int32(0))


def _wait_for_tile(run_tbl, tile, make_copy, sb):
    total = lax.fori_loop(0, N_EXPERTS, lambda e, acc: acc + run_tbl[tile * N_EXPERTS + e], jnp.int32(0))
    lvl = 1 << (sb.bit_length() - 1)
    while lvl >= RUN_ALIGN:
        @pl.when((total & lvl) != 0)
        def _(lvl=lvl):
            make_copy(0, 0, lvl).wait()
        lvl //= 2


def _slot_onehot(pos, sb):
    j = lax.broadcasted_iota(I32, (sb, pos.shape[1]), 0)
    hits = [j == pos[k:k + 1, :] for k in range(TOP_K)]
    return hits, (hits[0] | hits[1] | hits[2] | hits[3])


def _sorted_rows(td):
    return _round_up(TOP_K * td + N_EXPERTS * (RUN_ALIGN - 1), LANES)


def _dispatch_body(run_tbl, base_tbl, tail_len, tail_start, h_ref, pos_ref, gate_ref, xbuf_hbm,
                   sorted_ref, zero_ref, sem, zsem, *, d):
    i = pl.program_id(0)
    last = pl.num_programs(0) - 1
    slot = i % 2
    sb = sorted_ref.shape[1]

    def zero_copies(off, base, rows):
        del off
        return pltpu.make_async_copy(zero_ref.at[pl.ds(0, rows)], xbuf_hbm.at[pl.ds(base, rows)], zsem)

    @pl.when(i == 0)
    def _():
        zero_ref[...] = jnp.zeros_like(zero_ref)
        _for_each_run(tail_len, tail_start, 0, zero_copies, True)

    hits, any_hit = _slot_onehot(pos_ref[...], sb)
    perm = jnp.where(any_hit, 1.0, 0.0).astype(BF16)
    sorted_ref[slot, :, :d] = jnp.dot(perm, h_ref[...], preferred_element_type=F32)
    gates = gate_ref[...]
    gsel = jnp.where(hits[0], gates[0:1, :], 0.0)
    for k in range(1, TOP_K):
        gsel = gsel + jnp.where(hits[k], gates[k:k + 1, :], 0.0)
    sorted_ref[slot, :, d:] = jnp.broadcast_to(jnp.sum(gsel, axis=1, keepdims=True), (sb, LANES))

    def copies(slot):
        return lambda off, base, rows: pltpu.make_async_copy(
            sorted_ref.at[slot, pl.ds(off, rows)], xbuf_hbm.at[pl.ds(base, rows)], sem.at[slot])

    _for_each_run(run_tbl, base_tbl, i, copies(slot), True)

    @pl.when(i > 0)
    def _():
        _wait_for_tile(run_tbl, i - 1, copies(1 - slot), sb)

    @pl.when(i == last)
    def _():
        _wait_for_tile(run_tbl, i, copies(slot), sb)
        _for_each_run(tail_len, tail_start, 0, zero_copies, False)


def _dispatch(run_tbl, base_tbl, tail_len, tail_start, h2, pos_t, gate_t, rows):
    n, d = h2.shape
    td = min(ROUTE_TILE, n)
    sb = _sorted_rows(td)
    dx = d + LANES
    assert EXPERT_TILE <= 2 * _RUN_LEVELS[0]
    tok = pl.BlockSpec((TOP_K, td), lambda i, *_: (0, i))
    grid_spec = pltpu.PrefetchScalarGridSpec(
        num_scalar_prefetch=4,
        grid=(n // td,),
        in_specs=[pl.BlockSpec((td, d), lambda i, *_: (i, 0)), tok, tok],
        out_specs=pl.BlockSpec(memory_space=pl.ANY),
        scratch_shapes=[pltpu.VMEM((2, sb, dx), F32), pltpu.VMEM((_RUN_LEVELS[0], dx), F32),
                        pltpu.SemaphoreType.DMA((2,)), pltpu.SemaphoreType.DMA],
    )
    return pl.pallas_call(
        functools.partial(_dispatch_body, d=d),
        grid_spec=grid_spec,
        out_shape=jax.ShapeDtypeStruct((rows, dx), F32),
        compiler_params=_cparams(("arbitrary",)),
        name="dispatch_rows",
    )(run_tbl, base_tbl, tail_len, tail_start, h2, pos_t, gate_t)


def _expert_body(be_ref, nu_ref, x_ref, w1_ref, b1_ref, w2_ref, b2_ref, y_ref, w1b_ref, w2b_ref):
    i = pl.program_id(0)
    d_ff, d = w2_ref.shape[1], w2_ref.shape[2]

    @pl.when((i == 0) | (be_ref[i] != be_ref[jnp.maximum(i - 1, 0)]))
    def _():
        w1b_ref[...] = w1_ref[0].astype(BF16)
        w2b_ref[...] = w2_ref[0].astype(BF16)

    @pl.when(i < nu_ref[0])
    def _():
        gu = jnp.dot(x_ref[:, :d].astype(BF16), w1b_ref[...], preferred_element_type=F32) + b1_ref[0]
        gate = jnp.minimum(gu[:, :d_ff], SWIGLU_LIMIT)
        up = jnp.clip(gu[:, d_ff:], -SWIGLU_LIMIT, SWIGLU_LIMIT)
        glu = gate * jax.nn.sigmoid(gate * SWIGLU_ALPHA)
        act = ((up + 1.0) * glu).astype(BF16)
        y = jnp.dot(act, w2b_ref[...], preferred_element_type=F32) + b2_ref[0]
        y_ref[...] = x_ref[:, d:d + 1] * y

    @pl.when(i >= nu_ref[0])
    def _():
        y_ref[...] = jnp.zeros_like(y_ref)


def _experts(layer, block_expert, n_used, xbuf, w1, b1, w2, b2):
    rows, dx = xbuf.shape
    de, d, f2 = w1.shape
    f = w2.shape[1]
    blk = lambda i, be, nu: (jnp.maximum(jnp.minimum(i, nu[0] - 1), 0), 0)
    wsel = lambda i, be, nu: (layer * N_EXPERTS + be[i], 0, 0)
    grid_spec = pltpu.PrefetchScalarGridSpec(
        num_scalar_prefetch=2,
        grid=(rows // EXPERT_TILE,),
        in_specs=[pl.BlockSpec((EXPERT_TILE, dx), blk),
                  pl.BlockSpec((1, d, f2), wsel),
                  pl.BlockSpec((1, 1, f2), wsel),
                  pl.BlockSpec((1, f, d), wsel),
                  pl.BlockSpec((1, 1, d), wsel)],
        out_specs=pl.BlockSpec((EXPERT_TILE, d), lambda i, be, nu: (i, 0)),
        scratch_shapes=[pltpu.VMEM((d, f2), BF16), pltpu.VMEM((f, d), BF16)],
    )
    return pl.pallas_call(
        _expert_body,
        grid_spec=grid_spec,
        out_shape=jax.ShapeDtypeStruct((rows, d), F32),
        compiler_params=_cparams(("arbitrary",)),
        name="expert_swiglu",
    )(block_expert, n_used, xbuf, w1, b1.reshape(de, 1, f2), w2, b2.reshape(de, 1, d))


def _combine_body(run_tbl, base_tbl, x1_ref, g2_ref, pos_ref, fg_ref, ybuf_hbm, o_ref, ys_ref, sem, *, final):
    i = pl.program_id(0)
    last = pl.num_programs(0) - 1
    slot = i % 2
    sb = ys_ref.shape[1]

    def copies(slot):
        return lambda off, base, rows: pltpu.make_async_copy(
            ybuf_hbm.at[pl.ds(base, rows)], ys_ref.at[slot, pl.ds(off, rows)], sem.at[slot])

    @pl.when(i == 0)
    def _():
        ys_ref[...] = jnp.zeros_like(ys_ref)
        _for_each_run(run_tbl, base_tbl, 0, copies(0), True)

    @pl.when(i < last)
    def _():
        _for_each_run(run_tbl, base_tbl, i + 1, copies(1 - slot), True)

    _wait_for_tile(run_tbl, i, copies(slot), sb)

    _, any_hit = _slot_onehot(pos_ref[...], sb)
    perm = jnp.where(any_hit, 1.0, 0.0).astype(BF16)
    y = ys_ref[slot]
    y_hi = y.astype(BF16)
    y_lo = (y - y_hi.astype(F32)).astype(BF16)
    moe = (lax.dot_general(perm, y_hi, _TN, preferred_element_type=F32)
           + lax.dot_general(perm, y_lo, _TN, preferred_element_type=F32))
    out = x1_ref[...] + g2_ref[0] * moe
    if final:
        out = _rms(out) * fg_ref[...]
    o_ref[...] = out


def _combine(run_tbl, base_tbl, x1, g2, pos_t, final_g, ybuf, seq, final):
    n, d = x1.shape
    td = min(ROUTE_TILE, n)
    sb = _sorted_rows(td)
    grid_spec = pltpu.PrefetchScalarGridSpec(
        num_scalar_prefetch=2,
        grid=(n // td,),
        in_specs=[pl.BlockSpec((td, d), lambda i, *_: (i, 0)),
                  pl.BlockSpec((1, 1, d), lambda i, *_: ((i * td) // seq, 0, 0)),
                  pl.BlockSpec((TOP_K, td), lambda i, *_: (0, i)),
                  pl.BlockSpec((1, d), lambda i, *_: (0, 0)),
                  pl.BlockSpec(memory_space=pl.ANY)],
        out_specs=pl.BlockSpec((td, d), lambda i, *_: (i, 0)),
        scratch_shapes=[pltpu.VMEM((2, sb, d), F32), pltpu.SemaphoreType.DMA((2,))],
    )
    return pl.pallas_call(
        functools.partial(_combine_body, final=final),
        grid_spec=grid_spec,
        out_shape=jax.ShapeDtypeStruct((n, d), F32),
        compiler_params=_cparams(("arbitrary",)),
        name="combine_rows",
    )(run_tbl, base_tbl, x1, g2, pos_t, final_g, ybuf)


def _moe(layer, x1, h2, logits_t, g2, w1, b1, w2, b2, final_g, seq, final):
    n, d = h2.shape
    td = min(ROUTE_TILE, n)
    n_tiles = n // td
    assert seq % td == 0
    rows = (_round_up(n * TOP_K + n_tiles * N_EXPERTS * (RUN_ALIGN - 1), EXPERT_TILE)
            + N_EXPERTS * EXPERT_TILE)
    n_blocks = rows // EXPERT_TILE
    pos_t, gate_t, run_t = _route(logits_t)
    base_t, be, nu, tail = _layout(run_t, n_blocks)
    run_tbl = run_t[:, :n_tiles].T.astype(I32).reshape(-1)
    base_tbl = base_t[:, :n_tiles].T.reshape(-1)
    xbuf = _dispatch(run_tbl, base_tbl, tail[:, 1], tail[:, 0], h2, pos_t, gate_t, rows)
    ybuf = _experts(layer, be[0, :n_blocks], nu[0, :1], xbuf, w1, b1, w2, b2)
    return _combine(run_tbl, base_tbl, x1, g2, pos_t, final_g, ybuf, seq, final)


def kernel(x, c, positions, ada_w, ada_b, norm_mix, norm_ffn, w_in, b_forget, lambda_q1, lambda_k1,
           lambda_q2, lambda_k2, diff_subln, w_out, router_w, router_b, exp_w1, exp_b1, exp_w2, exp_b2,
           final_norm):
    b, s, d = x.shape
    depth = ada_w.shape[0]
    mod = _ada_mod(c, ada_w, ada_b)
    cosf, sinf = _rope_tables(positions)
    n_in = w_in.shape[2]
    pad_cols = 3072 + LANES - n_in
    flat = lambda a: a.reshape((a.shape[0] * a.shape[1],) + a.shape[2:])
    w1_all, b1_all, w2_all, b2_all = flat(exp_w1), flat(exp_b1), flat(exp_w2), flat(exp_b2)
    for l in range(depth):
        lam_init = 0.8 - 0.6 * math.exp(-0.3 * l)
        sh1, sc1, g1, sh2, sc2, g2 = [mod[l, :, i * d:(i + 1) * d].reshape(b, 1, d) for i in range(6)]
        w_pad = jnp.pad(w_in[l], ((0, 0), (0, pad_cols))).astype(BF16)
        bf_pad = jnp.pad(b_forget[l], (0, LANES - FOX_HEADS)).reshape(1, LANES)
        dq, dk, dv, fq, fk, fv, flog = _inproj(x, sc1, sh1, norm_mix[l].reshape(1, d), w_pad, bf_pad,
                                               cosf, sinf)
        cum = _forget_cumsum(flog)
        a_out = _diff_attention(dq, dk, dv, lambda_q1[l].reshape(1, -1), lambda_k1[l].reshape(1, -1),
                                lambda_q2[l].reshape(1, -1), lambda_k2[l].reshape(1, -1),
                                diff_subln[l].reshape(1, -1), lam_init)
        b_out = _fox_attention(fq, fk, fv, cum)
        x1, h2, logits_t = _outproj(x, a_out, b_out, w_out[l].astype(BF16), g1, sc2, sh2,
                                    norm_ffn[l].reshape(1, d), router_w[l].T,
                                    router_b[l].reshape(-1, 1))
        x = _moe(l, x1.reshape(b * s, d), h2, logits_t, g2, w1_all, b1_all, w2_all, b2_all,
                 final_norm.reshape(1, d), s, l == depth - 1).reshape(b, s, d)
    return x
```

```python
import functools
import math

import jax
import jax.numpy as jnp
from jax import lax
from jax.experimental import pallas as pl
from jax.experimental.pallas import tpu as pltpu

F32 = jnp.float32
BF16 = jnp.bfloat16
I32 = jnp.int32
HIGHEST = lax.Precision.HIGHEST

HEAD_DIM = 64
LANES = 128
SUBLANES = 8
CHUNK = 64
DIFF_HEADS = 4
FOX_HEADS = 8
D_DIFF = DIFF_HEADS * 2 * HEAD_DIM
D_FOX = FOX_HEADS * HEAD_DIM
ROPE_THETA = 500000.0
ROPE_DIM = HEAD_DIM // 4
ROPE_HALF = ROPE_DIM // 2
N_EXPERTS = 32
TOP_K = 4
SWIGLU_LIMIT = 7.0
SWIGLU_ALPHA = 1.702
NORM_EPS = 1e-6
NEG_BIG = -1e30
LOG2E = math.log2(math.e)
ATTN_Q_TILE = 1024
ATTN_K_TILE = 512
EXPERT_TILE = 512
ROUTE_TILE = 256
RUN_ALIGN = SUBLANES
VMEM_LIMIT = 56 * 1024 * 1024

_NT = (((1,), (1,)), ((), ()))
_TN = (((0,), (0,)), ((), ()))


def _cparams(sem, vmem=None):
    return pltpu.CompilerParams(dimension_semantics=sem, vmem_limit_bytes=vmem or VMEM_LIMIT)


def _rms(x):
    return x * lax.rsqrt(jnp.mean(x * x, axis=-1, keepdims=True) + NORM_EPS)


def _round_up(x, m):
    return (x + m - 1) // m * m


def _attn_tiles(s):
    tq = min(ATTN_Q_TILE, s)
    return tq, min(ATTN_K_TILE, tq)


def _ada_body(c_ref, w_ref, b_ref, o_ref):
    c = c_ref[...]
    ca = c * jax.nn.sigmoid(c)
    o_ref[0] = jnp.dot(ca, w_ref[0], preferred_element_type=F32, precision=HIGHEST) + b_ref[0]


def _ada_mod(c, ada_w, ada_b):
    depth, d, n6 = ada_w.shape
    b = c.shape[0]
    tn = 1536
    return pl.pallas_call(
        _ada_body,
        grid=(depth, n6 // tn),
        in_specs=[pl.BlockSpec((b, d), lambda l, j: (0, 0)),
                  pl.BlockSpec((1, d, tn), lambda l, j: (l, 0, j)),
                  pl.BlockSpec((1, 1, tn), lambda l, j: (l, 0, j))],
        out_specs=pl.BlockSpec((1, b, tn), lambda l, j: (l, 0, j)),
        out_shape=jax.ShapeDtypeStruct((depth, b, n6), F32),
        compiler_params=_cparams(("arbitrary", "arbitrary")),
        name="ada_mod",
    )(c, ada_w, ada_b.reshape(depth, 1, n6))


def _rope_body(pos_ref, invf_ref, cos_ref, sin_ref):
    ang = pos_ref[0].astype(F32) * invf_ref[...]
    j = lax.broadcasted_iota(I32, ang.shape, 1) & (HEAD_DIM - 1)
    c = jnp.cos(ang)
    s = jnp.sin(ang)
    cos_ref[0] = jnp.where(j < ROPE_DIM, c, 1.0)
    sin_ref[0] = jnp.where(j < ROPE_HALF, -s, jnp.where(j < ROPE_DIM, s, 0.0))


def _rope_tables(positions):
    b, s = positions.shape
    ts = min(512, s)
    inv_freq = ROPE_THETA ** (-jnp.arange(0, ROPE_DIM, 2, dtype=F32) / ROPE_DIM)
    invf = inv_freq[jnp.arange(LANES) % ROPE_HALF].reshape(1, LANES)
    out = jax.ShapeDtypeStruct((b, s, LANES), F32)
    return pl.pallas_call(
        _rope_body,
        grid=(b, s // ts),
        in_specs=[pl.BlockSpec((1, ts, 1), lambda i, j: (i, j, 0)),
                  pl.BlockSpec((1, LANES), lambda i, j: (0, 0))],
        out_specs=[pl.BlockSpec((1, ts, LANES), lambda i, j: (i, j, 0))] * 2,
        out_shape=[out, out],
        compiler_params=_cparams(("arbitrary", "arbitrary")),
        name="rope_tables",
    )(positions.reshape(b, s, 1), invf)


def _inproj_body(x_ref, sc_ref, sh_ref, g_ref, w_ref, bf_ref, cos_ref, sin_ref,
                 dq_ref, dk_ref, dv_ref, fq_ref, fk_ref, fv_ref, fl_ref):
    h = _rms(x_ref[0]) * g_ref[...] * (1.0 + sc_ref[0]) + sh_ref[0]
    hb = h.astype(BF16)
    cosf = cos_ref[0]
    sinf = sin_ref[0]
    lane = lax.broadcasted_iota(I32, cosf.shape, 1)
    first = (lane & (HEAD_DIM - 1)) < ROPE_HALF

    def rope(p):
        nxt = pltpu.roll(p, LANES - ROPE_HALF, 1)
        prv = pltpu.roll(p, ROPE_HALF, 1)
        return p * cosf + jnp.where(first, nxt, prv) * sinf

    def proj(c0):
        return jnp.dot(hb, w_ref[:, c0:c0 + 512], preferred_element_type=F32)

    scale = HEAD_DIM ** -0.5 * LOG2E
    p = proj(0)
    for c in range(4):
        sl = slice(c * LANES, (c + 1) * LANES)
        dq_ref[0, :, sl] = (rope(p[:, sl]) * scale).astype(BF16)
    p = proj(512)
    for c in range(4):
        sl = slice(c * LANES, (c + 1) * LANES)
        dk_ref[0, :, sl] = rope(p[:, sl]).astype(BF16)
    dv_ref[0] = proj(1024).astype(BF16)
    fq_ref[0] = (proj(1536) * scale).astype(BF16)
    fk_ref[0] = proj(2048).astype(BF16)
    fv_ref[0] = proj(2560).astype(BF16)
    z = jnp.dot(hb, w_ref[:, 3072:3072 + LANES], preferred_element_type=F32) + bf_ref[...]
    fl_ref[0] = -(jnp.maximum(-z, 0.0) + jnp.log1p(jnp.exp(-jnp.abs(z))))


def _inproj(x, sc, sh, g, w_pad, bf_pad, cosf, sinf):
    b, s, d = x.shape
    tm = min(512, s)
    act = lambda w: jax.ShapeDtypeStruct((b, s, w), BF16)
    row = lambda w: pl.BlockSpec((1, tm, w), lambda i, j: (i, j, 0))
    vec = pl.BlockSpec((1, 1, d), lambda i, j: (i, 0, 0))
    return pl.pallas_call(
        _inproj_body,
        grid=(b, s // tm),
        in_specs=[row(d), vec, vec,
                  pl.BlockSpec((1, d), lambda i, j: (0, 0)),
                  pl.BlockSpec(w_pad.shape, lambda i, j: (0, 0)),
                  pl.BlockSpec((1, LANES), lambda i, j: (0, 0)),
                  row(LANES), row(LANES)],
        out_specs=[row(512)] * 6 + [row(LANES)],
        out_shape=[act(512)] * 6 + [jax.ShapeDtypeStruct((b, s, LANES), F32)],
        compiler_params=_cparams(("arbitrary", "arbitrary")),
        name="in_proj",
    )(x, sc, sh, g, w_pad, bf_pad, cosf, sinf)


def _cum_body(fl_ref, cum_ref):
    x = fl_ref[0]
    s = x.shape[0]
    row = lax.broadcasted_iota(I32, x.shape, 0)
    d = 1
    while d < s:
        x = x + jnp.where(row >= d, pltpu.roll(x, d, 0), 0.0)
        d *= 2
    cum_ref[0] = x * LOG2E


def _forget_cumsum(flog):
    b, s, _ = flog.shape
    spec = pl.BlockSpec((1, s, LANES), lambda i: (i, 0, 0))
    return pl.pallas_call(
        _cum_body,
        grid=(b,),
        in_specs=[spec],
        out_specs=spec,
        out_shape=jax.ShapeDtypeStruct((b, s, LANES), F32),
        compiler_params=_cparams(("arbitrary",)),
        name="forget_cumsum",
    )(flog)


def _causal_sweep(t, tq, tk, tile, state):
    ratio = tq // tk
    assert ratio in (1, 2) and ratio * tk == tq
    n_full = t * ratio

    def steps(k, base, trips, st):
        def body(i, st):
            for u in range(k):
                st = tile(base + k * i + u, st, False, 0)
            return st
        return lax.fori_loop(0, trips, body, st)

    state = steps(4, 0, n_full // 4, state)
    state = steps(2, (n_full // 4) * 4, (n_full // 2) % 2, state)
    if ratio == 1:
        state = steps(1, (n_full // 2) * 2, n_full % 2, state)
    state = tile(n_full, state, True, 0)
    return state, (tile(n_full + 1, state, True, tk) if ratio == 2 else None)


def _finish_rows(state, lower, tk, finish):
    if lower is None:
        return finish(state)
    top = finish(jax.tree.map(lambda a: a[:tk], state))
    return jnp.concatenate([top, finish(lower)], axis=0)


def _diff_body(q_ref, k_ref, v_ref, lq1_ref, lk1_ref, lq2_ref, lk2_ref, g_ref, o_ref, *, tq, tk, lam_init):
    t = pl.program_id(2)
    q = q_ref[0]
    lane = lax.broadcasted_iota(I32, q.shape, 1)
    zero = jnp.zeros_like(q)
    qs = (jnp.where(lane < HEAD_DIM, q, zero), jnp.where(lane >= HEAD_DIM, q, zero))

    def tile(j, state, masked, lo):
        off = pl.multiple_of(j * tk, tk)
        kt = k_ref[0, pl.ds(off, tk), :]
        vt = v_ref[0, pl.ds(off, tk), :]
        out = []
        for mp in range(2):
            s = lax.dot_general(qs[mp][lo:], kt, _NT, preferred_element_type=F32)
            if masked:
                r = lax.broadcasted_iota(I32, s.shape, 0)
                c = lax.broadcasted_iota(I32, s.shape, 1)
                shift = CHUNK.bit_length() - 1
                s = jnp.where((c >> shift) <= (r >> shift), s, NEG_BIG)
            m, l, acc = state[mp]
            m_new = jnp.maximum(m[lo:], jnp.max(s, axis=1, keepdims=True))
            alpha = jnp.exp2(m[lo:] - m_new)
            p = jnp.exp2(s - m_new[:, :1])
            l_new = alpha * l[lo:] + jnp.sum(p, axis=1, keepdims=True)
            acc_new = alpha * acc[lo:] + jnp.dot(p.astype(BF16), vt, preferred_element_type=F32)
            out.append((m_new, l_new, acc_new))
        return tuple(out)

    lam = (jnp.exp(jnp.sum(lq1_ref[...] * lk1_ref[...], axis=1, keepdims=True))
           - jnp.exp(jnp.sum(lq2_ref[...] * lk2_ref[...], axis=1, keepdims=True)) + lam_init)

    def finish(st):
        (_, l0, a0), (_, l1, a1) = st
        out = a0 / l0 - lam * (a1 / l1)
        return (_rms(out) * g_ref[...] * (1.0 - lam_init)).astype(o_ref.dtype)

    init = (jnp.full((tq, LANES), NEG_BIG, F32), jnp.zeros((tq, LANES), F32), jnp.zeros((tq, LANES), F32))
    o_ref[0] = _finish_rows(*_causal_sweep(t, tq, tk, tile, (init, init)), tk, finish)


def _diff_attention(dq, dk, dv, lq1, lk1, lq2, lk2, subln, lam_init):
    b, s, _ = dq.shape
    tq, tk = _attn_tiles(s)
    qspec = pl.BlockSpec((1, tq, LANES), lambda i, h, t: (i, t, h))
    kvspec = pl.BlockSpec((1, s, LANES), lambda i, h, t: (i, 0, h))
    lspec = pl.BlockSpec((1, HEAD_DIM), lambda i, h, t: (0, 0))
    return pl.pallas_call(
        functools.partial(_diff_body, tq=tq, tk=tk, lam_init=lam_init),
        grid=(b, DIFF_HEADS, s // tq),
        in_specs=[qspec, kvspec, kvspec, lspec, lspec, lspec, lspec,
                  pl.BlockSpec((1, LANES), lambda i, h, t: (0, 0))],
        out_specs=qspec,
        out_shape=jax.ShapeDtypeStruct((b, s, D_DIFF), BF16),
        compiler_params=_cparams(("arbitrary", "arbitrary", "arbitrary")),
        name="diff_attention",
    )(dq, dk, dv, lq1, lk1, lq2, lk2, subln)


def _split3(c):
    hi = c.astype(BF16).astype(F32)
    r = c - hi
    mid = r.astype(BF16).astype(F32)
    return hi, mid, r - mid


def _bias_lanes(c, lane, first):
    hi, mid, lo = _split3(c)
    ones = jnp.where((lane >= HEAD_DIM) & (lane < HEAD_DIM + 6), 1.0, 0.0)
    return jnp.where(lane == first, hi, jnp.where(lane == first + 1, mid, jnp.where(lane == first + 2, lo, ones)))


def _head_column(c_all, h, lane):
    return jnp.broadcast_to(jnp.sum(jnp.where(lane == h, c_all, 0.0), axis=1, keepdims=True), c_all.shape)


def _fox_body(q_ref, k_ref, v_ref, cq_ref, ck_ref, o_ref, ka_ref, va_ref, *, tq, tk):
    pair = pl.program_id(1)
    t = pl.program_id(2)
    heads = (2 * pair, 2 * pair + 1)
    lane = lax.broadcasted_iota(I32, (tq, LANES), 1)
    klane = lax.broadcasted_iota(I32, (tk, LANES), 1)

    def per_head(x_f32, hh):
        return x_f32 if hh == 0 else pltpu.roll(x_f32, HEAD_DIM, 1)

    @pl.when(t == 0)
    def _():
        def chunk(ci, carry):
            rows = pl.ds(pl.multiple_of(ci * tk, tk), tk)
            kf = k_ref[0, rows, :].astype(F32)
            vf = v_ref[0, rows, :].astype(F32)
            c_all = ck_ref[0, rows, :]
            for hh in range(2):
                bias = _bias_lanes(-_head_column(c_all, heads[hh], klane), klane, HEAD_DIM + 3)
                ka_ref[hh, rows, :] = jnp.where(klane < HEAD_DIM, per_head(kf, hh), bias).astype(BF16)
                va_ref[hh, rows, :] = jnp.where(klane < HEAD_DIM, per_head(vf, hh), 1.0).astype(BF16)
            return carry

        lax.fori_loop(0, k_ref.shape[1] // tk, chunk, 0)

    qf = q_ref[0].astype(F32)
    cq_all = cq_ref[0]
    qs = [jnp.where(lane < HEAD_DIM, per_head(qf, hh),
                    _bias_lanes(_head_column(cq_all, heads[hh], lane), lane, HEAD_DIM)).astype(BF16)
          for hh in range(2)]

    def tile(j, state, masked, lo):
        off = pl.multiple_of(j * tk, tk)
        out = []
        for hh in range(2):
            s = lax.dot_general(qs[hh][lo:], ka_ref[hh, pl.ds(off, tk), :], _NT, preferred_element_type=F32)
            if masked:
                r = lax.broadcasted_iota(I32, s.shape, 0)
                c = lax.broadcasted_iota(I32, s.shape, 1)
                s = jnp.where(c <= r, s, NEG_BIG)
            m, acc = state[hh]
            m_new = jnp.maximum(m[lo:], jnp.max(s, axis=1, keepdims=True))
            p = jnp.exp2(s - m_new[:, :1]).astype(BF16)
            acc_new = jnp.exp2(m[lo:] - m_new) * acc[lo:] + jnp.dot(p, va_ref[hh, pl.ds(off, tk), :],
                                                                    preferred_element_type=F32)
            out.append((m_new, acc_new))
        return tuple(out)

    def finish(st):
        (_, a0), (_, a1) = st
        o0 = a0 / pltpu.roll(a0, HEAD_DIM, 1)
        o1 = a1 / pltpu.roll(a1, HEAD_DIM, 1)
        first = lax.broadcasted_iota(I32, a0.shape, 1) < HEAD_DIM
        return jnp.where(first, o0, pltpu.roll(o1, HEAD_DIM, 1)).astype(o_ref.dtype)

    init = (jnp.full((tq, LANES), NEG_BIG, F32), jnp.zeros((tq, LANES), F32))
    o_ref[0] = _finish_rows(*_causal_sweep(t, tq, tk, tile, (init, init)), tk, finish)


def _fox_attention(fq, fk, fv, cum):
    b, s, _ = fq.shape
    tq, tk = _attn_tiles(s)
    qspec = pl.BlockSpec((1, tq, LANES), lambda i, p, t: (i, t, p))
    kvspec = pl.BlockSpec((1, s, LANES), lambda i, p, t: (i, 0, p))
    return pl.pallas_call(
        functools.partial(_fox_body, tq=tq, tk=tk),
        grid=(b, FOX_HEADS // 2, s // tq),
        in_specs=[qspec, kvspec, kvspec,
                  pl.BlockSpec((1, tq, LANES), lambda i, p, t: (i, t, 0)),
                  pl.BlockSpec((1, s, LANES), lambda i, p, t: (i, 0, 0))],
        out_specs=qspec,
        out_shape=jax.ShapeDtypeStruct((b, s, D_FOX), BF16),
        scratch_shapes=[pltpu.VMEM((2, s, LANES), BF16), pltpu.VMEM((2, s, LANES), BF16)],
        compiler_params=_cparams(("arbitrary", "arbitrary", "arbitrary")),
        name="fox_attention",
    )(fq, fk, fv, cum, cum)


def _outproj_body(x_ref, a_ref, b_ref, w_ref, g1_ref, sc_ref, sh_ref, ng_ref, rw_ref, rb_ref,
                  x1_ref, h2_ref, lg_ref):
    mix = (jnp.dot(a_ref[0], w_ref[:D_DIFF, :], preferred_element_type=F32)
           + jnp.dot(b_ref[0], w_ref[D_DIFF:, :], preferred_element_type=F32))
    x1 = x_ref[0] + g1_ref[0] * mix
    x1_ref[0] = x1
    h2 = _rms(x1) * ng_ref[...] * (1.0 + sc_ref[0]) + sh_ref[0]
    h_hi = h2.astype(BF16)
    h2_ref[...] = h_hi
    h_lo = (h2 - h_hi.astype(F32)).astype(BF16)
    rw = rw_ref[...]
    w_hi = rw.astype(BF16)
    w_lo = (rw - w_hi.astype(F32)).astype(BF16)
    nt = lambda a, b: lax.dot_general(a, b, _NT, preferred_element_type=F32)
    lg_ref[...] = nt(w_hi, h_hi) + (nt(w_hi, h_lo) + nt(w_lo, h_hi)) + rb_ref[...]


def _outproj(x, a_out, b_out, w_out, g1, sc2, sh2, ng, rwt, rb):
    b, s, d = x.shape
    n = b * s
    tm = min(512, s)
    nt = s // tm
    row = lambda w: pl.BlockSpec((1, tm, w), lambda i, j: (i, j, 0))
    vec = pl.BlockSpec((1, 1, d), lambda i, j: (i, 0, 0))
    const = lambda shp: pl.BlockSpec(shp, lambda i, j: (0,) * len(shp))
    return pl.pallas_call(
        _outproj_body,
        grid=(b, nt),
        in_specs=[row(d), row(D_DIFF), row(D_FOX), const(w_out.shape), vec, vec, vec,
                  const((1, d)), const(rwt.shape), const(rb.shape)],
        out_specs=[row(d),
                   pl.BlockSpec((tm, d), lambda i, j: (i * nt + j, 0)),
                   pl.BlockSpec((N_EXPERTS, tm), lambda i, j: (0, i * nt + j))],
        out_shape=[jax.ShapeDtypeStruct((b, s, d), F32),
                   jax.ShapeDtypeStruct((n, d), BF16),
                   jax.ShapeDtypeStruct((N_EXPERTS, n), F32)],
        compiler_params=_cparams(("arbitrary", "arbitrary")),
        name="out_proj",
    )(x, a_out, b_out, w_out, g1, sc2, sh2, ng, rwt, rb)


def _strict_lower(n):
    r = lax.broadcasted_iota(I32, (n, n), 0)
    c = lax.broadcasted_iota(I32, (n, n), 1)
    return (c < r).astype(F32)


def _align_up(x_f32, m):
    shift = m.bit_length() - 1
    return (((x_f32.astype(I32) + (m - 1)) >> shift) << shift).astype(F32)


def _route_body(lg_ref, pos_ref, gate_ref, cnt_ref):
    i = pl.program_id(0)

    @pl.when(i == 0)
    def _():
        cnt_ref[...] = jnp.zeros_like(cnt_ref)

    work = lg_ref[...]
    td = work.shape[1]
    eidx = lax.broadcasted_iota(I32, work.shape, 0)
    vals, hots = [], []
    for k in range(TOP_K):
        m = jnp.max(work, axis=0, keepdims=True)
        sel = jnp.min(jnp.where(work == m, eidx, N_EXPERTS), axis=0, keepdims=True)
        hot = eidx == sel
        vals.append(m)
        hots.append(hot)
        work = jnp.where(hot, -jnp.inf, work)
    ex = [jnp.exp(v - vals[0]) for v in vals]
    den = ex[0] + ex[1] + ex[2] + ex[3]
    for k in range(TOP_K):
        gate_ref[k:k + 1, :] = ex[k] / den

    chosen = hots[0] | hots[1] | hots[2] | hots[3]
    r = lax.broadcasted_iota(I32, (td, td), 0)
    c = lax.broadcasted_iota(I32, (td, td), 1)
    before = (r < c).astype(BF16)
    earlier = jnp.dot(chosen.astype(BF16), before, preferred_element_type=F32)
    cnt = jnp.sum(chosen.astype(F32), axis=1, keepdims=True)
    run = _align_up(cnt, RUN_ALIGN)
    start = jnp.dot(_strict_lower(N_EXPERTS), jnp.broadcast_to(run, (N_EXPERTS, LANES)),
                    preferred_element_type=F32, precision=HIGHEST)[:, 0:1]
    slot = start + earlier
    for k in range(TOP_K):
        pos_ref[k:k + 1, :] = jnp.sum(jnp.where(hots[k], slot, 0.0), axis=0, keepdims=True).astype(I32)
    lane = lax.broadcasted_iota(I32, cnt_ref.shape, 1)
    cnt_ref[...] = jnp.where(lane == i, run, cnt_ref[...])


def _route(logits_t):
    e, n = logits_t.shape
    td = min(ROUTE_TILE, n)
    ntp = _round_up(n // td, LANES)
    tok = pl.BlockSpec((TOP_K, td), lambda i: (0, i))
    return pl.pallas_call(
        _route_body,
        grid=(n // td,),
        in_specs=[pl.BlockSpec((e, td), lambda i: (0, i))],
        out_specs=[tok, tok, pl.BlockSpec((e, ntp), lambda i: (0, 0))],
        out_shape=[jax.ShapeDtypeStruct((TOP_K, n), I32),
                   jax.ShapeDtypeStruct((TOP_K, n), F32),
                   jax.ShapeDtypeStruct((e, ntp), F32)],
        compiler_params=_cparams(("arbitrary",)),
        name="route_topk",
    )(logits_t)


def _layout_body(run_ref, base_ref, be_ref, nu_ref, tail_ref):
    run = run_ref[...]
    ntp = run.shape[1]
    total = jnp.sum(run, axis=1, keepdims=True)
    region = _align_up(total, EXPERT_TILE)
    pstart = jnp.dot(_strict_lower(N_EXPERTS), jnp.broadcast_to(region, (N_EXPERTS, LANES)),
                     preferred_element_type=F32, precision=HIGHEST)[:, 0:1]
    ti = lax.broadcasted_iota(I32, (ntp, ntp), 0)
    tj = lax.broadcasted_iota(I32, (ntp, ntp), 1)
    within = jnp.dot(run, (ti < tj).astype(F32), preferred_element_type=F32, precision=HIGHEST)
    base_ref[...] = (pstart + within).astype(I32)
    pend = pstart + region
    blk = (lax.broadcasted_iota(I32, (N_EXPERTS, be_ref.shape[1]), 1) * EXPERT_TILE).astype(F32)
    be = jnp.sum((pend <= blk).astype(I32), axis=0, keepdims=True)
    be_ref[...] = jnp.minimum(be, N_EXPERTS - 1)
    used = jnp.sum(region, axis=0, keepdims=True).astype(I32) >> (EXPERT_TILE.bit_length() - 1)
    nu_ref[...] = jnp.broadcast_to(used, nu_ref.shape)
    lane = lax.broadcasted_iota(I32, tail_ref.shape, 1)
    tail_ref[...] = jnp.where(lane == 0, pstart + total, jnp.where(lane == 1, region - total, 0.0)).astype(I32)


def _layout(run_t, n_blocks):
    e, ntp = run_t.shape
    nbp = _round_up(n_blocks, LANES)
    full = lambda shp: pl.BlockSpec(shp, lambda: (0,) * len(shp))
    return pl.pallas_call(
        _layout_body,
        in_specs=[full((e, ntp))],
        out_specs=[full((e, ntp)), full((1, nbp)), full((1, LANES)), full((e, LANES))],
        out_shape=[jax.ShapeDtypeStruct((e, ntp), I32),
                   jax.ShapeDtypeStruct((1, nbp), I32),
                   jax.ShapeDtypeStruct((1, LANES), I32),
                   jax.ShapeDtypeStruct((e, LANES), I32)],
        name="expert_layout",
    )(run_t)


_RUN_LEVELS = tuple(1 << b for b in range(ROUTE_TILE.bit_length() - 1, RUN_ALIGN.bit_length() - 2, -1))


def _for_each_run(run_tbl, base_tbl, tile, make_copy, start):
    def body(e, off):
        cnt = run_tbl[tile * N_EXPERTS + e]
        base = base_tbl[tile * N_EXPERTS + e]
        done = jnp.int32(0)
        for lvl in _RUN_LEVELS:
            bit = cnt & lvl

            @pl.when(bit != 0)
            def _(done=done, lvl=lvl):
                cp = make_copy(pl.multiple_of(off + done, RUN_ALIGN), pl.multiple_of(base + done, RUN_ALIGN), lvl)
                if start:
                    cp.start()
                else:
                    cp.wait()
            done = done + bit
        return off + cnt

    lax.fori_loop(0, N_EXPERTS, body, jnp.int32(0))


def _wait_for_tile(run_tbl, tile, make_copy, sb):
    total = lax.fori_loop(0, N_EXPERTS, lambda e, acc: acc + run_tbl[tile * N_EXPERTS + e], jnp.int32(0))
    lvl = 1 << (sb.bit_length() - 1)
    while lvl >= RUN_ALIGN:
        @pl.when((total & lvl) != 0)
        def _(lvl=lvl):
            make_copy(0, 0, lvl).wait()
        lvl //= 2


def _slot_onehot(pos, sb):
    j = lax.broadcasted_iota(I32, (sb, pos.shape[1]), 0)
    hits = [j == pos[k:k + 1, :] for k in range(TOP_K)]
    return hits, (hits[0] | hits[1] | hits[2] | hits[3])


def _sorted_rows(td):
    return _round_up(TOP_K * td + N_EXPERTS * (RUN_ALIGN - 1), LANES)


def _dispatch_body(run_tbl, base_tbl, tail_len, tail_start, h_ref, pos_ref, gate_ref, xbuf_hbm,
                   sorted_ref, zero_ref, sem, zsem, *, d):
    i = pl.program_id(0)
    last = pl.num_programs(0) - 1
    slot = i % 2
    sb = sorted_ref.shape[1]

    def zero_copies(off, base, rows):
        del off
        return pltpu.make_async_copy(zero_ref.at[pl.ds(0, rows)], xbuf_hbm.at[pl.ds(base, rows)], zsem)

    @pl.when(i == 0)
    def _():
        zero_ref[...] = jnp.zeros_like(zero_ref)
        _for_each_run(tail_len, tail_start, 0, zero_copies, True)

    hits, any_hit = _slot_onehot(pos_ref[...], sb)
    perm = jnp.where(any_hit, 1.0, 0.0).astype(BF16)
    sorted_ref[slot, :, :d] = jnp.dot(perm, h_ref[...], preferred_element_type=F32)
    gates = gate_ref[...]
    gsel = jnp.where(hits[0], gates[0:1, :], 0.0)
    for k in range(1, TOP_K):
        gsel = gsel + jnp.where(hits[k], gates[k:k + 1, :], 0.0)
    sorted_ref[slot, :, d:] = jnp.broadcast_to(jnp.sum(gsel, axis=1, keepdims=True), (sb, LANES))

    def copies(slot):
        return lambda off, base, rows: pltpu.make_async_copy(
            sorted_ref.at[slot, pl.ds(off, rows)], xbuf_hbm.at[pl.ds(base, rows)], sem.at[slot])

    _for_each_run(run_tbl, base_tbl, i, copies(slot), True)

    @pl.when(i > 0)
    def _():
        _wait_for_tile(run_tbl, i - 1, copies(1 - slot), sb)

    @pl.when(i == last)
    def _():
        _wait_for_tile(run_tbl, i, copies(slot), sb)
        _for_each_run(tail_len, tail_start, 0, zero_copies, False)


def _dispatch(run_tbl, base_tbl, tail_len, tail_start, h2, pos_t, gate_t, rows):
    n, d = h2.shape
    td = min(ROUTE_TILE, n)
    sb = _sorted_rows(td)
    dx = d + LANES
    assert EXPERT_TILE <= 2 * _RUN_LEVELS[0]
    tok = pl.BlockSpec((TOP_K, td), lambda i, *_: (0, i))
    grid_spec = pltpu.PrefetchScalarGridSpec(
        num_scalar_prefetch=4,
        grid=(n // td,),
        in_specs=[pl.BlockSpec((td, d), lambda i, *_: (i, 0)), tok, tok],
        out_specs=pl.BlockSpec(memory_space=pl.ANY),
        scratch_shapes=[pltpu.VMEM((2, sb, dx), F32), pltpu.VMEM((_RUN_LEVELS[0], dx), F32),
                        pltpu.SemaphoreType.DMA((2,)), pltpu.SemaphoreType.DMA],
    )
    return pl.pallas_call(
        functools.partial(_dispatch_body, d=d),
        grid_spec=grid_spec,
        out_shape=jax.ShapeDtypeStruct((rows, dx), F32),
        compiler_params=_cparams(("arbitrary",)),
        name="dispatch_rows",
    )(run_tbl, base_tbl, tail_len, tail_start, h2, pos_t, gate_t)


def _expert_body(be_ref, nu_ref, x_ref, w1_ref, b1_ref, w2_ref, b2_ref, y_ref, w1b_ref, w2b_ref):
    i = pl.program_id(0)
    d_ff, d = w2_ref.shape[1], w2_ref.shape[2]

    @pl.when((i == 0) | (be_ref[i] != be_ref[jnp.maximum(i - 1, 0)]))
    def _():
        w1b_ref[...] = w1_ref[0].astype(BF16)
        w2b_ref[...] = w2_ref[0].astype(BF16)

    @pl.when(i < nu_ref[0])
    def _():
        gu = jnp.dot(x_ref[:, :d].astype(BF16), w1b_ref[...], preferred_element_type=F32) + b1_ref[0]
        gate = jnp.minimum(gu[:, :d_ff], SWIGLU_LIMIT)
        up = jnp.clip(gu[:, d_ff:], -SWIGLU_LIMIT, SWIGLU_LIMIT)
        glu = gate * jax.nn.sigmoid(gate * SWIGLU_ALPHA)
        act = ((up + 1.0) * glu).astype(BF16)
        y = jnp.dot(act, w2b_ref[...], preferred_element_type=F32) + b2_ref[0]
        y_ref[...] = x_ref[:, d:d + 1] * y

    @pl.when(i >= nu_ref[0])
    def _():
        y_ref[...] = jnp.zeros_like(y_ref)


def _experts(layer, block_expert, n_used, xbuf, w1, b1, w2, b2):
    rows, dx = xbuf.shape
    de, d, f2 = w1.shape
    f = w2.shape[1]
    blk = lambda i, be, nu: (jnp.maximum(jnp.minimum(i, nu[0] - 1), 0), 0)
    wsel = lambda i, be, nu: (layer * N_EXPERTS + be[i], 0, 0)
    grid_spec = pltpu.PrefetchScalarGridSpec(
        num_scalar_prefetch=2,
        grid=(rows // EXPERT_TILE,),
        in_specs=[pl.BlockSpec((EXPERT_TILE, dx), blk),
                  pl.BlockSpec((1, d, f2), wsel),
                  pl.BlockSpec((1, 1, f2), wsel),
                  pl.BlockSpec((1, f, d), wsel),
                  pl.BlockSpec((1, 1, d), wsel)],
        out_specs=pl.BlockSpec((EXPERT_TILE, d), lambda i, be, nu: (i, 0)),
        scratch_shapes=[pltpu.VMEM((d, f2), BF16), pltpu.VMEM((f, d), BF16)],
    )
    return pl.pallas_call(
        _expert_body,
        grid_spec=grid_spec,
        out_shape=jax.ShapeDtypeStruct((rows, d), F32),
        compiler_params=_cparams(("arbitrary",)),
        name="expert_swiglu",
    )(block_expert, n_used, xbuf, w1, b1.reshape(de, 1, f2), w2, b2.reshape(de, 1, d))


def _combine_body(run_tbl, base_tbl, x1_ref, g2_ref, pos_ref, fg_ref, ybuf_hbm, o_ref, ys_ref, sem, *, final):
    i = pl.program_id(0)
    last = pl.num_programs(0) - 1
    slot = i % 2
    sb = ys_ref.shape[1]

    def copies(slot):
        return lambda off, base, rows: pltpu.make_async_copy(
            ybuf_hbm.at[pl.ds(base, rows)], ys_ref.at[slot, pl.ds(off, rows)], sem.at[slot])

    @pl.when(i == 0)
    def _():
        ys_ref[...] = jnp.zeros_like(ys_ref)
        _for_each_run(run_tbl, base_tbl, 0, copies(0), True)

    @pl.when(i < last)
    def _():
        _for_each_run(run_tbl, base_tbl, i + 1, copies(1 - slot), True)

    _wait_for_tile(run_tbl, i, copies(slot), sb)

    _, any_hit = _slot_onehot(pos_ref[...], sb)
    perm = jnp.where(any_hit, 1.0, 0.0).astype(BF16)
    y = ys_ref[slot]
    y_hi = y.astype(BF16)
    y_lo = (y - y_hi.astype(F32)).astype(BF16)
    moe = (lax.dot_general(perm, y_hi, _TN, preferred_element_type=F32)
           + lax.dot_general(perm, y_lo, _TN, preferred_element_type=F32))
    out = x1_ref[...] + g2_ref[0] * moe
    if final:
        out = _rms(out) * fg_ref[...]
    o_ref[...] = out


def _combine(run_tbl, base_tbl, x1, g2, pos_t, final_g, ybuf, seq, final):
    n, d = x1.shape
    td = min(ROUTE_TILE, n)
    sb = _sorted_rows(td)
    grid_spec = pltpu.PrefetchScalarGridSpec(
        num_scalar_prefetch=2,
        grid=(n // td,),
        in_specs=[pl.BlockSpec((td, d), lambda i, *_: (i, 0)),
                  pl.BlockSpec((1, 1, d), lambda i, *_: ((i * td) // seq, 0, 0)),
                  pl.BlockSpec((TOP_K, td), lambda i, *_: (0, i)),
                  pl.BlockSpec((1, d), lambda i, *_: (0, 0)),
                  pl.BlockSpec(memory_space=pl.ANY)],
        out_specs=pl.BlockSpec((td, d), lambda i, *_: (i, 0)),
        scratch_shapes=[pltpu.VMEM((2, sb, d), F32), pltpu.SemaphoreType.DMA((2,))],
    )
    return pl.pallas_call(
        functools.partial(_combine_body, final=final),
        grid_spec=grid_spec,
        out_shape=jax.ShapeDtypeStruct((n, d), F32),
        compiler_params=_cparams(("arbitrary",)),
        name="combine_rows",
    )(run_tbl, base_tbl, x1, g2, pos_t, final_g, ybuf)


def _moe(layer, x1, h2, logits_t, g2, w1, b1, w2, b2, final_g, seq, final):
    n, d = h2.shape
    td = min(ROUTE_TILE, n)
    n_tiles = n // td
    assert seq % td == 0
    rows = (_round_up(n * TOP_K + n_tiles * N_EXPERTS * (RUN_ALIGN - 1), EXPERT_TILE)
            + N_EXPERTS * EXPERT_TILE)
    n_blocks = rows // EXPERT_TILE
    pos_t, gate_t, run_t = _route(logits_t)
    base_t, be, nu, tail = _layout(run_t, n_blocks)
    run_tbl = run_t[:, :n_tiles].T.astype(I32).reshape(-1)
    base_tbl = base_t[:, :n_tiles].T.reshape(-1)
    xbuf = _dispatch(run_tbl, base_tbl, tail[:, 1], tail[:, 0], h2, pos_t, gate_t, rows)
    ybuf = _experts(layer, be[0, :n_blocks], nu[0, :1], xbuf, w1, b1, w2, b2)
    return _combine(run_tbl, base_tbl, x1, g2, pos_t, final_g, ybuf, seq, final)


def kernel(x, c, positions, ada_w, ada_b, norm_mix, norm_ffn, w_in, b_forget, lambda_q1, lambda_k1,
           lambda_q2, lambda_k2, diff_subln, w_out, router_w, router_b, exp_w1, exp_b1, exp_w2, exp_b2,
           final_norm):
    b, s, d = x.shape
    depth = ada_w.shape[0]
    mod = _ada_mod(c, ada_w, ada_b)
    cosf, sinf = _rope_tables(positions)
    n_in = w_in.shape[2]
    pad_cols = 3072 + LANES - n_in
    flat = lambda a: a.reshape((a.shape[0] * a.shape[1],) + a.shape[2:])
    w1_all, b1_all, w2_all, b2_all = flat(exp_w1), flat(exp_b1), flat(exp_w2), flat(exp_b2)
    for l in range(depth):
        lam_init = 0.8 - 0.6 * math.exp(-0.3 * l)
        sh1, sc1, g1, sh2, sc2, g2 = [mod[l, :, i * d:(i + 1) * d].reshape(b, 1, d) for i in range(6)]
        w_pad = jnp.pad(w_in[l], ((0, 0), (0, pad_cols))).astype(BF16)
        bf_pad = jnp.pad(b_forget[l], (0, LANES - FOX_HEADS)).reshape(1, LANES)
        dq, dk, dv, fq, fk, fv, flog = _inproj(x, sc1, sh1, norm_mix[l].reshape(1, d), w_pad, bf_pad,
                                               cosf, sinf)
        cum = _forget_cumsum(flog)
        a_out = _diff_attention(dq, dk, dv, lambda_q1[l].reshape(1, -1), lambda_k1[l].reshape(1, -1),
                                lambda_q2[l].reshape(1, -1), lambda_k2[l].reshape(1, -1),
                                diff_subln[l].reshape(1, -1), lam_init)
        b_out = _fox_attention(fq, fk, fv, cum)
        x1, h2, logits_t = _outproj(x, a_out, b_out, w_out[l].astype(BF16), g1, sc2, sh2,
                                    norm_ffn[l].reshape(1, d), router_w[l].T,
                                    router_b[l].reshape(-1, 1))
        x = _moe(l, x1.reshape(b * s, d), h2, logits_t, g2, w1_all, b1_all, w2_all, b2_all,
                 final_norm.reshape(1, d), s, l == depth - 1).reshape(b, s, d)
    return x
```

```python
import functools
import math

import jax
import jax.numpy as jnp
from jax import lax
from jax.experimental import pallas as pl
from jax.experimental.pallas import tpu as pltpu

F32 = jnp.float32
BF16 = jnp.bfloat16
I32 = jnp.int32
HIGHEST = lax.Precision.HIGHEST

HEAD_DIM = 64
LANES = 128
SUBLANES = 8
CHUNK = 64
DIFF_HEADS = 4
FOX_HEADS = 8
D_DIFF = DIFF_HEADS * 2 * HEAD_DIM
D_FOX = FOX_HEADS * HEAD_DIM
ROPE_THETA = 500000.0
ROPE_DIM = HEAD_DIM // 4
ROPE_HALF = ROPE_DIM // 2
N_EXPERTS = 32
TOP_K = 4
SWIGLU_LIMIT = 7.0
SWIGLU_ALPHA = 1.702
NORM_EPS = 1e-6
NEG_BIG = -1e30
LOG2E = math.log2(math.e)
ATTN_Q_TILE = 1024
ATTN_K_TILE = 512
EXPERT_TILE = 512
ROUTE_TILE = 256
RUN_ALIGN = SUBLANES
VMEM_LIMIT = 56 * 1024 * 1024

_NT = (((1,), (1,)), ((), ()))
_TN = (((0,), (0,)), ((), ()))


def _cparams(sem, vmem=None):
    return pltpu.CompilerParams(dimension_semantics=sem, vmem_limit_bytes=vmem or VMEM_LIMIT)


def _rms(x):
    return x * lax.rsqrt(jnp.mean(x * x, axis=-1, keepdims=True) + NORM_EPS)


def _round_up(x, m):
    return (x + m - 1) // m * m


def _attn_tiles(s):
    tq = min(ATTN_Q_TILE, s)
    return tq, min(ATTN_K_TILE, tq)


def _ada_body(c_ref, w_ref, b_ref, o_ref):
    c = c_ref[...]
    ca = c * jax.nn.sigmoid(c)
    o_ref[0] = jnp.dot(ca, w_ref[0], preferred_element_type=F32, precision=HIGHEST) + b_ref[0]


def _ada_mod(c, ada_w, ada_b):
    depth, d, n6 = ada_w.shape
    b = c.shape[0]
    tn = 1536
    return pl.pallas_call(
        _ada_body,
        grid=(depth, n6 // tn),
        in_specs=[pl.BlockSpec((b, d), lambda l, j: (0, 0)),
                  pl.BlockSpec((1, d, tn), lambda l, j: (l, 0, j)),
                  pl.BlockSpec((1, 1, tn), lambda l, j: (l, 0, j))],
        out_specs=pl.BlockSpec((1, b, tn), lambda l, j: (l, 0, j)),
        out_shape=jax.ShapeDtypeStruct((depth, b, n6), F32),
        compiler_params=_cparams(("arbitrary", "arbitrary")),
        name="ada_mod",
    )(c, ada_w, ada_b.reshape(depth, 1, n6))


def _rope_body(pos_ref, invf_ref, cos_ref, sin_ref):
    ang = pos_ref[0].astype(F32) * invf_ref[...]
    j = lax.broadcasted_iota(I32, ang.shape, 1) & (HEAD_DIM - 1)
    c = jnp.cos(ang)
    s = jnp.sin(ang)
    cos_ref[0] = jnp.where(j < ROPE_DIM, c, 1.0)
    sin_ref[0] = jnp.where(j < ROPE_HALF, -s, jnp.where(j < ROPE_DIM, s, 0.0))


def _rope_tables(positions):
    b, s = positions.shape
    ts = min(512, s)
    inv_freq = ROPE_THETA ** (-jnp.arange(0, ROPE_DIM, 2, dtype=F32) / ROPE_DIM)
    invf = inv_freq[jnp.arange(LANES) % ROPE_HALF].reshape(1, LANES)
    out = jax.ShapeDtypeStruct((b, s, LANES), F32)
    return pl.pallas_call(
        _rope_body,
        grid=(b, s // ts),
        in_specs=[pl.BlockSpec((1, ts, 1), lambda i, j: (i, j, 0)),
                  pl.BlockSpec((1, LANES), lambda i, j: (0, 0))],
        out_specs=[pl.BlockSpec((1, ts, LANES), lambda i, j: (i, j, 0))] * 2,
        out_shape=[out, out],
        compiler_params=_cparams(("arbitrary", "arbitrary")),
        name="rope_tables",
    )(positions.reshape(b, s, 1), invf)


def _inproj_body(x_ref, sc_ref, sh_ref, g_ref, w_ref, bf_ref, cos_ref, sin_ref,
                 dq_ref, dk_ref, dv_ref, fq_ref, fk_ref, fv_ref, fl_ref):
    h = _rms(x_ref[0]) * g_ref[...] * (1.0 + sc_ref[0]) + sh_ref[0]
    hb = h.astype(BF16)
    cosf = cos_ref[0]
    sinf = sin_ref[0]
    lane = lax.broadcasted_iota(I32, cosf.shape, 1)
    first = (lane & (HEAD_DIM - 1)) < ROPE_HALF

    def rope(p):
        nxt = pltpu.roll(p, LANES - ROPE_HALF, 1)
        prv = pltpu.roll(p, ROPE_HALF, 1)
        return p * cosf + jnp.where(first, nxt, prv) * sinf

    def proj(c0):
        return jnp.dot(hb, w_ref[:, c0:c0 + 512], preferred_element_type=F32)

    scale = HEAD_DIM ** -0.5 * LOG2E
    p = proj(0)
    for c in range(4):
        sl = slice(c * LANES, (c + 1) * LANES)
        dq_ref[0, :, sl] = (rope(p[:, sl]) * scale).astype(BF16)
    p = proj(512)
    for c in range(4):
        sl = slice(c * LANES, (c + 1) * LANES)
        dk_ref[0, :, sl] = rope(p[:, sl]).astype(BF16)
    dv_ref[0] = proj(1024).astype(BF16)
    fq_ref[0] = (proj(1536) * scale).astype(BF16)
    fk_ref[0] = proj(2048).astype(BF16)
    fv_ref[0] = proj(2560).astype(BF16)
    z = jnp.dot(hb, w_ref[:, 3072:3072 + LANES], preferred_element_type=F32) + bf_ref[...]
    fl_ref[0] = -(jnp.maximum(-z, 0.0) + jnp.log1p(jnp.exp(-jnp.abs(z))))


def _inproj(x, sc, sh, g, w_pad, bf_pad, cosf, sinf):
    b, s, d = x.shape
    tm = min(512, s)
    act = lambda w: jax.ShapeDtypeStruct((b, s, w), BF16)
    row = lambda w: pl.BlockSpec((1, tm, w), lambda i, j: (i, j, 0))
    vec = pl.BlockSpec((1, 1, d), lambda i, j: (i, 0, 0))
    return pl.pallas_call(
        _inproj_body,
        grid=(b, s // tm),
        in_specs=[row(d), vec, vec,
                  pl.BlockSpec((1, d), lambda i, j: (0, 0)),
                  pl.BlockSpec(w_pad.shape, lambda i, j: (0, 0)),
                  pl.BlockSpec((1, LANES), lambda i, j: (0, 0)),
                  row(LANES), row(LANES)],
        out_specs=[row(512)] * 6 + [row(LANES)],
        out_shape=[act(512)] * 6 + [jax.ShapeDtypeStruct((b, s, LANES), F32)],
        compiler_params=_cparams(("arbitrary", "arbitrary")),
        name="in_proj",
    )(x, sc, sh, g, w_pad, bf_pad, cosf, sinf)


def _cum_body(fl_ref, cum_ref):
    x = fl_ref[0]
    s = x.shape[0]
    row = lax.broadcasted_iota(I32, x.shape, 0)
    d = 1
    while d < s:
        x = x + jnp.where(row >= d, pltpu.roll(x, d, 0), 0.0)
        d *= 2
    cum_ref[0] = x * LOG2E


def _forget_cumsum(flog):
    b, s, _ = flog.shape
    spec = pl.BlockSpec((1, s, LANES), lambda i: (i, 0, 0))
    return pl.pallas_call(
        _cum_body,
        grid=(b,),
        in_specs=[spec],
        out_specs=spec,
        out_shape=jax.ShapeDtypeStruct((b, s, LANES), F32),
        compiler_params=_cparams(("arbitrary",)),
        name="forget_cumsum",
    )(flog)


def _causal_sweep(t, tq, tk, tile, state):
    ratio = tq // tk
    assert ratio in (1, 2) and ratio * tk == tq
    n_full = t * ratio

    def steps(k, base, trips, st):
        def body(i, st):
            for u in range(k):
                st = tile(base + k * i + u, st, False, 0)
            return st
        return lax.fori_loop(0, trips, body, st)

    state = steps(4, 0, n_full // 4, state)
    state = steps(2, (n_full // 4) * 4, (n_full // 2) % 2, state)
    if ratio == 1:
        state = steps(1, (n_full // 2) * 2, n_full % 2, state)
    state = tile(n_full, state, True, 0)
    return state, (tile(n_full + 1, state, True, tk) if ratio == 2 else None)


def _finish_rows(state, lower, tk, finish):
    if lower is None:
        return finish(state)
    top = finish(jax.tree.map(lambda a: a[:tk], state))
    return jnp.concatenate([top, finish(lower)], axis=0)


def _diff_body(q_ref, k_ref, v_ref, lq1_ref, lk1_ref, lq2_ref, lk2_ref, g_ref, o_ref, *, tq, tk, lam_init):
    t = pl.program_id(2)
    q = q_ref[0]
    lane = lax.broadcasted_iota(I32, q.shape, 1)
    zero = jnp.zeros_like(q)
    qs = (jnp.where(lane < HEAD_DIM, q, zero), jnp.where(lane >= HEAD_DIM, q, zero))

    def tile(j, state, masked, lo):
        off = pl.multiple_of(j * tk, tk)
        kt = k_ref[0, pl.ds(off, tk), :]
        vt = v_ref[0, pl.ds(off, tk), :]
        out = []
        for mp in range(2):
            s = lax.dot_general(qs[mp][lo:], kt, _NT, preferred_element_type=F32)
            if masked:
                r = lax.broadcasted_iota(I32, s.shape, 0)
                c = lax.broadcasted_iota(I32, s.shape, 1)
                shift = CHUNK.bit_length() - 1
                s = jnp.where((c >> shift) <= (r >> shift), s, NEG_BIG)
            m, l, acc = state[mp]
            m_new = jnp.maximum(m[lo:], jnp.max(s, axis=1, keepdims=True))
            alpha = jnp.exp2(m[lo:] - m_new)
            p = jnp.exp2(s - m_new[:, :1])
            l_new = alpha * l[lo:] + jnp.sum(p, axis=1, keepdims=True)
            acc_new = alpha * acc[lo:] + jnp.dot(p.astype(BF16), vt, preferred_element_type=F32)
            out.append((m_new, l_new, acc_new))
        return tuple(out)

    lam = (jnp.exp(jnp.sum(lq1_ref[...] * lk1_ref[...], axis=1, keepdims=True))
           - jnp.exp(jnp.sum(lq2_ref[...] * lk2_ref[...], axis=1, keepdims=True)) + lam_init)

    def finish(st):
        (_, l0, a0), (_, l1, a1) = st
        out = a0 / l0 - lam * (a1 / l1)
        return (_rms(out) * g_ref[...] * (1.0 - lam_init)).astype(o_ref.dtype)

    init = (jnp.full((tq, LANES), NEG_BIG, F32), jnp.zeros((tq, LANES), F32), jnp.zeros((tq, LANES), F32))
    o_ref[0] = _finish_rows(*_causal_sweep(t, tq, tk, tile, (init, init)), tk, finish)


def _diff_attention(dq, dk, dv, lq1, lk1, lq2, lk2, subln, lam_init):
    b, s, _ = dq.shape
    tq, tk = _attn_tiles(s)
    qspec = pl.BlockSpec((1, tq, LANES), lambda i, h, t: (i, t, h))
    kvspec = pl.BlockSpec((1, s, LANES), lambda i, h, t: (i, 0, h))
    lspec = pl.BlockSpec((1, HEAD_DIM), lambda i, h, t: (0, 0))
    return pl.pallas_call(
        functools.partial(_diff_body, tq=tq, tk=tk, lam_init=lam_init),
        grid=(b, DIFF_HEADS, s // tq),
        in_specs=[qspec, kvspec, kvspec, lspec, lspec, lspec, lspec,
                  pl.BlockSpec((1, LANES), lambda i, h, t: (0, 0))],
        out_specs=qspec,
        out_shape=jax.ShapeDtypeStruct((b, s, D_DIFF), BF16),
        compiler_params=_cparams(("arbitrary", "arbitrary", "arbitrary")),
        name="diff_attention",
    )(dq, dk, dv, lq1, lk1, lq2, lk2, subln)


def _split3(c):
    hi = c.astype(BF16).astype(F32)
    r = c - hi
    mid = r.astype(BF16).astype(F32)
    return hi, mid, r - mid


def _bias_lanes(c, lane, first):
    hi, mid, lo = _split3(c)
    ones = jnp.where((lane >= HEAD_DIM) & (lane < HEAD_DIM + 6), 1.0, 0.0)
    return jnp.where(lane == first, hi, jnp.where(lane == first + 1, mid, jnp.where(lane == first + 2, lo, ones)))


def _head_column(c_all, h, lane):
    return jnp.broadcast_to(jnp.sum(jnp.where(lane == h, c_all, 0.0), axis=1, keepdims=True), c_all.shape)


def _fox_body(q_ref, k_ref, v_ref, cq_ref, ck_ref, o_ref, ka_ref, va_ref, *, tq, tk):
    pair = pl.program_id(1)
    t = pl.program_id(2)
    heads = (2 * pair, 2 * pair + 1)
    lane = lax.broadcasted_iota(I32, (tq, LANES), 1)
    klane = lax.broadcasted_iota(I32, (tk, LANES), 1)

    def per_head(x_f32, hh):
        return x_f32 if hh == 0 else pltpu.roll(x_f32, HEAD_DIM, 1)

    @pl.when(t == 0)
    def _():
        def chunk(ci, carry):
            rows = pl.ds(pl.multiple_of(ci * tk, tk), tk)
            kf = k_ref[0, rows, :].astype(F32)
            vf = v_ref[0, rows, :].astype(F32)
            c_all = ck_ref[0, rows, :]
            for hh in range(2):
                bias = _bias_lanes(-_head_column(c_all, heads[hh], klane), klane, HEAD_DIM + 3)
                ka_ref[hh, rows, :] = jnp.where(klane < HEAD_DIM, per_head(kf, hh), bias).astype(BF16)
                va_ref[hh, rows, :] = jnp.where(klane < HEAD_DIM, per_head(vf, hh), 1.0).astype(BF16)
            return carry

        lax.fori_loop(0, k_ref.shape[1] // tk, chunk, 0)

    qf = q_ref[0].astype(F32)
    cq_all = cq_ref[0]
    qs = [jnp.where(lane < HEAD_DIM, per_head(qf, hh),
                    _bias_lanes(_head_column(cq_all, heads[hh], lane), lane, HEAD_DIM)).astype(BF16)
          for hh in range(2)]

    def tile(j, state, masked, lo):
        off = pl.multiple_of(j * tk, tk)
        out = []
        for hh in range(2):
            s = lax.dot_general(qs[hh][lo:], ka_ref[hh, pl.ds(off, tk), :], _NT, preferred_element_type=F32)
            if masked:
                r = lax.broadcasted_iota(I32, s.shape, 0)
                c = lax.broadcasted_iota(I32, s.shape, 1)
                s = jnp.where(c <= r, s, NEG_BIG)
            m, acc = state[hh]
            m_new = jnp.maximum(m[lo:], jnp.max(s, axis=1, keepdims=True))
            p = jnp.exp2(s - m_new[:, :1]).astype(BF16)
            acc_new = jnp.exp2(m[lo:] - m_new) * acc[lo:] + jnp.dot(p, va_ref[hh, pl.ds(off, tk), :],
                                                                    preferred_element_type=F32)
            out.append((m_new, acc_new))
        return tuple(out)

    def finish(st):
        (_, a0), (_, a1) = st
        o0 = a0 / pltpu.roll(a0, HEAD_DIM, 1)
        o1 = a1 / pltpu.roll(a1, HEAD_DIM, 1)
        first = lax.broadcasted_iota(I32, a0.shape, 1) < HEAD_DIM
        return jnp.where(first, o0, pltpu.roll(o1, HEAD_DIM, 1)).astype(o_ref.dtype)

    init = (jnp.full((tq, LANES), NEG_BIG, F32), jnp.zeros((tq, LANES), F32))
    o_ref[0] = _finish_rows(*_causal_sweep(t, tq, tk, tile, (init, init)), tk, finish)


def _fox_attention(fq, fk, fv, cum):
    b, s, _ = fq.shape
    tq, tk = _attn_tiles(s)
    qspec = pl.BlockSpec((1, tq, LANES), lambda i, p, t: (i, t, p))
    kvspec = pl.BlockSpec((1, s, LANES), lambda i, p, t: (i, 0, p))
    return pl.pallas_call(
        functools.partial(_fox_body, tq=tq, tk=tk),
        grid=(b, FOX_HEADS // 2, s // tq),
        in_specs=[qspec, kvspec, kvspec,
                  pl.BlockSpec((1, tq, LANES), lambda i, p, t: (i, t, 0)),
                  pl.BlockSpec((1, s, LANES), lambda i, p, t: (i, 0, 0))],
        out_specs=qspec,
        out_shape=jax.ShapeDtypeStruct((b, s, D_FOX), BF16),
        scratch_shapes=[pltpu.VMEM((2, s, LANES), BF16), pltpu.VMEM((2, s, LANES), BF16)],
        compiler_params=_cparams(("arbitrary", "arbitrary", "arbitrary")),
        name="fox_attention",
    )(fq, fk, fv, cum, cum)


def _outproj_body(x_ref, a_ref, b_ref, w_ref, g1_ref, sc_ref, sh_ref, ng_ref, rw_ref, rb_ref,
                  x1_ref, h2_ref, lg_ref):
    mix = (jnp.dot(a_ref[0], w_ref[:D_DIFF, :], preferred_element_type=F32)
           + jnp.dot(b_ref[0], w_ref[D_DIFF:, :], preferred_element_type=F32))
    x1 = x_ref[0] + g1_ref[0] * mix
    x1_ref[0] = x1
    h2 = _rms(x1) * ng_ref[...] * (1.0 + sc_ref[0]) + sh_ref[0]
    h_hi = h2.astype(BF16)
    h2_ref[...] = h_hi
    h_lo = (h2 - h_hi.astype(F32)).astype(BF16)
    rw = rw_ref[...]
    w_hi = rw.astype(BF16)
    w_lo = (rw - w_hi.astype(F32)).astype(BF16)
    nt = lambda a, b: lax.dot_general(a, b, _NT, preferred_element_type=F32)
    lg_ref[...] = nt(w_hi, h_hi) + (nt(w_hi, h_lo) + nt(w_lo, h_hi)) + rb_ref[...]


def _outproj(x, a_out, b_out, w_out, g1, sc2, sh2, ng, rwt, rb):
    b, s, d = x.shape
    n = b * s
    tm = min(512, s)
    nt = s // tm
    row = lambda w: pl.BlockSpec((1, tm, w), lambda i, j: (i, j, 0))
    vec = pl.BlockSpec((1, 1, d), lambda i, j: (i, 0, 0))
    const = lambda shp: pl.BlockSpec(shp, lambda i, j: (0,) * len(shp))
    return pl.pallas_call(
        _outproj_body,
        grid=(b, nt),
        in_specs=[row(d), row(D_DIFF), row(D_FOX), const(w_out.shape), vec, vec, vec,
                  const((1, d)), const(rwt.shape), const(rb.shape)],
        out_specs=[row(d),
                   pl.BlockSpec((tm, d), lambda i, j: (i * nt + j, 0)),
                   pl.BlockSpec((N_EXPERTS, tm), lambda i, j: (0, i * nt + j))],
        out_shape=[jax.ShapeDtypeStruct((b, s, d), F32),
                   jax.ShapeDtypeStruct((n, d), BF16),
                   jax.ShapeDtypeStruct((N_EXPERTS, n), F32)],
        compiler_params=_cparams(("arbitrary", "arbitrary")),
        name="out_proj",
    )(x, a_out, b_out, w_out, g1, sc2, sh2, ng, rwt, rb)


def _strict_lower(n):
    r = lax.broadcasted_iota(I32, (n, n), 0)
    c = lax.broadcasted_iota(I32, (n, n), 1)
    return (c < r).astype(F32)


def _align_up(x_f32, m):
    shift = m.bit_length() - 1
    return (((x_f32.astype(I32) + (m - 1)) >> shift) << shift).astype(F32)


def _route_body(lg_ref, pos_ref, gate_ref, cnt_ref):
    i = pl.program_id(0)

    @pl.when(i == 0)
    def _():
        cnt_ref[...] = jnp.zeros_like(cnt_ref)

    work = lg_ref[...]
    td = work.shape[1]
    eidx = lax.broadcasted_iota(I32, work.shape, 0)
    vals, hots = [], []
    for k in range(TOP_K):
        m = jnp.max(work, axis=0, keepdims=True)
        sel = jnp.min(jnp.where(work == m, eidx, N_EXPERTS), axis=0, keepdims=True)
        hot = eidx == sel
        vals.append(m)
        hots.append(hot)
        work = jnp.where(hot, -jnp.inf, work)
    ex = [jnp.exp(v - vals[0]) for v in vals]
    den = ex[0] + ex[1] + ex[2] + ex[3]
    for k in range(TOP_K):
        gate_ref[k:k + 1, :] = ex[k] / den

    chosen = hots[0] | hots[1] | hots[2] | hots[3]
    r = lax.broadcasted_iota(I32, (td, td), 0)
    c = lax.broadcasted_iota(I32, (td, td), 1)
    before = (r < c).astype(BF16)
    earlier = jnp.dot(chosen.astype(BF16), before, preferred_element_type=F32)
    cnt = jnp.sum(chosen.astype(F32), axis=1, keepdims=True)
    run = _align_up(cnt, RUN_ALIGN)
    start = jnp.dot(_strict_lower(N_EXPERTS), jnp.broadcast_to(run, (N_EXPERTS, LANES)),
                    preferred_element_type=F32, precision=HIGHEST)[:, 0:1]
    slot = start + earlier
    for k in range(TOP_K):
        pos_ref[k:k + 1, :] = jnp.sum(jnp.where(hots[k], slot, 0.0), axis=0, keepdims=True).astype(I32)
    lane = lax.broadcasted_iota(I32, cnt_ref.shape, 1)
    cnt_ref[...] = jnp.where(lane == i, run, cnt_ref[...])


def _route(logits_t):
    e, n = logits_t.shape
    td = min(ROUTE_TILE, n)
    ntp = _round_up(n // td, LANES)
    tok = pl.BlockSpec((TOP_K, td), lambda i: (0, i))
    return pl.pallas_call(
        _route_body,
        grid=(n // td,),
        in_specs=[pl.BlockSpec((e, td), lambda i: (0, i))],
        out_specs=[tok, tok, pl.BlockSpec((e, ntp), lambda i: (0, 0))],
        out_shape=[jax.ShapeDtypeStruct((TOP_K, n), I32),
                   jax.ShapeDtypeStruct((TOP_K, n), F32),
                   jax.ShapeDtypeStruct((e, ntp), F32)],
        compiler_params=_cparams(("arbitrary",)),
        name="route_topk",
    )(logits_t)


def _layout_body(run_ref, base_ref, be_ref, nu_ref, tail_ref):
    run = run_ref[...]
    ntp = run.shape[1]
    total = jnp.sum(run, axis=1, keepdims=True)
    region = _align_up(total, EXPERT_TILE)
    pstart = jnp.dot(_strict_lower(N_EXPERTS), jnp.broadcast_to(region, (N_EXPERTS, LANES)),
                     preferred_element_type=F32, precision=HIGHEST)[:, 0:1]
    ti = lax.broadcasted_iota(I32, (ntp, ntp), 0)
    tj = lax.broadcasted_iota(I32, (ntp, ntp), 1)
    within = jnp.dot(run, (ti < tj).astype(F32), preferred_element_type=F32, precision=HIGHEST)
    base_ref[...] = (pstart + within).astype(I32)
    pend = pstart + region
    blk = (lax.broadcasted_iota(I32, (N_EXPERTS, be_ref.shape[1]), 1) * EXPERT_TILE).astype(F32)
    be = jnp.sum((pend <= blk).astype(I32), axis=0, keepdims=True)
    be_ref[...] = jnp.minimum(be, N_EXPERTS - 1)
    used = jnp.sum(region, axis=0, keepdims=True).astype(I32) >> (EXPERT_TILE.bit_length() - 1)
    nu_ref[...] = jnp.broadcast_to(used, nu_ref.shape)
    lane = lax.broadcasted_iota(I32, tail_ref.shape, 1)
    tail_ref[...] = jnp.where(lane == 0, pstart + total, jnp.where(lane == 1, region - total, 0.0)).astype(I32)


def _layout(run_t, n_blocks):
    e, ntp = run_t.shape
    nbp = _round_up(n_blocks, LANES)
    full = lambda shp: pl.BlockSpec(shp, lambda: (0,) * len(shp))
    return pl.pallas_call(
        _layout_body,
        in_specs=[full((e, ntp))],
        out_specs=[full((e, ntp)), full((1, nbp)), full((1, LANES)), full((e, LANES))],
        out_shape=[jax.ShapeDtypeStruct((e, ntp), I32),
                   jax.ShapeDtypeStruct((1, nbp), I32),
                   jax.ShapeDtypeStruct((1, LANES), I32),
                   jax.ShapeDtypeStruct((e, LANES), I32)],
        name="expert_layout",
    )(run_t)


_RUN_LEVELS = tuple(1 << b for b in range(ROUTE_TILE.bit_length() - 1, RUN_ALIGN.bit_length() - 2, -1))


def _for_each_run(run_tbl, base_tbl, tile, make_copy, start):
    def body(e, off):
        cnt = run_tbl[tile * N_EXPERTS + e]
        base = base_tbl[tile * N_EXPERTS + e]
        done = jnp.int32(0)
        for lvl in _RUN_LEVELS:
            bit = cnt & lvl

            @pl.when(bit != 0)
            def _(done=done, lvl=lvl):
                cp = make_copy(pl.multiple_of(off + done, RUN_ALIGN), pl.multiple_of(base + done, RUN_ALIGN), lvl)
                if start:
                    cp.start()
                else:
                    cp.wait()
            done = done + bit
        return off + cnt

    lax.fori_loop(0, N_EXPERTS, body, jnp.int32(0))


def _wait_for_tile(run_tbl, tile, make_copy, sb):
    total = lax.fori_loop(0, N_EXPERTS, lambda e, acc: acc + run_tbl[tile * N_EXPERTS + e], jnp.int32(0))
    lvl = 1 << (sb.bit_length() - 1)
    while lvl >= RUN_ALIGN:
        @pl.when((total & lvl) != 0)
        def _(lvl=lvl):
            make_copy(0, 0, lvl).wait()
        lvl //= 2


def _slot_onehot(pos, sb):
    j = lax.broadcasted_iota(I32, (sb, pos.shape[1]), 0)
    hits = [j == pos[k:k + 1, :] for k in range(TOP_K)]
    return hits, (hits[0] | hits[1] | hits[2] | hits[3])


def _sorted_rows(td):
    return _round_up(TOP_K * td + N_EXPERTS * (RUN_ALIGN - 1), LANES)


def _dispatch_body(run_tbl, base_tbl, tail_len, tail_start, h_ref, pos_ref, gate_ref, xbuf_hbm,
                   sorted_ref, zero_ref, sem, zsem, *, d):
    i = pl.program_id(0)
    last = pl.num_programs(0) - 1
    slot = i % 2
    sb = sorted_ref.shape[1]

    def zero_copies(off, base, rows):
        del off
        return pltpu.make_async_copy(zero_ref.at[pl.ds(0, rows)], xbuf_hbm.at[pl.ds(base, rows)], zsem)

    @pl.when(i == 0)
    def _():
        zero_ref[...] = jnp.zeros_like(zero_ref)
        _for_each_run(tail_len, tail_start, 0, zero_copies, True)

    hits, any_hit = _slot_onehot(pos_ref[...], sb)
    perm = jnp.where(any_hit, 1.0, 0.0).astype(BF16)
    sorted_ref[slot, :, :d] = jnp.dot(perm, h_ref[...], preferred_element_type=F32)
    gates = gate_ref[...]
    gsel = jnp.where(hits[0], gates[0:1, :], 0.0)
    for k in range(1, TOP_K):
        gsel = gsel + jnp.where(hits[k], gates[k:k + 1, :], 0.0)
    sorted_ref[slot, :, d:] = jnp.broadcast_to(jnp.sum(gsel, axis=1, keepdims=True), (sb, LANES))

    def copies(slot):
        return lambda off, base, rows: pltpu.make_async_copy(
            sorted_ref.at[slot, pl.ds(off, rows)], xbuf_hbm.at[pl.ds(base, rows)], sem.at[slot])

    _for_each_run(run_tbl, base_tbl, i, copies(slot), True)

    @pl.when(i > 0)
    def _():
        _wait_for_tile(run_tbl, i - 1, copies(1 - slot), sb)

    @pl.when(i == last)
    def _():
        _wait_for_tile(run_tbl, i, copies(slot), sb)
        _for_each_run(tail_len, tail_start, 0, zero_copies, False)


def _dispatch(run_tbl, base_tbl, tail_len, tail_start, h2, pos_t, gate_t, rows):
    n, d = h2.shape
    td = min(ROUTE_TILE, n)
    sb = _sorted_rows(td)
    dx = d + LANES
    assert EXPERT_TILE <= 2 * _RUN_LEVELS[0]
    tok = pl.BlockSpec((TOP_K, td), lambda i, *_: (0, i))
    grid_spec = pltpu.PrefetchScalarGridSpec(
        num_scalar_prefetch=4,
        grid=(n // td,),
        in_specs=[pl.BlockSpec((td, d), lambda i, *_: (i, 0)), tok, tok],
        out_specs=pl.BlockSpec(memory_space=pl.ANY),
        scratch_shapes=[pltpu.VMEM((2, sb, dx), F32), pltpu.VMEM((_RUN_LEVELS[0], dx), F32),
                        pltpu.SemaphoreType.DMA((2,)), pltpu.SemaphoreType.DMA],
    )
    return pl.pallas_call(
        functools.partial(_dispatch_body, d=d),
        grid_spec=grid_spec,
        out_shape=jax.ShapeDtypeStruct((rows, dx), F32),
        compiler_params=_cparams(("arbitrary",)),
        name="dispatch_rows",
    )(run_tbl, base_tbl, tail_len, tail_start, h2, pos_t, gate_t)


def _expert_body(be_ref, nu_ref, x_ref, w1_hbm, b1_ref, w2_hbm, b2_ref, y_ref,
                 w1f_ref, w2f_ref, w1b_ref, w2b_ref, sem, *, layer):
    i = pl.program_id(0)
    n = pl.num_programs(0)
    d_ff, d = w2b_ref.shape

    def fetch(blk):
        e = layer * N_EXPERTS + be_ref[blk]
        return (pltpu.make_async_copy(w1_hbm.at[e], w1f_ref, sem.at[0]),
                pltpu.make_async_copy(w2_hbm.at[e], w2f_ref, sem.at[1]))

    @pl.when(i == 0)
    def _():
        for cp in fetch(0):
            cp.start()

    @pl.when((i == 0) | (be_ref[i] != be_ref[jnp.maximum(i - 1, 0)]))
    def _():
        for cp in fetch(i):
            cp.wait()
        w1b_ref[...] = w1f_ref[...].astype(BF16)
        w2b_ref[...] = w2f_ref[...].astype(BF16)
        nxt = lax.while_loop(lambda j: (j < n) & (be_ref[jnp.minimum(j, n - 1)] == be_ref[i]),
                             lambda j: j + 1, i + 1)

        @pl.when(nxt < n)
        def _():
            for cp in fetch(nxt):
                cp.start()

    @pl.when(i < nu_ref[0])
    def _():
        gu = jnp.dot(x_ref[:, :d].astype(BF16), w1b_ref[...], preferred_element_type=F32) + b1_ref[0]
        gate = jnp.minimum(gu[:, :d_ff], SWIGLU_LIMIT)
        up = jnp.clip(gu[:, d_ff:], -SWIGLU_LIMIT, SWIGLU_LIMIT)
        glu = gate * jax.nn.sigmoid(gate * SWIGLU_ALPHA)
        act = ((up + 1.0) * glu).astype(BF16)
        y = jnp.dot(act, w2b_ref[...], preferred_element_type=F32) + b2_ref[0]
        y_ref[...] = x_ref[:, d:d + 1] * y

    @pl.when(i >= nu_ref[0])
    def _():
        y_ref[...] = jnp.zeros_like(y_ref)


def _experts(layer, block_expert, n_used, xbuf, w1, b1, w2, b2):
    rows, dx = xbuf.shape
    de, d, f2 = w1.shape
    f = w2.shape[1]
    blk = lambda i, be, nu: (jnp.maximum(jnp.minimum(i, nu[0] - 1), 0), 0)
    wsel = lambda i, be, nu: (layer * N_EXPERTS + be[i], 0, 0)
    grid_spec = pltpu.PrefetchScalarGridSpec(
        num_scalar_prefetch=2,
        grid=(rows // EXPERT_TILE,),
        in_specs=[pl.BlockSpec((EXPERT_TILE, dx), blk),
                  pl.BlockSpec(memory_space=pl.ANY),
                  pl.BlockSpec((1, 1, f2), wsel),
                  pl.BlockSpec(memory_space=pl.ANY),
                  pl.BlockSpec((1, 1, d), wsel)],
        out_specs=pl.BlockSpec((EXPERT_TILE, d), lambda i, be, nu: (i, 0)),
        scratch_shapes=[pltpu.VMEM((d, f2), F32), pltpu.VMEM((f, d), F32),
                        pltpu.VMEM((d, f2), BF16), pltpu.VMEM((f, d), BF16),
                        pltpu.SemaphoreType.DMA((2,))],
    )
    return pl.pallas_call(
        functools.partial(_expert_body, layer=layer),
        grid_spec=grid_spec,
        out_shape=jax.ShapeDtypeStruct((rows, d), F32),
        compiler_params=_cparams(("arbitrary",)),
        name="expert_swiglu",
    )(block_expert, n_used, xbuf, w1, b1.reshape(de, 1, f2), w2, b2.reshape(de, 1, d))


def _combine_body(run_tbl, base_tbl, x1_ref, g2_ref, pos_ref, fg_ref, ybuf_hbm, o_ref, ys_ref, sem, *, final):
    i = pl.program_id(0)
    last = pl.num_programs(0) - 1
    slot = i % 2
    sb = ys_ref.shape[1]

    def copies(slot):
        return lambda off, base, rows: pltpu.make_async_copy(
            ybuf_hbm.at[pl.ds(base, rows)], ys_ref.at[slot, pl.ds(off, rows)], sem.at[slot])

    @pl.when(i == 0)
    def _():
        ys_ref[...] = jnp.zeros_like(ys_ref)
        _for_each_run(run_tbl, base_tbl, 0, copies(0), True)

    @pl.when(i < last)
    def _():
        _for_each_run(run_tbl, base_tbl, i + 1, copies(1 - slot), True)

    _wait_for_tile(run_tbl, i, copies(slot), sb)

    _, any_hit = _slot_onehot(pos_ref[...], sb)
    perm = jnp.where(any_hit, 1.0, 0.0).astype(BF16)
    y = ys_ref[slot]
    y_hi = y.astype(BF16)
    y_lo = (y - y_hi.astype(F32)).astype(BF16)
    moe = (lax.dot_general(perm, y_hi, _TN, preferred_element_type=F32)
           + lax.dot_general(perm, y_lo, _TN, preferred_element_type=F32))
    out = x1_ref[...] + g2_ref[0] * moe
    if final:
        out = _rms(out) * fg_ref[...]
    o_ref[...] = out


def _combine(run_tbl, base_tbl, x1, g2, pos_t, final_g, ybuf, seq, final):
    n, d = x1.shape
    td = min(ROUTE_TILE, n)
    sb = _sorted_rows(td)
    grid_spec = pltpu.PrefetchScalarGridSpec(
        num_scalar_prefetch=2,
        grid=(n // td,),
        in_specs=[pl.BlockSpec((td, d), lambda i, *_: (i, 0)),
                  pl.BlockSpec((1, 1, d), lambda i, *_: ((i * td) // seq, 0, 0)),
                  pl.BlockSpec((TOP_K, td), lambda i, *_: (0, i)),
                  pl.BlockSpec((1, d), lambda i, *_: (0, 0)),
                  pl.BlockSpec(memory_space=pl.ANY)],
        out_specs=pl.BlockSpec((td, d), lambda i, *_: (i, 0)),
        scratch_shapes=[pltpu.VMEM((2, sb, d), F32), pltpu.SemaphoreType.DMA((2,))],
    )
    return pl.pallas_call(
        functools.partial(_combine_body, final=final),
        grid_spec=grid_spec,
        out_shape=jax.ShapeDtypeStruct((n, d), F32),
        compiler_params=_cparams(("arbitrary",)),
        name="combine_rows",
    )(run_tbl, base_tbl, x1, g2, pos_t, final_g, ybuf)


def _moe(layer, x1, h2, logits_t, g2, w1, b1, w2, b2, final_g, seq, final):
    n, d = h2.shape
    td = min(ROUTE_TILE, n)
    n_tiles = n // td
    assert seq % td == 0
    rows = (_round_up(n * TOP_K + n_tiles * N_EXPERTS * (RUN_ALIGN - 1), EXPERT_TILE)
            + N_EXPERTS * EXPERT_TILE)
    n_blocks = rows // EXPERT_TILE
    pos_t, gate_t, run_t = _route(logits_t)
    base_t, be, nu, tail = _layout(run_t, n_blocks)
    run_tbl = run_t[:, :n_tiles].T.astype(I32).reshape(-1)
    base_tbl = base_t[:, :n_tiles].T.reshape(-1)
    xbuf = _dispatch(run_tbl, base_tbl, tail[:, 1], tail[:, 0], h2, pos_t, gate_t, rows)
    ybuf = _experts(layer, be[0, :n_blocks], nu[0, :1], xbuf, w1, b1, w2, b2)
    return _combine(run_tbl, base_tbl, x1, g2, pos_t, final_g, ybuf, seq, final)


def kernel(x, c, positions, ada_w, ada_b, norm_mix, norm_ffn, w_in, b_forget, lambda_q1, lambda_k1,
           lambda_q2, lambda_k2, diff_subln, w_out, router_w, router_b, exp_w1, exp_b1, exp_w2, exp_b2,
           final_norm):
    b, s, d = x.shape
    depth = ada_w.shape[0]
    mod = _ada_mod(c, ada_w, ada_b)
    cosf, sinf = _rope_tables(positions)
    n_in = w_in.shape[2]
    pad_cols = 3072 + LANES - n_in
    flat = lambda a: a.reshape((a.shape[0] * a.shape[1],) + a.shape[2:])
    w1_all, b1_all, w2_all, b2_all = flat(exp_w1), flat(exp_b1), flat(exp_w2), flat(exp_b2)
    for l in range(depth):
        lam_init = 0.8 - 0.6 * math.exp(-0.3 * l)
        sh1, sc1, g1, sh2, sc2, g2 = [mod[l, :, i * d:(i + 1) * d].reshape(b, 1, d) for i in range(6)]
        w_pad = jnp.pad(w_in[l], ((0, 0), (0, pad_cols))).astype(BF16)
        bf_pad = jnp.pad(b_forget[l], (0, LANES - FOX_HEADS)).reshape(1, LANES)
        dq, dk, dv, fq, fk, fv, flog = _inproj(x, sc1, sh1, norm_mix[l].reshape(1, d), w_pad, bf_pad,
                                               cosf, sinf)
        cum = _forget_cumsum(flog)
        a_out = _diff_attention(dq, dk, dv, lambda_q1[l].reshape(1, -1), lambda_k1[l].reshape(1, -1),
                                lambda_q2[l].reshape(1, -1), lambda_k2[l].reshape(1, -1),
                                diff_subln[l].reshape(1, -1), lam_init)
        b_out = _fox_attention(fq, fk, fv, cum)
        x1, h2, logits_t = _outproj(x, a_out, b_out, w_out[l].astype(BF16), g1, sc2, sh2,
                                    norm_ffn[l].reshape(1, d), router_w[l].T,
                                    router_b[l].reshape(-1, 1))
        x = _moe(l, x1.reshape(b * s, d), h2, logits_t, g2, w1_all, b1_all, w2_all, b2_all,
                 final_norm.reshape(1, d), s, l == depth - 1).reshape(b, s, d)
    return x
```

```python
import functools
import math

import jax
import jax.numpy as jnp
from jax import lax
from jax.experimental import pallas as pl
from jax.experimental.pallas import tpu as pltpu

F32 = jnp.float32
BF16 = jnp.bfloat16
I32 = jnp.int32
HIGHEST = lax.Precision.HIGHEST

HEAD_DIM = 64
LANES = 128
SUBLANES = 8
CHUNK = 64
DIFF_HEADS = 4
FOX_HEADS = 8
D_DIFF = DIFF_HEADS * 2 * HEAD_DIM
D_FOX = FOX_HEADS * HEAD_DIM
ROPE_THETA = 500000.0
ROPE_DIM = HEAD_DIM // 4
ROPE_HALF = ROPE_DIM // 2
N_EXPERTS = 32
TOP_K = 4
SWIGLU_LIMIT = 7.0
SWIGLU_ALPHA = 1.702
NORM_EPS = 1e-6
NEG_BIG = -1e30
LOG2E = math.log2(math.e)
ATTN_Q_TILE = 1024
ATTN_K_TILE = 512
EXPERT_TILE = 512
ROUTE_TILE = 256
RUN_ALIGN = SUBLANES
VMEM_LIMIT = 56 * 1024 * 1024

_NT = (((1,), (1,)), ((), ()))
_TN = (((0,), (0,)), ((), ()))


def _cparams(sem, vmem=None):
    return pltpu.CompilerParams(dimension_semantics=sem, vmem_limit_bytes=vmem or VMEM_LIMIT)


def _rms(x):
    return x * lax.rsqrt(jnp.mean(x * x, axis=-1, keepdims=True) + NORM_EPS)


def _round_up(x, m):
    return (x + m - 1) // m * m


def _attn_tiles(s):
    tq = min(ATTN_Q_TILE, s)
    return tq, min(ATTN_K_TILE, tq)


def _ada_body(c_ref, w_ref, b_ref, o_ref):
    c = c_ref[...]
    ca = c * jax.nn.sigmoid(c)
    o_ref[0] = jnp.dot(ca, w_ref[0], preferred_element_type=F32, precision=HIGHEST) + b_ref[0]


def _ada_mod(c, ada_w, ada_b):
    depth, d, n6 = ada_w.shape
    b = c.shape[0]
    tn = 1536
    return pl.pallas_call(
        _ada_body,
        grid=(depth, n6 // tn),
        in_specs=[pl.BlockSpec((b, d), lambda l, j: (0, 0)),
                  pl.BlockSpec((1, d, tn), lambda l, j: (l, 0, j)),
                  pl.BlockSpec((1, 1, tn), lambda l, j: (l, 0, j))],
        out_specs=pl.BlockSpec((1, b, tn), lambda l, j: (l, 0, j)),
        out_shape=jax.ShapeDtypeStruct((depth, b, n6), F32),
        compiler_params=_cparams(("arbitrary", "arbitrary")),
        name="ada_mod",
    )(c, ada_w, ada_b.reshape(depth, 1, n6))


def _rope_body(pos_ref, invf_ref, cos_ref, sin_ref):
    ang = pos_ref[0].astype(F32) * invf_ref[...]
    j = lax.broadcasted_iota(I32, ang.shape, 1) & (HEAD_DIM - 1)
    c = jnp.cos(ang)
    s = jnp.sin(ang)
    cos_ref[0] = jnp.where(j < ROPE_DIM, c, 1.0)
    sin_ref[0] = jnp.where(j < ROPE_HALF, -s, jnp.where(j < ROPE_DIM, s, 0.0))


def _rope_tables(positions):
    b, s = positions.shape
    ts = min(512, s)
    inv_freq = ROPE_THETA ** (-jnp.arange(0, ROPE_DIM, 2, dtype=F32) / ROPE_DIM)
    invf = inv_freq[jnp.arange(LANES) % ROPE_HALF].reshape(1, LANES)
    out = jax.ShapeDtypeStruct((b, s, LANES), F32)
    return pl.pallas_call(
        _rope_body,
        grid=(b, s // ts),
        in_specs=[pl.BlockSpec((1, ts, 1), lambda i, j: (i, j, 0)),
                  pl.BlockSpec((1, LANES), lambda i, j: (0, 0))],
        out_specs=[pl.BlockSpec((1, ts, LANES), lambda i, j: (i, j, 0))] * 2,
        out_shape=[out, out],
        compiler_params=_cparams(("arbitrary", "arbitrary")),
        name="rope_tables",
    )(positions.reshape(b, s, 1), invf)


def _inproj_body(x_ref, sc_ref, sh_ref, g_ref, w_ref, bf_ref, cos_ref, sin_ref,
                 dq_ref, dk_ref, dv_ref, fq_ref, fk_ref, fv_ref, fl_ref):
    h = _rms(x_ref[0]) * g_ref[...] * (1.0 + sc_ref[0]) + sh_ref[0]
    hb = h.astype(BF16)
    cosf = cos_ref[0]
    sinf = sin_ref[0]
    lane = lax.broadcasted_iota(I32, cosf.shape, 1)
    first = (lane & (HEAD_DIM - 1)) < ROPE_HALF

    def rope(p):
        nxt = pltpu.roll(p, LANES - ROPE_HALF, 1)
        prv = pltpu.roll(p, ROPE_HALF, 1)
        return p * cosf + jnp.where(first, nxt, prv) * sinf

    def proj(c0):
        return jnp.dot(hb, w_ref[:, c0:c0 + 512], preferred_element_type=F32)

    scale = HEAD_DIM ** -0.5 * LOG2E
    p = proj(0)
    for c in range(4):
        sl = slice(c * LANES, (c + 1) * LANES)
        dq_ref[0, :, sl] = (rope(p[:, sl]) * scale).astype(BF16)
    p = proj(512)
    for c in range(4):
        sl = slice(c * LANES, (c + 1) * LANES)
        dk_ref[0, :, sl] = rope(p[:, sl]).astype(BF16)
    dv_ref[0] = proj(1024).astype(BF16)
    fq_ref[0] = (proj(1536) * scale).astype(BF16)
    fk_ref[0] = proj(2048).astype(BF16)
    fv_ref[0] = proj(2560).astype(BF16)
    z = jnp.dot(hb, w_ref[:, 3072:3072 + LANES], preferred_element_type=F32) + bf_ref[...]
    fl_ref[0] = -(jnp.maximum(-z, 0.0) + jnp.log1p(jnp.exp(-jnp.abs(z))))


def _inproj(x, sc, sh, g, w_pad, bf_pad, cosf, sinf):
    b, s, d = x.shape
    tm = min(512, s)
    act = lambda w: jax.ShapeDtypeStruct((b, s, w), BF16)
    row = lambda w: pl.BlockSpec((1, tm, w), lambda i, j: (i, j, 0))
    vec = pl.BlockSpec((1, 1, d), lambda i, j: (i, 0, 0))
    return pl.pallas_call(
        _inproj_body,
        grid=(b, s // tm),
        in_specs=[row(d), vec, vec,
                  pl.BlockSpec((1, d), lambda i, j: (0, 0)),
                  pl.BlockSpec(w_pad.shape, lambda i, j: (0, 0)),
                  pl.BlockSpec((1, LANES), lambda i, j: (0, 0)),
                  row(LANES), row(LANES)],
        out_specs=[row(512)] * 6 + [row(LANES)],
        out_shape=[act(512)] * 6 + [jax.ShapeDtypeStruct((b, s, LANES), F32)],
        compiler_params=_cparams(("arbitrary", "arbitrary")),
        name="in_proj",
    )(x, sc, sh, g, w_pad, bf_pad, cosf, sinf)


BIAS_LANES = 6


def _split3(c):
    hi = c.astype(BF16).astype(F32)
    r = c - hi
    mid = r.astype(BF16).astype(F32)
    return hi, mid, r - mid


def _cum_body(fl_ref, fq_ref, fk_ref):
    x = fl_ref[0]
    s = x.shape[0]
    row = lax.broadcasted_iota(I32, x.shape, 0)
    d = 1
    while d < s:
        x = x + jnp.where(row >= d, pltpu.roll(x, d, 0), 0.0)
        d *= 2
    pieces = jnp.concatenate(_split3(x * LOG2E), axis=1).astype(BF16)
    r = lax.broadcasted_iota(I32, (3 * LANES, LANES), 0)
    c = lax.broadcasted_iota(I32, (3 * LANES, LANES), 1)
    head, piece = r & (LANES - 1), r >> (LANES.bit_length() - 1)
    to_q = ((c == head * BIAS_LANES + piece) & (head < FOX_HEADS)).astype(BF16)
    to_k = ((c == head * BIAS_LANES + piece + 3) & (head < FOX_HEADS)).astype(BF16)
    lane = lax.broadcasted_iota(I32, (1, LANES), 1)
    ones_q = jnp.zeros((1, LANES), F32)
    ones_k = jnp.zeros((1, LANES), F32)
    for h in range(FOX_HEADS):
        ones_k = jnp.where((lane >= h * BIAS_LANES) & (lane < h * BIAS_LANES + 3), 1.0, ones_k)
        ones_q = jnp.where((lane >= h * BIAS_LANES + 3) & (lane < (h + 1) * BIAS_LANES), 1.0, ones_q)
    fq_ref[0] = jnp.dot(pieces, to_q, preferred_element_type=F32) + ones_q
    fk_ref[0] = ones_k - jnp.dot(pieces, to_k, preferred_element_type=F32)


def _forget_bias_lanes(flog):
    b, s, _ = flog.shape
    spec = pl.BlockSpec((1, s, LANES), lambda i: (i, 0, 0))
    out = jax.ShapeDtypeStruct((b, s, LANES), F32)
    return pl.pallas_call(
        _cum_body,
        grid=(b,),
        in_specs=[spec],
        out_specs=[spec, spec],
        out_shape=[out, out],
        compiler_params=_cparams(("arbitrary",)),
        name="forget_cumsum",
    )(flog)


def _causal_sweep(t, tq, tk, tile, state):
    ratio = tq // tk
    assert ratio in (1, 2) and ratio * tk == tq
    n_full = t * ratio

    def steps(k, base, trips, st):
        def body(i, st):
            for u in range(k):
                st = tile(base + k * i + u, st, False, 0)
            return st
        return lax.fori_loop(0, trips, body, st)

    state = steps(4, 0, n_full // 4, state)
    state = steps(2, (n_full // 4) * 4, (n_full // 2) % 2, state)
    if ratio == 1:
        state = steps(1, (n_full // 2) * 2, n_full % 2, state)
    state = tile(n_full, state, True, 0)
    return state, (tile(n_full + 1, state, True, tk) if ratio == 2 else None)


def _finish_rows(state, lower, tk, finish):
    if lower is None:
        return finish(state)
    top = finish(jax.tree.map(lambda a: a[:tk], state))
    return jnp.concatenate([top, finish(lower)], axis=0)


def _diff_body(q_ref, k_ref, v_ref, lq1_ref, lk1_ref, lq2_ref, lk2_ref, g_ref, o_ref, *, tq, tk, lam_init):
    t = pl.program_id(2)
    q = q_ref[0]
    lane = lax.broadcasted_iota(I32, q.shape, 1)
    zero = jnp.zeros_like(q)
    qs = (jnp.where(lane < HEAD_DIM, q, zero), jnp.where(lane >= HEAD_DIM, q, zero))

    def tile(j, state, masked, lo):
        off = pl.multiple_of(j * tk, tk)
        kt = k_ref[0, pl.ds(off, tk), :]
        vt = v_ref[0, pl.ds(off, tk), :]
        out = []
        for mp in range(2):
            s = lax.dot_general(qs[mp][lo:], kt, _NT, preferred_element_type=F32)
            if masked:
                r = lax.broadcasted_iota(I32, s.shape, 0)
                c = lax.broadcasted_iota(I32, s.shape, 1)
                shift = CHUNK.bit_length() - 1
                s = jnp.where((c >> shift) <= (r >> shift), s, NEG_BIG)
            m, l, acc = state[mp]
            m_new = jnp.maximum(m[lo:], jnp.max(s, axis=1, keepdims=True))
            alpha = jnp.exp2(m[lo:] - m_new)
            p = jnp.exp2(s - m_new[:, :1])
            l_new = alpha * l[lo:] + jnp.sum(p, axis=1, keepdims=True)
            acc_new = alpha * acc[lo:] + jnp.dot(p.astype(BF16), vt, preferred_element_type=F32)
            out.append((m_new, l_new, acc_new))
        return tuple(out)

    lam = (jnp.exp(jnp.sum(lq1_ref[...] * lk1_ref[...], axis=1, keepdims=True))
           - jnp.exp(jnp.sum(lq2_ref[...] * lk2_ref[...], axis=1, keepdims=True)) + lam_init)

    def finish(st):
        (_, l0, a0), (_, l1, a1) = st
        out = a0 / l0 - lam * (a1 / l1)
        return (_rms(out) * g_ref[...] * (1.0 - lam_init)).astype(o_ref.dtype)

    init = (jnp.full((tq, LANES), NEG_BIG, F32), jnp.zeros((tq, LANES), F32), jnp.zeros((tq, LANES), F32))
    o_ref[0] = _finish_rows(*_causal_sweep(t, tq, tk, tile, (init, init)), tk, finish)


def _diff_attention(dq, dk, dv, lq1, lk1, lq2, lk2, subln, lam_init):
    b, s, _ = dq.shape
    tq, tk = _attn_tiles(s)
    qspec = pl.BlockSpec((1, tq, LANES), lambda i, h, t: (i, t, h))
    kvspec = pl.BlockSpec((1, s, LANES), lambda i, h, t: (i, 0, h))
    lspec = pl.BlockSpec((1, HEAD_DIM), lambda i, h, t: (0, 0))
    return pl.pallas_call(
        functools.partial(_diff_body, tq=tq, tk=tk, lam_init=lam_init),
        grid=(b, DIFF_HEADS, s // tq),
        in_specs=[qspec, kvspec, kvspec, lspec, lspec, lspec, lspec,
                  pl.BlockSpec((1, LANES), lambda i, h, t: (0, 0))],
        out_specs=qspec,
        out_shape=jax.ShapeDtypeStruct((b, s, D_DIFF), BF16),
        compiler_params=_cparams(("arbitrary", "arbitrary", "arbitrary")),
        name="diff_attention",
    )(dq, dk, dv, lq1, lk1, lq2, lk2, subln)


def _fox_operand(x_f32, bias, hh, h, lane):
    xh = x_f32 if hh == 0 else pltpu.roll(x_f32, HEAD_DIM, 1)
    bh = pltpu.roll(bias, HEAD_DIM - BIAS_LANES * h, 1)
    return jnp.where(lane < HEAD_DIM, xh, jnp.where(lane < HEAD_DIM + BIAS_LANES, bh, 0.0)).astype(BF16)


def _fox_body(q_ref, k_ref, v_ref, cq_ref, ck_ref, o_ref, ka_ref, va_ref, *, tq, tk):
    pair = pl.program_id(1)
    t = pl.program_id(2)
    heads = (2 * pair, 2 * pair + 1)
    lane = lax.broadcasted_iota(I32, (tq, LANES), 1)
    klane = lax.broadcasted_iota(I32, (tk, LANES), 1)

    @pl.when(t == 0)
    def _():
        def chunk(ci, carry):
            rows = pl.ds(pl.multiple_of(ci * tk, tk), tk)
            kf = k_ref[0, rows, :].astype(F32)
            vf = v_ref[0, rows, :].astype(F32)
            bias = ck_ref[0, rows, :]
            for hh in range(2):
                ka_ref[hh, rows, :] = _fox_operand(kf, bias, hh, heads[hh], klane)
                vh = vf if hh == 0 else pltpu.roll(vf, HEAD_DIM, 1)
                va_ref[hh, rows, :] = jnp.where(klane < HEAD_DIM, vh, 1.0).astype(BF16)
            return carry

        lax.fori_loop(0, k_ref.shape[1] // tk, chunk, 0)

    qf = q_ref[0].astype(F32)
    qs = [_fox_operand(qf, cq_ref[0], hh, heads[hh], lane) for hh in range(2)]

    def tile(j, state, masked, lo):
        off = pl.multiple_of(j * tk, tk)
        out = []
        for hh in range(2):
            s = lax.dot_general(qs[hh][lo:], ka_ref[hh, pl.ds(off, tk), :], _NT, preferred_element_type=F32)
            if masked:
                r = lax.broadcasted_iota(I32, s.shape, 0)
                c = lax.broadcasted_iota(I32, s.shape, 1)
                s = jnp.where(c <= r, s, NEG_BIG)
            m, acc = state[hh]
            m_new = jnp.maximum(m[lo:], jnp.max(s, axis=1, keepdims=True))
            p = jnp.exp2(s - m_new[:, :1]).astype(BF16)
            acc_new = jnp.exp2(m[lo:] - m_new) * acc[lo:] + jnp.dot(p, va_ref[hh, pl.ds(off, tk), :],
                                                                    preferred_element_type=F32)
            out.append((m_new, acc_new))
        return tuple(out)

    def finish(st):
        (_, a0), (_, a1) = st
        o0 = a0 / pltpu.roll(a0, HEAD_DIM, 1)
        o1 = a1 / pltpu.roll(a1, HEAD_DIM, 1)
        first = lax.broadcasted_iota(I32, a0.shape, 1) < HEAD_DIM
        return jnp.where(first, o0, pltpu.roll(o1, HEAD_DIM, 1)).astype(o_ref.dtype)

    init = (jnp.full((tq, LANES), NEG_BIG, F32), jnp.zeros((tq, LANES), F32))
    o_ref[0] = _finish_rows(*_causal_sweep(t, tq, tk, tile, (init, init)), tk, finish)


def _fox_attention(fq, fk, fv, bias_q, bias_k):
    b, s, _ = fq.shape
    tq, tk = _attn_tiles(s)
    qspec = pl.BlockSpec((1, tq, LANES), lambda i, p, t: (i, t, p))
    kvspec = pl.BlockSpec((1, s, LANES), lambda i, p, t: (i, 0, p))
    return pl.pallas_call(
        functools.partial(_fox_body, tq=tq, tk=tk),
        grid=(b, FOX_HEADS // 2, s // tq),
        in_specs=[qspec, kvspec, kvspec,
                  pl.BlockSpec((1, tq, LANES), lambda i, p, t: (i, t, 0)),
                  pl.BlockSpec((1, s, LANES), lambda i, p, t: (i, 0, 0))],
        out_specs=qspec,
        out_shape=jax.ShapeDtypeStruct((b, s, D_FOX), BF16),
        scratch_shapes=[pltpu.VMEM((2, s, LANES), BF16), pltpu.VMEM((2, s, LANES), BF16)],
        compiler_params=_cparams(("arbitrary", "arbitrary", "arbitrary")),
        name="fox_attention",
    )(fq, fk, fv, bias_q, bias_k)


def _outproj_body(x_ref, a_ref, b_ref, w_ref, g1_ref, sc_ref, sh_ref, ng_ref, rw_ref, rb_ref,
                  x1_ref, h2_ref, lg_ref):
    mix = (jnp.dot(a_ref[0], w_ref[:D_DIFF, :], preferred_element_type=F32)
           + jnp.dot(b_ref[0], w_ref[D_DIFF:, :], preferred_element_type=F32))
    x1 = x_ref[0] + g1_ref[0] * mix
    x1_ref[0] = x1
    h2 = _rms(x1) * ng_ref[...] * (1.0 + sc_ref[0]) + sh_ref[0]
    h_hi = h2.astype(BF16)
    h2_ref[...] = h_hi
    h_lo = (h2 - h_hi.astype(F32)).astype(BF16)
    rw = rw_ref[...]
    w_hi = rw.astype(BF16)
    w_lo = (rw - w_hi.astype(F32)).astype(BF16)
    nt = lambda a, b: lax.dot_general(a, b, _NT, preferred_element_type=F32)
    lg_ref[...] = nt(w_hi, h_hi) + (nt(w_hi, h_lo) + nt(w_lo, h_hi)) + rb_ref[...]


def _outproj(x, a_out, b_out, w_out, g1, sc2, sh2, ng, rwt, rb):
    b, s, d = x.shape
    n = b * s
    tm = min(512, s)
    nt = s // tm
    row = lambda w: pl.BlockSpec((1, tm, w), lambda i, j: (i, j, 0))
    vec = pl.BlockSpec((1, 1, d), lambda i, j: (i, 0, 0))
    const = lambda shp: pl.BlockSpec(shp, lambda i, j: (0,) * len(shp))
    return pl.pallas_call(
        _outproj_body,
        grid=(b, nt),
        in_specs=[row(d), row(D_DIFF), row(D_FOX), const(w_out.shape), vec, vec, vec,
                  const((1, d)), const(rwt.shape), const(rb.shape)],
        out_specs=[row(d),
                   pl.BlockSpec((tm, d), lambda i, j: (i * nt + j, 0)),
                   pl.BlockSpec((N_EXPERTS, tm), lambda i, j: (0, i * nt + j))],
        out_shape=[jax.ShapeDtypeStruct((b, s, d), F32),
                   jax.ShapeDtypeStruct((n, d), BF16),
                   jax.ShapeDtypeStruct((N_EXPERTS, n), F32)],
        compiler_params=_cparams(("arbitrary", "arbitrary")),
        name="out_proj",
    )(x, a_out, b_out, w_out, g1, sc2, sh2, ng, rwt, rb)


def _strict_lower(n):
    r = lax.broadcasted_iota(I32, (n, n), 0)
    c = lax.broadcasted_iota(I32, (n, n), 1)
    return (c < r).astype(F32)


def _align_up(x_f32, m):
    shift = m.bit_length() - 1
    return (((x_f32.astype(I32) + (m - 1)) >> shift) << shift).astype(F32)


def _route_body(lg_ref, pos_ref, gate_ref, cnt_ref):
    i = pl.program_id(0)

    @pl.when(i == 0)
    def _():
        cnt_ref[...] = jnp.zeros_like(cnt_ref)

    work = lg_ref[...]
    td = work.shape[1]
    eidx = lax.broadcasted_iota(I32, work.shape, 0)
    vals, hots = [], []
    for k in range(TOP_K):
        m = jnp.max(work, axis=0, keepdims=True)
        sel = jnp.min(jnp.where(work == m, eidx, N_EXPERTS), axis=0, keepdims=True)
        hot = eidx == sel
        vals.append(m)
        hots.append(hot)
        work = jnp.where(hot, -jnp.inf, work)
    ex = [jnp.exp(v - vals[0]) for v in vals]
    den = ex[0] + ex[1] + ex[2] + ex[3]
    for k in range(TOP_K):
        gate_ref[k:k + 1, :] = ex[k] / den

    chosen = hots[0] | hots[1] | hots[2] | hots[3]
    r = lax.broadcasted_iota(I32, (td, td), 0)
    c = lax.broadcasted_iota(I32, (td, td), 1)
    before = (r < c).astype(BF16)
    earlier = jnp.dot(chosen.astype(BF16), before, preferred_element_type=F32)
    cnt = jnp.sum(chosen.astype(F32), axis=1, keepdims=True)
    run = _align_up(cnt, RUN_ALIGN)
    start = jnp.dot(_strict_lower(N_EXPERTS), jnp.broadcast_to(run, (N_EXPERTS, LANES)),
                    preferred_element_type=F32, precision=HIGHEST)[:, 0:1]
    slot = start + earlier
    for k in range(TOP_K):
        pos_ref[k:k + 1, :] = jnp.sum(jnp.where(hots[k], slot, 0.0), axis=0, keepdims=True).astype(I32)
    lane = lax.broadcasted_iota(I32, cnt_ref.shape, 1)
    cnt_ref[...] = jnp.where(lane == i, run, cnt_ref[...])


def _route(logits_t):
    e, n = logits_t.shape
    td = min(ROUTE_TILE, n)
    ntp = _round_up(n // td, LANES)
    tok = pl.BlockSpec((TOP_K, td), lambda i: (0, i))
    return pl.pallas_call(
        _route_body,
        grid=(n // td,),
        in_specs=[pl.BlockSpec((e, td), lambda i: (0, i))],
        out_specs=[tok, tok, pl.BlockSpec((e, ntp), lambda i: (0, 0))],
        out_shape=[jax.ShapeDtypeStruct((TOP_K, n), I32),
                   jax.ShapeDtypeStruct((TOP_K, n), F32),
                   jax.ShapeDtypeStruct((e, ntp), F32)],
        compiler_params=_cparams(("arbitrary",)),
        name="route_topk",
    )(logits_t)


def _layout_body(run_ref, base_ref, be_ref, nu_ref, tail_ref):
    run = run_ref[...]
    ntp = run.shape[1]
    total = jnp.sum(run, axis=1, keepdims=True)
    region = _align_up(total, EXPERT_TILE)
    pstart = jnp.dot(_strict_lower(N_EXPERTS), jnp.broadcast_to(region, (N_EXPERTS, LANES)),
                     preferred_element_type=F32, precision=HIGHEST)[:, 0:1]
    ti = lax.broadcasted_iota(I32, (ntp, ntp), 0)
    tj = lax.broadcasted_iota(I32, (ntp, ntp), 1)
    within = jnp.dot(run, (ti < tj).astype(F32), preferred_element_type=F32, precision=HIGHEST)
    base_ref[...] = (pstart + within).astype(I32)
    pend = pstart + region
    blk = (lax.broadcasted_iota(I32, (N_EXPERTS, be_ref.shape[1]), 1) * EXPERT_TILE).astype(F32)
    be = jnp.sum((pend <= blk).astype(I32), axis=0, keepdims=True)
    be_ref[...] = jnp.minimum(be, N_EXPERTS - 1)
    used = jnp.sum(region, axis=0, keepdims=True).astype(I32) >> (EXPERT_TILE.bit_length() - 1)
    nu_ref[...] = jnp.broadcast_to(used, nu_ref.shape)
    lane = lax.broadcasted_iota(I32, tail_ref.shape, 1)
    tail_ref[...] = jnp.where(lane == 0, pstart + total, jnp.where(lane == 1, region - total, 0.0)).astype(I32)


def _layout(run_t, n_blocks):
    e, ntp = run_t.shape
    nbp = _round_up(n_blocks, LANES)
    full = lambda shp: pl.BlockSpec(shp, lambda: (0,) * len(shp))
    return pl.pallas_call(
        _layout_body,
        in_specs=[full((e, ntp))],
        out_specs=[full((e, ntp)), full((1, nbp)), full((1, LANES)), full((e, LANES))],
        out_shape=[jax.ShapeDtypeStruct((e, ntp), I32),
                   jax.ShapeDtypeStruct((1, nbp), I32),
                   jax.ShapeDtypeStruct((1, LANES), I32),
                   jax.ShapeDtypeStruct((e, LANES), I32)],
        name="expert_layout",
    )(run_t)


_RUN_LEVELS = tuple(1 << b for b in range(ROUTE_TILE.bit_length() - 1, RUN_ALIGN.bit_length() - 2, -1))


def _for_each_run(run_tbl, base_tbl, tile, make_copy, start):
    def body(e, off):
        cnt = run_tbl[tile * N_EXPERTS + e]
        base = base_tbl[tile * N_EXPERTS + e]
        done = jnp.int32(0)
        for lvl in _RUN_LEVELS:
            bit = cnt & lvl

            @pl.when(bit != 0)
            def _(done=done, lvl=lvl):
                cp = make_copy(pl.multiple_of(off + done, RUN_ALIGN), pl.multiple_of(base + done, RUN_ALIGN), lvl)
                if start:
                    cp.start()
                else:
                    cp.wait()
            done = done + bit
        return off + cnt

    lax.fori_loop(0, N_EXPERTS, body, jnp.int32(0))


def _wait_for_tile(run_tbl, tile, make_copy, sb):
    total = lax.fori_loop(0, N_EXPERTS, lambda e, acc: acc + run_tbl[tile * N_EXPERTS + e], jnp.int32(0))
    lvl = 1 << (sb.bit_length() - 1)
    while lvl >= RUN_ALIGN:
        @pl.when((total & lvl) != 0)
        def _(lvl=lvl):
            make_copy(0, 0, lvl).wait()
        lvl //= 2


def _slot_onehot(pos, sb):
    j = lax.broadcasted_iota(I32, (sb, pos.shape[1]), 0)
    hits = [j == pos[k:k + 1, :] for k in range(TOP_K)]
    return hits, (hits[0] | hits[1] | hits[2] | hits[3])


def _sorted_rows(td):
    return _round_up(TOP_K * td + N_EXPERTS * (RUN_ALIGN - 1), LANES)


def _dispatch_body(run_tbl, base_tbl, tail_len, tail_start, h_ref, pos_ref, gate_ref, xbuf_hbm,
                   sorted_ref, zero_ref, sem, zsem, *, d):
    i = pl.program_id(0)
    last = pl.num_programs(0) - 1
    slot = i % 2
    sb = sorted_ref.shape[1]

    def zero_copies(off, base, rows):
        del off
        return pltpu.make_async_copy(zero_ref.at[pl.ds(0, rows)], xbuf_hbm.at[pl.ds(base, rows)], zsem)

    @pl.when(i == 0)
    def _():
        zero_ref[...] = jnp.zeros_like(zero_ref)
        _for_each_run(tail_len, tail_start, 0, zero_copies, True)

    hits, any_hit = _slot_onehot(pos_ref[...], sb)
    perm = jnp.where(any_hit, 1.0, 0.0).astype(BF16)
    sorted_ref[slot, :, :d] = jnp.dot(perm, h_ref[...], preferred_element_type=F32)
    gates = gate_ref[...]
    gsel = jnp.where(hits[0], gates[0:1, :], 0.0)
    for k in range(1, TOP_K):
        gsel = gsel + jnp.where(hits[k], gates[k:k + 1, :], 0.0)
    sorted_ref[slot, :, d:] = jnp.broadcast_to(jnp.sum(gsel, axis=1, keepdims=True), (sb, LANES))

    def copies(slot):
        return lambda off, base, rows: pltpu.make_async_copy(
            sorted_ref.at[slot, pl.ds(off, rows)], xbuf_hbm.at[pl.ds(base, rows)], sem.at[slot])

    _for_each_run(run_tbl, base_tbl, i, copies(slot), True)

    @pl.when(i > 0)
    def _():
        _wait_for_tile(run_tbl, i - 1, copies(1 - slot), sb)

    @pl.when(i == last)
    def _():
        _wait_for_tile(run_tbl, i, copies(slot), sb)
        _for_each_run(tail_len, tail_start, 0, zero_copies, False)


def _dispatch(run_tbl, base_tbl, tail_len, tail_start, h2, pos_t, gate_t, rows):
    n, d = h2.shape
    td = min(ROUTE_TILE, n)
    sb = _sorted_rows(td)
    dx = d + LANES
    assert EXPERT_TILE <= 2 * _RUN_LEVELS[0]
    tok = pl.BlockSpec((TOP_K, td), lambda i, *_: (0, i))
    grid_spec = pltpu.PrefetchScalarGridSpec(
        num_scalar_prefetch=4,
        grid=(n // td,),
        in_specs=[pl.BlockSpec((td, d), lambda i, *_: (i, 0)), tok, tok],
        out_specs=pl.BlockSpec(memory_space=pl.ANY),
        scratch_shapes=[pltpu.VMEM((2, sb, dx), F32), pltpu.VMEM((_RUN_LEVELS[0], dx), F32),
                        pltpu.SemaphoreType.DMA((2,)), pltpu.SemaphoreType.DMA],
    )
    return pl.pallas_call(
        functools.partial(_dispatch_body, d=d),
        grid_spec=grid_spec,
        out_shape=jax.ShapeDtypeStruct((rows, dx), F32),
        compiler_params=_cparams(("arbitrary",)),
        name="dispatch_rows",
    )(run_tbl, base_tbl, tail_len, tail_start, h2, pos_t, gate_t)


def _expert_body(be_ref, nu_ref, x_ref, w1_hbm, b1_ref, w2_hbm, b2_ref, y_ref,
                 w1f_ref, w2f_ref, w1b_ref, w2b_ref, sem, *, layer):
    i = pl.program_id(0)
    n = pl.num_programs(0)
    d_ff, d = w2b_ref.shape

    def fetch(blk):
        e = layer * N_EXPERTS + be_ref[blk]
        return (pltpu.make_async_copy(w1_hbm.at[e], w1f_ref, sem.at[0]),
                pltpu.make_async_copy(w2_hbm.at[e], w2f_ref, sem.at[1]))

    @pl.when(i == 0)
    def _():
        for cp in fetch(0):
            cp.start()

    @pl.when((i == 0) | (be_ref[i] != be_ref[jnp.maximum(i - 1, 0)]))
    def _():
        for cp in fetch(i):
            cp.wait()
        w1b_ref[...] = w1f_ref[...].astype(BF16)
        w2b_ref[...] = w2f_ref[...].astype(BF16)
        nxt = lax.while_loop(lambda j: (j < n) & (be_ref[jnp.minimum(j, n - 1)] == be_ref[i]),
                             lambda j: j + 1, i + 1)

        @pl.when(nxt < n)
        def _():
            for cp in fetch(nxt):
                cp.start()

    @pl.when(i < nu_ref[0])
    def _():
        gu = jnp.dot(x_ref[:, :d].astype(BF16), w1b_ref[...], preferred_element_type=F32) + b1_ref[0]
        gate = jnp.minimum(gu[:, :d_ff], SWIGLU_LIMIT)
        up = jnp.clip(gu[:, d_ff:], -SWIGLU_LIMIT, SWIGLU_LIMIT)
        glu = gate * jax.nn.sigmoid(gate * SWIGLU_ALPHA)
        act = ((up + 1.0) * glu).astype(BF16)
        y = jnp.dot(act, w2b_ref[...], preferred_element_type=F32) + b2_ref[0]
        y_ref[...] = x_ref[:, d:d + 1] * y

    @pl.when(i >= nu_ref[0])
    def _():
        y_ref[...] = jnp.zeros_like(y_ref)


def _experts(layer, block_expert, n_used, xbuf, w1, b1, w2, b2):
    rows, dx = xbuf.shape
    de, d, f2 = w1.shape
    f = w2.shape[1]
    blk = lambda i, be, nu: (jnp.maximum(jnp.minimum(i, nu[0] - 1), 0), 0)
    wsel = lambda i, be, nu: (layer * N_EXPERTS + be[i], 0, 0)
    grid_spec = pltpu.PrefetchScalarGridSpec(
        num_scalar_prefetch=2,
        grid=(rows // EXPERT_TILE,),
        in_specs=[pl.BlockSpec((EXPERT_TILE, dx), blk),
                  pl.BlockSpec(memory_space=pl.ANY),
                  pl.BlockSpec((1, 1, f2), wsel),
                  pl.BlockSpec(memory_space=pl.ANY),
                  pl.BlockSpec((1, 1, d), wsel)],
        out_specs=pl.BlockSpec((EXPERT_TILE, d), lambda i, be, nu: (i, 0)),
        scratch_shapes=[pltpu.VMEM((d, f2), F32), pltpu.VMEM((f, d), F32),
                        pltpu.VMEM((d, f2), BF16), pltpu.VMEM((f, d), BF16),
                        pltpu.SemaphoreType.DMA((2,))],
    )
    return pl.pallas_call(
        functools.partial(_expert_body, layer=layer),
        grid_spec=grid_spec,
        out_shape=jax.ShapeDtypeStruct((rows, d), F32),
        compiler_params=_cparams(("arbitrary",)),
        name="expert_swiglu",
    )(block_expert, n_used, xbuf, w1, b1.reshape(de, 1, f2), w2, b2.reshape(de, 1, d))


def _combine_body(run_tbl, base_tbl, x1_ref, g2_ref, pos_ref, fg_ref, ybuf_hbm, o_ref, ys_ref, sem, *, final):
    i = pl.program_id(0)
    last = pl.num_programs(0) - 1
    slot = i % 2
    sb = ys_ref.shape[1]

    def copies(slot):
        return lambda off, base, rows: pltpu.make_async_copy(
            ybuf_hbm.at[pl.ds(base, rows)], ys_ref.at[slot, pl.ds(off, rows)], sem.at[slot])

    @pl.when(i == 0)
    def _():
        ys_ref[...] = jnp.zeros_like(ys_ref)
        _for_each_run(run_tbl, base_tbl, 0, copies(0), True)

    @pl.when(i < last)
    def _():
        _for_each_run(run_tbl, base_tbl, i + 1, copies(1 - slot), True)

    _wait_for_tile(run_tbl, i, copies(slot), sb)

    _, any_hit = _slot_onehot(pos_ref[...], sb)
    perm = jnp.where(any_hit, 1.0, 0.0).astype(BF16)
    moe = lax.dot_general(perm, ys_ref[slot].astype(BF16), _TN, preferred_element_type=F32)
    out = x1_ref[...] + g2_ref[0] * moe
    if final:
        out = _rms(out) * fg_ref[...]
    o_ref[...] = out


def _combine(run_tbl, base_tbl, x1, g2, pos_t, final_g, ybuf, seq, final):
    n, d = x1.shape
    td = min(ROUTE_TILE, n)
    sb = _sorted_rows(td)
    grid_spec = pltpu.PrefetchScalarGridSpec(
        num_scalar_prefetch=2,
        grid=(n // td,),
        in_specs=[pl.BlockSpec((td, d), lambda i, *_: (i, 0)),
                  pl.BlockSpec((1, 1, d), lambda i, *_: ((i * td) // seq, 0, 0)),
                  pl.BlockSpec((TOP_K, td), lambda i, *_: (0, i)),
                  pl.BlockSpec((1, d), lambda i, *_: (0, 0)),
                  pl.BlockSpec(memory_space=pl.ANY)],
        out_specs=pl.BlockSpec((td, d), lambda i, *_: (i, 0)),
        scratch_shapes=[pltpu.VMEM((2, sb, d), F32), pltpu.SemaphoreType.DMA((2,))],
    )
    return pl.pallas_call(
        functools.partial(_combine_body, final=final),
        grid_spec=grid_spec,
        out_shape=jax.ShapeDtypeStruct((n, d), F32),
        compiler_params=_cparams(("arbitrary",)),
        name="combine_rows",
    )(run_tbl, base_tbl, x1, g2, pos_t, final_g, ybuf)


def _moe(layer, x1, h2, logits_t, g2, w1, b1, w2, b2, final_g, seq, final):
    n, d = h2.shape
    td = min(ROUTE_TILE, n)
    n_tiles = n // td
    assert seq % td == 0
    rows = (_round_up(n * TOP_K + n_tiles * N_EXPERTS * (RUN_ALIGN - 1), EXPERT_TILE)
            + N_EXPERTS * EXPERT_TILE)
    n_blocks = rows // EXPERT_TILE
    pos_t, gate_t, run_t = _route(logits_t)
    base_t, be, nu, tail = _layout(run_t, n_blocks)
    run_tbl = run_t[:, :n_tiles].T.astype(I32).reshape(-1)
    base_tbl = base_t[:, :n_tiles].T.reshape(-1)
    xbuf = _dispatch(run_tbl, base_tbl, tail[:, 1], tail[:, 0], h2, pos_t, gate_t, rows)
    ybuf = _experts(layer, be[0, :n_blocks], nu[0, :1], xbuf, w1, b1, w2, b2)
    return _combine(run_tbl, base_tbl, x1, g2, pos_t, final_g, ybuf, seq, final)


def kernel(x, c, positions, ada_w, ada_b, norm_mix, norm_ffn, w_in, b_forget, lambda_q1, lambda_k1,
           lambda_q2, lambda_k2, diff_subln, w_out, router_w, router_b, exp_w1, exp_b1, exp_w2, exp_b2,
           final_norm):
    b, s, d = x.shape
    depth = ada_w.shape[0]
    mod = _ada_mod(c, ada_w, ada_b)
    cosf, sinf = _rope_tables(positions)
    n_in = w_in.shape[2]
    pad_cols = 3072 + LANES - n_in
    flat = lambda a: a.reshape((a.shape[0] * a.shape[1],) + a.shape[2:])
    w1_all, b1_all, w2_all, b2_all = flat(exp_w1), flat(exp_b1), flat(exp_w2), flat(exp_b2)
    for l in range(depth):
        lam_init = 0.8 - 0.6 * math.exp(-0.3 * l)
        sh1, sc1, g1, sh2, sc2, g2 = [mod[l, :, i * d:(i + 1) * d].reshape(b, 1, d) for i in range(6)]
        w_pad = jnp.pad(w_in[l], ((0, 0), (0, pad_cols))).astype(BF16)
        bf_pad = jnp.pad(b_forget[l], (0, LANES - FOX_HEADS)).reshape(1, LANES)
        dq, dk, dv, fq, fk, fv, flog = _inproj(x, sc1, sh1, norm_mix[l].reshape(1, d), w_pad, bf_pad,
                                               cosf, sinf)
        bias_q, bias_k = _forget_bias_lanes(flog)
        a_out = _diff_attention(dq, dk, dv, lambda_q1[l].reshape(1, -1), lambda_k1[l].reshape(1, -1),
                                lambda_q2[l].reshape(1, -1), lambda_k2[l].reshape(1, -1),
                                diff_subln[l].reshape(1, -1), lam_init)
        b_out = _fox_attention(fq, fk, fv, bias_q, bias_k)
        x1, h2, logits_t = _outproj(x, a_out, b_out, w_out[l].astype(BF16), g1, sc2, sh2,
                                    norm_ffn[l].reshape(1, d), router_w[l].T,
                                    router_b[l].reshape(-1, 1))
        x = _moe(l, x1.reshape(b * s, d), h2, logits_t, g2, w1_all, b1_all, w2_all, b2_all,
                 final_norm.reshape(1, d), s, l == depth - 1).reshape(b, s, d)
    return x
```

```python
import functools
import math

import jax
import jax.numpy as jnp
from jax import lax
from jax.experimental import pallas as pl
from jax.experimental.pallas import tpu as pltpu

F32 = jnp.float32
BF16 = jnp.bfloat16
I32 = jnp.int32
HIGHEST = lax.Precision.HIGHEST

HEAD_DIM = 64
LANES = 128
SUBLANES = 8
CHUNK = 64
DIFF_HEADS = 4
FOX_HEADS = 8
D_DIFF = DIFF_HEADS * 2 * HEAD_DIM
D_FOX = FOX_HEADS * HEAD_DIM
ROPE_THETA = 500000.0
ROPE_DIM = HEAD_DIM // 4
ROPE_HALF = ROPE_DIM // 2
N_EXPERTS = 32
TOP_K = 4
SWIGLU_LIMIT = 7.0
SWIGLU_ALPHA = 1.702
NORM_EPS = 1e-6
NEG_BIG = -1e30
LOG2E = math.log2(math.e)
ATTN_Q_TILE = 1024
ATTN_K_TILE = 512
EXPERT_TILE = 512
ROUTE_TILE = 256
RUN_ALIGN = SUBLANES
VMEM_LIMIT = 56 * 1024 * 1024

_NT = (((1,), (1,)), ((), ()))
_TN = (((0,), (0,)), ((), ()))


def _cparams(sem, vmem=None):
    return pltpu.CompilerParams(dimension_semantics=sem, vmem_limit_bytes=vmem or VMEM_LIMIT)


def _rms(x):
    return x * lax.rsqrt(jnp.mean(x * x, axis=-1, keepdims=True) + NORM_EPS)


def _round_up(x, m):
    return (x + m - 1) // m * m


def _attn_tiles(s):
    tq = min(ATTN_Q_TILE, s)
    return tq, min(ATTN_K_TILE, tq)


def _ada_body(c_ref, w_ref, b_ref, o_ref):
    c = c_ref[...]
    ca = c * jax.nn.sigmoid(c)
    o_ref[0] = jnp.dot(ca, w_ref[0], preferred_element_type=F32, precision=HIGHEST) + b_ref[0]


def _ada_mod(c, ada_w, ada_b):
    depth, d, n6 = ada_w.shape
    b = c.shape[0]
    tn = 1536
    return pl.pallas_call(
        _ada_body,
        grid=(depth, n6 // tn),
        in_specs=[pl.BlockSpec((b, d), lambda l, j: (0, 0)),
                  pl.BlockSpec((1, d, tn), lambda l, j: (l, 0, j)),
                  pl.BlockSpec((1, 1, tn), lambda l, j: (l, 0, j))],
        out_specs=pl.BlockSpec((1, b, tn), lambda l, j: (l, 0, j)),
        out_shape=jax.ShapeDtypeStruct((depth, b, n6), F32),
        compiler_params=_cparams(("arbitrary", "arbitrary")),
        name="ada_mod",
    )(c, ada_w, ada_b.reshape(depth, 1, n6))


def _rope_body(pos_ref, invf_ref, cos_ref, sin_ref):
    ang = pos_ref[0].astype(F32) * invf_ref[...]
    j = lax.broadcasted_iota(I32, ang.shape, 1) & (HEAD_DIM - 1)
    c = jnp.cos(ang)
    s = jnp.sin(ang)
    cos_ref[0] = jnp.where(j < ROPE_DIM, c, 1.0)
    sin_ref[0] = jnp.where(j < ROPE_HALF, -s, jnp.where(j < ROPE_DIM, s, 0.0))


def _rope_tables(positions):
    b, s = positions.shape
    ts = min(512, s)
    inv_freq = ROPE_THETA ** (-jnp.arange(0, ROPE_DIM, 2, dtype=F32) / ROPE_DIM)
    invf = inv_freq[jnp.arange(LANES) % ROPE_HALF].reshape(1, LANES)
    out = jax.ShapeDtypeStruct((b, s, LANES), F32)
    return pl.pallas_call(
        _rope_body,
        grid=(b, s // ts),
        in_specs=[pl.BlockSpec((1, ts, 1), lambda i, j: (i, j, 0)),
                  pl.BlockSpec((1, LANES), lambda i, j: (0, 0))],
        out_specs=[pl.BlockSpec((1, ts, LANES), lambda i, j: (i, j, 0))] * 2,
        out_shape=[out, out],
        compiler_params=_cparams(("arbitrary", "arbitrary")),
        name="rope_tables",
    )(positions.reshape(b, s, 1), invf)


def _inproj_body(x_ref, sc_ref, sh_ref, g_ref, w_ref, bf_ref, cos_ref, sin_ref,
                 dq_ref, dk_ref, dv_ref, fq_ref, fk_ref, fv_ref, fl_ref):
    h = _rms(x_ref[0]) * g_ref[...] * (1.0 + sc_ref[0]) + sh_ref[0]
    hb = h.astype(BF16)
    cosf = cos_ref[0]
    sinf = sin_ref[0]
    lane = lax.broadcasted_iota(I32, cosf.shape, 1)
    first = (lane & (HEAD_DIM - 1)) < ROPE_HALF

    def rope(p):
        nxt = pltpu.roll(p, LANES - ROPE_HALF, 1)
        prv = pltpu.roll(p, ROPE_HALF, 1)
        return p * cosf + jnp.where(first, nxt, prv) * sinf

    def proj(c0):
        return jnp.dot(hb, w_ref[:, c0:c0 + 512], preferred_element_type=F32)

    scale = HEAD_DIM ** -0.5 * LOG2E
    p = proj(0)
    for c in range(4):
        sl = slice(c * LANES, (c + 1) * LANES)
        dq_ref[0, :, sl] = (rope(p[:, sl]) * scale).astype(BF16)
    p = proj(512)
    for c in range(4):
        sl = slice(c * LANES, (c + 1) * LANES)
        dk_ref[0, :, sl] = rope(p[:, sl]).astype(BF16)
    dv_ref[0] = proj(1024).astype(BF16)
    fq_ref[0] = (proj(1536) * scale).astype(BF16)
    fk_ref[0] = proj(2048).astype(BF16)
    fv_ref[0] = proj(2560).astype(BF16)
    z = jnp.dot(hb, w_ref[:, 3072:3072 + LANES], preferred_element_type=F32) + bf_ref[...]
    fl_ref[0] = -(jnp.maximum(-z, 0.0) + jnp.log1p(jnp.exp(-jnp.abs(z))))


def _inproj(x, sc, sh, g, w_pad, bf_pad, cosf, sinf):
    b, s, d = x.shape
    tm = min(512, s)
    act = lambda w: jax.ShapeDtypeStruct((b, s, w), BF16)
    row = lambda w: pl.BlockSpec((1, tm, w), lambda i, j: (i, j, 0))
    vec = pl.BlockSpec((1, 1, d), lambda i, j: (i, 0, 0))
    return pl.pallas_call(
        _inproj_body,
        grid=(b, s // tm),
        in_specs=[row(d), vec, vec,
                  pl.BlockSpec((1, d), lambda i, j: (0, 0)),
                  pl.BlockSpec(w_pad.shape, lambda i, j: (0, 0)),
                  pl.BlockSpec((1, LANES), lambda i, j: (0, 0)),
                  row(LANES), row(LANES)],
        out_specs=[row(512)] * 6 + [row(LANES)],
        out_shape=[act(512)] * 6 + [jax.ShapeDtypeStruct((b, s, LANES), F32)],
        compiler_params=_cparams(("arbitrary", "arbitrary")),
        name="in_proj",
    )(x, sc, sh, g, w_pad, bf_pad, cosf, sinf)


BIAS_LANES = 6


def _split3(c):
    hi = c.astype(BF16).astype(F32)
    r = c - hi
    mid = r.astype(BF16).astype(F32)
    return hi, mid, r - mid


def _cum_body(fl_ref, fq_ref, fk_ref):
    x = fl_ref[0]
    s = x.shape[0]
    row = lax.broadcasted_iota(I32, x.shape, 0)
    d = 1
    while d < s:
        x = x + jnp.where(row >= d, pltpu.roll(x, d, 0), 0.0)
        d *= 2
    pieces = jnp.concatenate(_split3(x * LOG2E), axis=1).astype(BF16)
    r = lax.broadcasted_iota(I32, (3 * LANES, LANES), 0)
    c = lax.broadcasted_iota(I32, (3 * LANES, LANES), 1)
    head, piece = r & (LANES - 1), r >> (LANES.bit_length() - 1)
    to_q = ((c == head * BIAS_LANES + piece) & (head < FOX_HEADS)).astype(BF16)
    to_k = ((c == head * BIAS_LANES + piece + 3) & (head < FOX_HEADS)).astype(BF16)
    lane = lax.broadcasted_iota(I32, (1, LANES), 1)
    ones_q = jnp.zeros((1, LANES), F32)
    ones_k = jnp.zeros((1, LANES), F32)
    for h in range(FOX_HEADS):
        ones_k = jnp.where((lane >= h * BIAS_LANES) & (lane < h * BIAS_LANES + 3), 1.0, ones_k)
        ones_q = jnp.where((lane >= h * BIAS_LANES + 3) & (lane < (h + 1) * BIAS_LANES), 1.0, ones_q)
    fq_ref[0] = jnp.dot(pieces, to_q, preferred_element_type=F32) + ones_q
    fk_ref[0] = ones_k - jnp.dot(pieces, to_k, preferred_element_type=F32)


def _forget_bias_lanes(flog):
    b, s, _ = flog.shape
    spec = pl.BlockSpec((1, s, LANES), lambda i: (i, 0, 0))
    out = jax.ShapeDtypeStruct((b, s, LANES), F32)
    return pl.pallas_call(
        _cum_body,
        grid=(b,),
        in_specs=[spec],
        out_specs=[spec, spec],
        out_shape=[out, out],
        compiler_params=_cparams(("arbitrary",)),
        name="forget_cumsum",
    )(flog)


def _causal_sweep(t, tq, tk, tile):
    ratio = tq // tk
    assert ratio in (1, 2) and ratio * tk == tq
    n_full = t * ratio

    def steps(k, base, trips, st):
        def body(i, st):
            for u in range(k):
                st = tile(base + k * i + u, st, False, 0)
            return st
        return lax.fori_loop(0, trips, body, st)

    state = tile(n_full, None, True, 0)
    state = steps(4, 0, n_full // 4, state)
    state = steps(2, (n_full // 4) * 4, (n_full // 2) % 2, state)
    if ratio == 1:
        state = steps(1, (n_full // 2) * 2, n_full % 2, state)
    return state, (tile(n_full + 1, state, True, tk) if ratio == 2 else None)


def _store_diagonal_mask(mask_ref, visible):
    r = lax.broadcasted_iota(I32, mask_ref.shape, 0)
    c = lax.broadcasted_iota(I32, mask_ref.shape, 1)
    mask_ref[...] = jnp.where(visible(r, c), 0.0, NEG_BIG)


def _finish_rows(state, lower, tk, finish):
    if lower is None:
        return finish(state)
    top = finish(jax.tree.map(lambda a: a[:tk], state))
    return jnp.concatenate([top, finish(lower)], axis=0)


def _diff_body(q_ref, k_ref, v_ref, lq1_ref, lk1_ref, lq2_ref, lk2_ref, g_ref, o_ref, mask_ref, *, tq, tk,
               lam_init):
    t = pl.program_id(2)
    q = q_ref[0]
    lane = lax.broadcasted_iota(I32, q.shape, 1)
    zero = jnp.zeros_like(q)
    qs = (jnp.where(lane < HEAD_DIM, q, zero), jnp.where(lane >= HEAD_DIM, q, zero))

    @pl.when(t == 0)
    def _():
        shift = CHUNK.bit_length() - 1
        _store_diagonal_mask(mask_ref, lambda r, c: (c >> shift) <= (r >> shift))

    def tile(j, state, masked, lo):
        off = pl.multiple_of(j * tk, tk)
        kt = k_ref[0, pl.ds(off, tk), :]
        vt = v_ref[0, pl.ds(off, tk), :]
        out = []
        for mp in range(2):
            s = lax.dot_general(qs[mp][lo:], kt, _NT, preferred_element_type=F32)
            if masked:
                s = s + mask_ref[:tq - lo]
            row_max = jnp.max(s, axis=1, keepdims=True)
            if state is None:
                m_new = jnp.broadcast_to(row_max, (s.shape[0], LANES))
            else:
                m, l, acc = state[mp]
                m_new = jnp.maximum(m[lo:], row_max)
                alpha = jnp.exp2(m[lo:] - m_new)
            p = jnp.exp2(s - m_new[:, :1])
            l_new = jnp.sum(p, axis=1, keepdims=True)
            acc_new = jnp.dot(p.astype(BF16), vt, preferred_element_type=F32)
            if state is None:
                l_new = jnp.broadcast_to(l_new, m_new.shape)
            else:
                l_new = alpha * l[lo:] + l_new
                acc_new = alpha * acc[lo:] + acc_new
            out.append((m_new, l_new, acc_new))
        return tuple(out)

    lam = (jnp.exp(jnp.sum(lq1_ref[...] * lk1_ref[...], axis=1, keepdims=True))
           - jnp.exp(jnp.sum(lq2_ref[...] * lk2_ref[...], axis=1, keepdims=True)) + lam_init)

    def finish(st):
        (_, l0, a0), (_, l1, a1) = st
        out = a0 / l0 - lam * (a1 / l1)
        return (_rms(out) * g_ref[...] * (1.0 - lam_init)).astype(o_ref.dtype)

    o_ref[0] = _finish_rows(*_causal_sweep(t, tq, tk, tile), tk, finish)


def _diff_attention(dq, dk, dv, lq1, lk1, lq2, lk2, subln, lam_init):
    b, s, _ = dq.shape
    tq, tk = _attn_tiles(s)
    qspec = pl.BlockSpec((1, tq, LANES), lambda i, h, t: (i, t, h))
    kvspec = pl.BlockSpec((1, s, LANES), lambda i, h, t: (i, 0, h))
    lspec = pl.BlockSpec((1, HEAD_DIM), lambda i, h, t: (0, 0))
    return pl.pallas_call(
        functools.partial(_diff_body, tq=tq, tk=tk, lam_init=lam_init),
        grid=(b, DIFF_HEADS, s // tq),
        in_specs=[qspec, kvspec, kvspec, lspec, lspec, lspec, lspec,
                  pl.BlockSpec((1, LANES), lambda i, h, t: (0, 0))],
        out_specs=qspec,
        out_shape=jax.ShapeDtypeStruct((b, s, D_DIFF), BF16),
        scratch_shapes=[pltpu.VMEM((tq, tk), F32)],
        compiler_params=_cparams(("arbitrary", "arbitrary", "arbitrary")),
        name="diff_attention",
    )(dq, dk, dv, lq1, lk1, lq2, lk2, subln)


def _fox_operand(x_f32, bias, hh, h, lane):
    xh = x_f32 if hh == 0 else pltpu.roll(x_f32, HEAD_DIM, 1)
    bh = pltpu.roll(bias, HEAD_DIM - BIAS_LANES * h, 1)
    return jnp.where(lane < HEAD_DIM, xh, jnp.where(lane < HEAD_DIM + BIAS_LANES, bh, 0.0)).astype(BF16)


def _fox_body(q_ref, k_ref, v_ref, cq_ref, ck_ref, o_ref, ka_ref, va_ref, mask_ref, *, tq, tk):
    pair = pl.program_id(1)
    t = pl.program_id(2)
    heads = (2 * pair, 2 * pair + 1)
    lane = lax.broadcasted_iota(I32, (tq, LANES), 1)
    klane = lax.broadcasted_iota(I32, (tk, LANES), 1)

    @pl.when(t == 0)
    def _():
        _store_diagonal_mask(mask_ref, lambda r, c: c <= r)

        def chunk(ci, carry):
            rows = pl.ds(pl.multiple_of(ci * tk, tk), tk)
            kf = k_ref[0, rows, :].astype(F32)
            vf = v_ref[0, rows, :].astype(F32)
            bias = ck_ref[0, rows, :]
            for hh in range(2):
                ka_ref[hh, rows, :] = _fox_operand(kf, bias, hh, heads[hh], klane)
                vh = vf if hh == 0 else pltpu.roll(vf, HEAD_DIM, 1)
                va_ref[hh, rows, :] = jnp.where(klane < HEAD_DIM, vh, 1.0).astype(BF16)
            return carry

        lax.fori_loop(0, k_ref.shape[1] // tk, chunk, 0)

    qf = q_ref[0].astype(F32)
    qs = [_fox_operand(qf, cq_ref[0], hh, heads[hh], lane) for hh in range(2)]

    def tile(j, state, masked, lo):
        off = pl.multiple_of(j * tk, tk)
        out = []
        for hh in range(2):
            s = lax.dot_general(qs[hh][lo:], ka_ref[hh, pl.ds(off, tk), :], _NT, preferred_element_type=F32)
            if masked:
                s = s + mask_ref[:tq - lo]
            row_max = jnp.max(s, axis=1, keepdims=True)
            if state is None:
                m_new = jnp.broadcast_to(row_max, (s.shape[0], LANES))
            else:
                m, acc = state[hh]
                m_new = jnp.maximum(m[lo:], row_max)
            p = jnp.exp2(s - m_new[:, :1]).astype(BF16)
            acc_new = jnp.dot(p, va_ref[hh, pl.ds(off, tk), :], preferred_element_type=F32)
            if state is not None:
                acc_new = jnp.exp2(m[lo:] - m_new) * acc[lo:] + acc_new
            out.append((m_new, acc_new))
        return tuple(out)

    def finish(st):
        (_, a0), (_, a1) = st
        o0 = a0 / pltpu.roll(a0, HEAD_DIM, 1)
        o1 = a1 / pltpu.roll(a1, HEAD_DIM, 1)
        first = lax.broadcasted_iota(I32, a0.shape, 1) < HEAD_DIM
        return jnp.where(first, o0, pltpu.roll(o1, HEAD_DIM, 1)).astype(o_ref.dtype)

    o_ref[0] = _finish_rows(*_causal_sweep(t, tq, tk, tile), tk, finish)


def _fox_attention(fq, fk, fv, bias_q, bias_k):
    b, s, _ = fq.shape
    tq, tk = _attn_tiles(s)
    qspec = pl.BlockSpec((1, tq, LANES), lambda i, p, t: (i, t, p))
    kvspec = pl.BlockSpec((1, s, LANES), lambda i, p, t: (i, 0, p))
    return pl.pallas_call(
        functools.partial(_fox_body, tq=tq, tk=tk),
        grid=(b, FOX_HEADS // 2, s // tq),
        in_specs=[qspec, kvspec, kvspec,
                  pl.BlockSpec((1, tq, LANES), lambda i, p, t: (i, t, 0)),
                  pl.BlockSpec((1, s, LANES), lambda i, p, t: (i, 0, 0))],
        out_specs=qspec,
        out_shape=jax.ShapeDtypeStruct((b, s, D_FOX), BF16),
        scratch_shapes=[pltpu.VMEM((2, s, LANES), BF16), pltpu.VMEM((2, s, LANES), BF16),
                        pltpu.VMEM((tq, tk), F32)],
        compiler_params=_cparams(("arbitrary", "arbitrary", "arbitrary")),
        name="fox_attention",
    )(fq, fk, fv, bias_q, bias_k)


def _outproj_body(x_ref, a_ref, b_ref, w_ref, g1_ref, sc_ref, sh_ref, ng_ref, rw_ref, rb_ref,
                  x1_ref, h2_ref, lg_ref):
    mix = (jnp.dot(a_ref[0], w_ref[:D_DIFF, :], preferred_element_type=F32)
           + jnp.dot(b_ref[0], w_ref[D_DIFF:, :], preferred_element_type=F32))
    x1 = x_ref[0] + g1_ref[0] * mix
    x1_ref[0] = x1
    h2 = _rms(x1) * ng_ref[...] * (1.0 + sc_ref[0]) + sh_ref[0]
    h_hi = h2.astype(BF16)
    h2_ref[...] = h_hi
    h_lo = (h2 - h_hi.astype(F32)).astype(BF16)
    rw = rw_ref[...]
    w_hi = rw.astype(BF16)
    w_lo = (rw - w_hi.astype(F32)).astype(BF16)
    nt = lambda a, b: lax.dot_general(a, b, _NT, preferred_element_type=F32)
    lg_ref[...] = nt(w_hi, h_hi) + (nt(w_hi, h_lo) + nt(w_lo, h_hi)) + rb_ref[...]


def _outproj(x, a_out, b_out, w_out, g1, sc2, sh2, ng, rwt, rb):
    b, s, d = x.shape
    n = b * s
    tm = min(512, s)
    nt = s // tm
    row = lambda w: pl.BlockSpec((1, tm, w), lambda i, j: (i, j, 0))
    vec = pl.BlockSpec((1, 1, d), lambda i, j: (i, 0, 0))
    const = lambda shp: pl.BlockSpec(shp, lambda i, j: (0,) * len(shp))
    return pl.pallas_call(
        _outproj_body,
        grid=(b, nt),
        in_specs=[row(d), row(D_DIFF), row(D_FOX), const(w_out.shape), vec, vec, vec,
                  const((1, d)), const(rwt.shape), const(rb.shape)],
        out_specs=[row(d),
                   pl.BlockSpec((tm, d), lambda i, j: (i * nt + j, 0)),
                   pl.BlockSpec((N_EXPERTS, tm), lambda i, j: (0, i * nt + j))],
        out_shape=[jax.ShapeDtypeStruct((b, s, d), F32),
                   jax.ShapeDtypeStruct((n, d), BF16),
                   jax.ShapeDtypeStruct((N_EXPERTS, n), F32)],
        compiler_params=_cparams(("arbitrary", "arbitrary")),
        name="out_proj",
    )(x, a_out, b_out, w_out, g1, sc2, sh2, ng, rwt, rb)


def _strict_lower(n):
    r = lax.broadcasted_iota(I32, (n, n), 0)
    c = lax.broadcasted_iota(I32, (n, n), 1)
    return (c < r).astype(F32)


def _align_up(x_f32, m):
    shift = m.bit_length() - 1
    return (((x_f32.astype(I32) + (m - 1)) >> shift) << shift).astype(F32)


def _route_body(lg_ref, pos_ref, gate_ref, cnt_ref):
    i = pl.program_id(0)

    @pl.when(i == 0)
    def _():
        cnt_ref[...] = jnp.zeros_like(cnt_ref)

    work = lg_ref[...]
    td = work.shape[1]
    eidx = lax.broadcasted_iota(I32, work.shape, 0)
    vals, hots = [], []
    for k in range(TOP_K):
        m = jnp.max(work, axis=0, keepdims=True)
        sel = jnp.min(jnp.where(work == m, eidx, N_EXPERTS), axis=0, keepdims=True)
        hot = eidx == sel
        vals.append(m)
        hots.append(hot)
        work = jnp.where(hot, -jnp.inf, work)
    ex = [jnp.exp(v - vals[0]) for v in vals]
    den = ex[0] + ex[1] + ex[2] + ex[3]
    for k in range(TOP_K):
        gate_ref[k:k + 1, :] = ex[k] / den

    chosen = hots[0] | hots[1] | hots[2] | hots[3]
    r = lax.broadcasted_iota(I32, (td, td), 0)
    c = lax.broadcasted_iota(I32, (td, td), 1)
    before = (r < c).astype(BF16)
    earlier = jnp.dot(chosen.astype(BF16), before, preferred_element_type=F32)
    cnt = jnp.sum(chosen.astype(F32), axis=1, keepdims=True)
    run = _align_up(cnt, RUN_ALIGN)
    start = jnp.dot(_strict_lower(N_EXPERTS), jnp.broadcast_to(run, (N_EXPERTS, LANES)),
                    preferred_element_type=F32, precision=HIGHEST)[:, 0:1]
    slot = start + earlier
    for k in range(TOP_K):
        pos_ref[k:k + 1, :] = jnp.sum(jnp.where(hots[k], slot, 0.0), axis=0, keepdims=True).astype(I32)
    lane = lax.broadcasted_iota(I32, cnt_ref.shape, 1)
    cnt_ref[...] = jnp.where(lane == i, run, cnt_ref[...])


def _route(logits_t):
    e, n = logits_t.shape
    td = min(ROUTE_TILE, n)
    ntp = _round_up(n // td, LANES)
    tok = pl.BlockSpec((TOP_K, td), lambda i: (0, i))
    return pl.pallas_call(
        _route_body,
        grid=(n // td,),
        in_specs=[pl.BlockSpec((e, td), lambda i: (0, i))],
        out_specs=[tok, tok, pl.BlockSpec((e, ntp), lambda i: (0, 0))],
        out_shape=[jax.ShapeDtypeStruct((TOP_K, n), I32),
                   jax.ShapeDtypeStruct((TOP_K, n), F32),
                   jax.ShapeDtypeStruct((e, ntp), F32)],
        compiler_params=_cparams(("arbitrary",)),
        name="route_topk",
    )(logits_t)


def _layout_body(run_ref, base_ref, be_ref, nu_ref, tail_ref):
    run = run_ref[...]
    ntp = run.shape[1]
    total = jnp.sum(run, axis=1, keepdims=True)
    region = _align_up(total, EXPERT_TILE)
    pstart = jnp.dot(_strict_lower(N_EXPERTS), jnp.broadcast_to(region, (N_EXPERTS, LANES)),
                     preferred_element_type=F32, precision=HIGHEST)[:, 0:1]
    ti = lax.broadcasted_iota(I32, (ntp, ntp), 0)
    tj = lax.broadcasted_iota(I32, (ntp, ntp), 1)
    within = jnp.dot(run, (ti < tj).astype(F32), preferred_element_type=F32, precision=HIGHEST)
    base_ref[...] = (pstart + within).astype(I32)
    pend = pstart + region
    blk = (lax.broadcasted_iota(I32, (N_EXPERTS, be_ref.shape[1]), 1) * EXPERT_TILE).astype(F32)
    be = jnp.sum((pend <= blk).astype(I32), axis=0, keepdims=True)
    be_ref[...] = jnp.minimum(be, N_EXPERTS - 1)
    used = jnp.sum(region, axis=0, keepdims=True).astype(I32) >> (EXPERT_TILE.bit_length() - 1)
    nu_ref[...] = jnp.broadcast_to(used, nu_ref.shape)
    lane = lax.broadcasted_iota(I32, tail_ref.shape, 1)
    tail_ref[...] = jnp.where(lane == 0, pstart + total, jnp.where(lane == 1, region - total, 0.0)).astype(I32)


def _layout(run_t, n_blocks):
    e, ntp = run_t.shape
    nbp = _round_up(n_blocks, LANES)
    full = lambda shp: pl.BlockSpec(shp, lambda: (0,) * len(shp))
    return pl.pallas_call(
        _layout_body,
        in_specs=[full((e, ntp))],
        out_specs=[full((e, ntp)), full((1, nbp)), full((1, LANES)), full((e, LANES))],
        out_shape=[jax.ShapeDtypeStruct((e, ntp), I32),
                   jax.ShapeDtypeStruct((1, nbp), I32),
                   jax.ShapeDtypeStruct((1, LANES), I32),
                   jax.ShapeDtypeStruct((e, LANES), I32)],
        name="expert_layout",
    )(run_t)


_RUN_LEVELS = tuple(1 << b for b in range(ROUTE_TILE.bit_length() - 1, RUN_ALIGN.bit_length() - 2, -1))


def _for_each_run(run_tbl, base_tbl, tile, make_copy, start):
    def body(e, off):
        cnt = run_tbl[tile * N_EXPERTS + e]
        base = base_tbl[tile * N_EXPERTS + e]
        done = jnp.int32(0)
        for lvl in _RUN_LEVELS:
            bit = cnt & lvl

            @pl.when(bit != 0)
            def _(done=done, lvl=lvl):
                cp = make_copy(pl.multiple_of(off + done, RUN_ALIGN), pl.multiple_of(base + done, RUN_ALIGN), lvl)
                if start:
                    cp.start()
                else:
                    cp.wait()
            done = done + bit
        return off + cnt

    lax.fori_loop(0, N_EXPERTS, body, jnp.int32(0))


def _wait_for_tile(run_tbl, tile, make_copy, sb):
    total = lax.fori_loop(0, N_EXPERTS, lambda e, acc: acc + run_tbl[tile * N_EXPERTS + e], jnp.int32(0))
    lvl = 1 << (sb.bit_length() - 1)
    while lvl >= RUN_ALIGN:
        @pl.when((total & lvl) != 0)
        def _(lvl=lvl):
            make_copy(0, 0, lvl).wait()
        lvl //= 2


def _slot_onehot(pos, sb):
    j = lax.broadcasted_iota(I32, (sb, pos.shape[1]), 0)
    hits = [j == pos[k:k + 1, :] for k in range(TOP_K)]
    return hits, (hits[0] | hits[1] | hits[2] | hits[3])


def _sorted_rows(td):
    return _round_up(TOP_K * td + N_EXPERTS * (RUN_ALIGN - 1), LANES)


def _dispatch_body(run_tbl, base_tbl, tail_len, tail_start, h_ref, pos_ref, gate_ref, xbuf_hbm,
                   sorted_ref, zero_ref, sem, zsem, *, d):
    i = pl.program_id(0)
    last = pl.num_programs(0) - 1
    slot = i % 2
    sb = sorted_ref.shape[1]

    def zero_copies(off, base, rows):
        del off
        return pltpu.make_async_copy(zero_ref.at[pl.ds(0, rows)], xbuf_hbm.at[pl.ds(base, rows)], zsem)

    @pl.when(i == 0)
    def _():
        zero_ref[...] = jnp.zeros_like(zero_ref)
        _for_each_run(tail_len, tail_start, 0, zero_copies, True)

    hits, any_hit = _slot_onehot(pos_ref[...], sb)
    perm = jnp.where(any_hit, 1.0, 0.0).astype(BF16)
    sorted_ref[slot, :, :d] = jnp.dot(perm, h_ref[...], preferred_element_type=F32)
    gates = gate_ref[...]
    gsel = jnp.where(hits[0], gates[0:1, :], 0.0)
    for k in range(1, TOP_K):
        gsel = gsel + jnp.where(hits[k], gates[k:k + 1, :], 0.0)
    sorted_ref[slot, :, d:] = jnp.broadcast_to(jnp.sum(gsel, axis=1, keepdims=True), (sb, LANES))

    def copies(slot):
        return lambda off, base, rows: pltpu.make_async_copy(
            sorted_ref.at[slot, pl.ds(off, rows)], xbuf_hbm.at[pl.ds(base, rows)], sem.at[slot])

    _for_each_run(run_tbl, base_tbl, i, copies(slot), True)

    @pl.when(i > 0)
    def _():
        _wait_for_tile(run_tbl, i - 1, copies(1 - slot), sb)

    @pl.when(i == last)
    def _():
        _wait_for_tile(run_tbl, i, copies(slot), sb)
        _for_each_run(tail_len, tail_start, 0, zero_copies, False)


def _dispatch(run_tbl, base_tbl, tail_len, tail_start, h2, pos_t, gate_t, rows):
    n, d = h2.shape
    td = min(ROUTE_TILE, n)
    sb = _sorted_rows(td)
    dx = d + LANES
    assert EXPERT_TILE <= 2 * _RUN_LEVELS[0]
    tok = pl.BlockSpec((TOP_K, td), lambda i, *_: (0, i))
    grid_spec = pltpu.PrefetchScalarGridSpec(
        num_scalar_prefetch=4,
        grid=(n // td,),
        in_specs=[pl.BlockSpec((td, d), lambda i, *_: (i, 0)), tok, tok],
        out_specs=pl.BlockSpec(memory_space=pl.ANY),
        scratch_shapes=[pltpu.VMEM((2, sb, dx), F32), pltpu.VMEM((_RUN_LEVELS[0], dx), F32),
                        pltpu.SemaphoreType.DMA((2,)), pltpu.SemaphoreType.DMA],
    )
    return pl.pallas_call(
        functools.partial(_dispatch_body, d=d),
        grid_spec=grid_spec,
        out_shape=jax.ShapeDtypeStruct((rows, dx), F32),
        compiler_params=_cparams(("arbitrary",)),
        name="dispatch_rows",
    )(run_tbl, base_tbl, tail_len, tail_start, h2, pos_t, gate_t)


def _expert_body(be_ref, nu_ref, x_ref, w1_hbm, b1_ref, w2_hbm, b2_ref, y_ref,
                 w1f_ref, w2f_ref, w1b_ref, w2b_ref, sem, *, layer):
    i = pl.program_id(0)
    n = pl.num_programs(0)
    d_ff, d = w2b_ref.shape

    def fetch(blk):
        e = layer * N_EXPERTS + be_ref[blk]
        return (pltpu.make_async_copy(w1_hbm.at[e], w1f_ref, sem.at[0]),
                pltpu.make_async_copy(w2_hbm.at[e], w2f_ref, sem.at[1]))

    @pl.when(i == 0)
    def _():
        for cp in fetch(0):
            cp.start()

    @pl.when((i == 0) | (be_ref[i] != be_ref[jnp.maximum(i - 1, 0)]))
    def _():
        for cp in fetch(i):
            cp.wait()
        w1b_ref[...] = w1f_ref[...].astype(BF16)
        w2b_ref[...] = w2f_ref[...].astype(BF16)
        nxt = lax.while_loop(lambda j: (j < n) & (be_ref[jnp.minimum(j, n - 1)] == be_ref[i]),
                             lambda j: j + 1, i + 1)

        @pl.when(nxt < n)
        def _():
            for cp in fetch(nxt):
                cp.start()

    @pl.when(i < nu_ref[0])
    def _():
        gu = jnp.dot(x_ref[:, :d].astype(BF16), w1b_ref[...], preferred_element_type=F32) + b1_ref[0]
        gate = jnp.minimum(gu[:, :d_ff], SWIGLU_LIMIT)
        up = jnp.clip(gu[:, d_ff:], -SWIGLU_LIMIT, SWIGLU_LIMIT)
        glu = gate * jax.nn.sigmoid(gate * SWIGLU_ALPHA)
        act = ((up + 1.0) * glu).astype(BF16)
        y = jnp.dot(act, w2b_ref[...], preferred_element_type=F32) + b2_ref[0]
        y_ref[...] = x_ref[:, d:d + 1] * y

    @pl.when(i >= nu_ref[0])
    def _():
        y_ref[...] = jnp.zeros_like(y_ref)


def _experts(layer, block_expert, n_used, xbuf, w1, b1, w2, b2):
    rows, dx = xbuf.shape
    de, d, f2 = w1.shape
    f = w2.shape[1]
    blk = lambda i, be, nu: (jnp.maximum(jnp.minimum(i, nu[0] - 1), 0), 0)
    wsel = lambda i, be, nu: (layer * N_EXPERTS + be[i], 0, 0)
    grid_spec = pltpu.PrefetchScalarGridSpec(
        num_scalar_prefetch=2,
        grid=(rows // EXPERT_TILE,),
        in_specs=[pl.BlockSpec((EXPERT_TILE, dx), blk),
                  pl.BlockSpec(memory_space=pl.ANY),
                  pl.BlockSpec((1, 1, f2), wsel),
                  pl.BlockSpec(memory_space=pl.ANY),
                  pl.BlockSpec((1, 1, d), wsel)],
        out_specs=pl.BlockSpec((EXPERT_TILE, d), lambda i, be, nu: (i, 0)),
        scratch_shapes=[pltpu.VMEM((d, f2), F32), pltpu.VMEM((f, d), F32),
                        pltpu.VMEM((d, f2), BF16), pltpu.VMEM((f, d), BF16),
                        pltpu.SemaphoreType.DMA((2,))],
    )
    return pl.pallas_call(
        functools.partial(_expert_body, layer=layer),
        grid_spec=grid_spec,
        out_shape=jax.ShapeDtypeStruct((rows, d), F32),
        compiler_params=_cparams(("arbitrary",)),
        name="expert_swiglu",
    )(block_expert, n_used, xbuf, w1, b1.reshape(de, 1, f2), w2, b2.reshape(de, 1, d))


def _combine_body(run_tbl, base_tbl, x1_ref, g2_ref, pos_ref, fg_ref, ybuf_hbm, o_ref, ys_ref, sem, *, final):
    i = pl.program_id(0)
    last = pl.num_programs(0) - 1
    slot = i % 2
    sb = ys_ref.shape[1]

    def copies(slot):
        return lambda off, base, rows: pltpu.make_async_copy(
            ybuf_hbm.at[pl.ds(base, rows)], ys_ref.at[slot, pl.ds(off, rows)], sem.at[slot])

    @pl.when(i == 0)
    def _():
        ys_ref[...] = jnp.zeros_like(ys_ref)
        _for_each_run(run_tbl, base_tbl, 0, copies(0), True)

    @pl.when(i < last)
    def _():
        _for_each_run(run_tbl, base_tbl, i + 1, copies(1 - slot), True)

    _wait_for_tile(run_tbl, i, copies(slot), sb)

    _, any_hit = _slot_onehot(pos_ref[...], sb)
    perm = jnp.where(any_hit, 1.0, 0.0).astype(BF16)
    moe = lax.dot_general(perm, ys_ref[slot].astype(BF16), _TN, preferred_element_type=F32)
    out = x1_ref[...] + g2_ref[0] * moe
    if final:
        out = _rms(out) * fg_ref[...]
    o_ref[...] = out


def _combine(run_tbl, base_tbl, x1, g2, pos_t, final_g, ybuf, seq, final):
    n, d = x1.shape
    td = min(ROUTE_TILE, n)
    sb = _sorted_rows(td)
    grid_spec = pltpu.PrefetchScalarGridSpec(
        num_scalar_prefetch=2,
        grid=(n // td,),
        in_specs=[pl.BlockSpec((td, d), lambda i, *_: (i, 0)),
                  pl.BlockSpec((1, 1, d), lambda i, *_: ((i * td) // seq, 0, 0)),
                  pl.BlockSpec((TOP_K, td), lambda i, *_: (0, i)),
                  pl.BlockSpec((1, d), lambda i, *_: (0, 0)),
                  pl.BlockSpec(memory_space=pl.ANY)],
        out_specs=pl.BlockSpec((td, d), lambda i, *_: (i, 0)),
        scratch_shapes=[pltpu.VMEM((2, sb, d), F32), pltpu.SemaphoreType.DMA((2,))],
    )
    return pl.pallas_call(
        functools.partial(_combine_body, final=final),
        grid_spec=grid_spec,
        out_shape=jax.ShapeDtypeStruct((n, d), F32),
        compiler_params=_cparams(("arbitrary",)),
        name="combine_rows",
    )(run_tbl, base_tbl, x1, g2, pos_t, final_g, ybuf)


def _moe(layer, x1, h2, logits_t, g2, w1, b1, w2, b2, final_g, seq, final):
    n, d = h2.shape
    td = min(ROUTE_TILE, n)
    n_tiles = n // td
    assert seq % td == 0
    rows = (_round_up(n * TOP_K + n_tiles * N_EXPERTS * (RUN_ALIGN - 1), EXPERT_TILE)
            + N_EXPERTS * EXPERT_TILE)
    n_blocks = rows // EXPERT_TILE
    pos_t, gate_t, run_t = _route(logits_t)
    base_t, be, nu, tail = _layout(run_t, n_blocks)
    run_tbl = run_t[:, :n_tiles].T.astype(I32).reshape(-1)
    base_tbl = base_t[:, :n_tiles].T.reshape(-1)
    xbuf = _dispatch(run_tbl, base_tbl, tail[:, 1], tail[:, 0], h2, pos_t, gate_t, rows)
    ybuf = _experts(layer, be[0, :n_blocks], nu[0, :1], xbuf, w1, b1, w2, b2)
    return _combine(run_tbl, base_tbl, x1, g2, pos_t, final_g, ybuf, seq, final)


def kernel(x, c, positions, ada_w, ada_b, norm_mix, norm_ffn, w_in, b_forget, lambda_q1, lambda_k1,
           lambda_q2, lambda_k2, diff_subln, w_out, router_w, router_b, exp_w1, exp_b1, exp_w2, exp_b2,
           final_norm):
    b, s, d = x.shape
    depth = ada_w.shape[0]
    mod = _ada_mod(c, ada_w, ada_b)
    cosf, sinf = _rope_tables(positions)
    n_in = w_in.shape[2]
    pad_cols = 3072 + LANES - n_in
    flat = lambda a: a.reshape((a.shape[0] * a.shape[1],) + a.shape[2:])
    w1_all, b1_all, w2_all, b2_all = flat(exp_w1), flat(exp_b1), flat(exp_w2), flat(exp_b2)
    for l in range(depth):
        lam_init = 0.8 - 0.6 * math.exp(-0.3 * l)
        sh1, sc1, g1, sh2, sc2, g2 = [mod[l, :, i * d:(i + 1) * d].reshape(b, 1, d) for i in range(6)]
        w_pad = jnp.pad(w_in[l], ((0, 0), (0, pad_cols))).astype(BF16)
        bf_pad = jnp.pad(b_forget[l], (0, LANES - FOX_HEADS)).reshape(1, LANES)
        dq, dk, dv, fq, fk, fv, flog = _inproj(x, sc1, sh1, norm_mix[l].reshape(1, d), w_pad, bf_pad,
                                               cosf, sinf)
        bias_q, bias_k = _forget_bias_lanes(flog)
        a_out = _diff_attention(dq, dk, dv, lambda_q1[l].reshape(1, -1), lambda_k1[l].reshape(1, -1),
                                lambda_q2[l].reshape(1, -1), lambda_k2[l].reshape(1, -1),
                                diff_subln[l].reshape(1, -1), lam_init)
        b_out = _fox_attention(fq, fk, fv, bias_q, bias_k)
        x1, h2, logits_t = _outproj(x, a_out, b_out, w_out[l].astype(BF16), g1, sc2, sh2,
                                    norm_ffn[l].reshape(1, d), router_w[l].T,
                                    router_b[l].reshape(-1, 1))
        x = _moe(l, x1.reshape(b * s, d), h2, logits_t, g2, w1_all, b1_all, w2_all, b2_all,
                 final_norm.reshape(1, d), s, l == depth - 1).reshape(b, s, d)
    return x
```

```python
import functools
import math

import jax
import jax.numpy as jnp
from jax import lax
from jax.experimental import pallas as pl
from jax.experimental.pallas import tpu as pltpu

F32 = jnp.float32
BF16 = jnp.bfloat16
I32 = jnp.int32
HIGHEST = lax.Precision.HIGHEST

HEAD_DIM = 64
LANES = 128
SUBLANES = 8
CHUNK = 64
DIFF_HEADS = 4
FOX_HEADS = 8
D_DIFF = DIFF_HEADS * 2 * HEAD_DIM
D_FOX = FOX_HEADS * HEAD_DIM
ROPE_THETA = 500000.0
ROPE_DIM = HEAD_DIM // 4
ROPE_HALF = ROPE_DIM // 2
N_EXPERTS = 32
TOP_K = 4
SWIGLU_LIMIT = 7.0
SWIGLU_ALPHA = 1.702
NORM_EPS = 1e-6
NEG_BIG = -1e30
LOG2E = math.log2(math.e)
ATTN_Q_TILE = 1024
ATTN_K_TILE = 512
EXPERT_TILE = 512
ROUTE_TILE = 256
RUN_ALIGN = SUBLANES
VMEM_LIMIT = 56 * 1024 * 1024

_NT = (((1,), (1,)), ((), ()))
_TN = (((0,), (0,)), ((), ()))


def _cparams(sem, vmem=None):
    return pltpu.CompilerParams(dimension_semantics=sem, vmem_limit_bytes=vmem or VMEM_LIMIT)


def _rms(x):
    return x * lax.rsqrt(jnp.mean(x * x, axis=-1, keepdims=True) + NORM_EPS)


def _round_up(x, m):
    return (x + m - 1) // m * m


def _attn_tiles(s):
    tq = min(ATTN_Q_TILE, s)
    return tq, min(ATTN_K_TILE, tq)


def _ada_body(c_ref, w_ref, b_ref, o_ref):
    c = c_ref[...]
    ca = c * jax.nn.sigmoid(c)
    o_ref[0] = jnp.dot(ca, w_ref[0], preferred_element_type=F32, precision=HIGHEST) + b_ref[0]


def _ada_mod(c, ada_w, ada_b):
    depth, d, n6 = ada_w.shape
    b = c.shape[0]
    tn = 1536
    return pl.pallas_call(
        _ada_body,
        grid=(depth, n6 // tn),
        in_specs=[pl.BlockSpec((b, d), lambda l, j: (0, 0)),
                  pl.BlockSpec((1, d, tn), lambda l, j: (l, 0, j)),
                  pl.BlockSpec((1, 1, tn), lambda l, j: (l, 0, j))],
        out_specs=pl.BlockSpec((1, b, tn), lambda l, j: (l, 0, j)),
        out_shape=jax.ShapeDtypeStruct((depth, b, n6), F32),
        compiler_params=_cparams(("arbitrary", "arbitrary")),
        name="ada_mod",
    )(c, ada_w, ada_b.reshape(depth, 1, n6))


def _rope_body(pos_ref, invf_ref, cos_ref, sin_ref):
    ang = pos_ref[0].astype(F32) * invf_ref[...]
    j = lax.broadcasted_iota(I32, ang.shape, 1) & (HEAD_DIM - 1)
    c = jnp.cos(ang)
    s = jnp.sin(ang)
    cos_ref[0] = jnp.where(j < ROPE_DIM, c, 1.0)
    sin_ref[0] = jnp.where(j < ROPE_HALF, -s, jnp.where(j < ROPE_DIM, s, 0.0))


def _rope_tables(positions):
    b, s = positions.shape
    ts = min(512, s)
    inv_freq = ROPE_THETA ** (-jnp.arange(0, ROPE_DIM, 2, dtype=F32) / ROPE_DIM)
    invf = inv_freq[jnp.arange(LANES) % ROPE_HALF].reshape(1, LANES)
    out = jax.ShapeDtypeStruct((b, s, LANES), F32)
    return pl.pallas_call(
        _rope_body,
        grid=(b, s // ts),
        in_specs=[pl.BlockSpec((1, ts, 1), lambda i, j: (i, j, 0)),
                  pl.BlockSpec((1, LANES), lambda i, j: (0, 0))],
        out_specs=[pl.BlockSpec((1, ts, LANES), lambda i, j: (i, j, 0))] * 2,
        out_shape=[out, out],
        compiler_params=_cparams(("arbitrary", "arbitrary")),
        name="rope_tables",
    )(positions.reshape(b, s, 1), invf)


def _inproj_body(x_ref, sc_ref, sh_ref, g_ref, w_ref, bf_ref, cos_ref, sin_ref,
                 dq_ref, dk_ref, dv_ref, fq_ref, fk_ref, fv_ref, fl_ref):
    h = _rms(x_ref[0]) * g_ref[...] * (1.0 + sc_ref[0]) + sh_ref[0]
    hb = h.astype(BF16)
    cosf = cos_ref[0]
    sinf = sin_ref[0]
    lane = lax.broadcasted_iota(I32, cosf.shape, 1)
    first = (lane & (HEAD_DIM - 1)) < ROPE_HALF

    def rope(p):
        nxt = pltpu.roll(p, LANES - ROPE_HALF, 1)
        prv = pltpu.roll(p, ROPE_HALF, 1)
        return p * cosf + jnp.where(first, nxt, prv) * sinf

    def proj(c0):
        return jnp.dot(hb, w_ref[:, c0:c0 + 512], preferred_element_type=F32)

    scale = HEAD_DIM ** -0.5 * LOG2E
    p = proj(0)
    for c in range(4):
        sl = slice(c * LANES, (c + 1) * LANES)
        dq_ref[0, :, sl] = (rope(p[:, sl]) * scale).astype(BF16)
    p = proj(512)
    for c in range(4):
        sl = slice(c * LANES, (c + 1) * LANES)
        dk_ref[0, :, sl] = rope(p[:, sl]).astype(BF16)
    dv_ref[0] = proj(1024).astype(BF16)
    fq_ref[0] = (proj(1536) * scale).astype(BF16)
    fk_ref[0] = proj(2048).astype(BF16)
    fv_ref[0] = proj(2560).astype(BF16)
    z = jnp.dot(hb, w_ref[:, 3072:3072 + LANES], preferred_element_type=F32) + bf_ref[...]
    fl_ref[0] = -(jnp.maximum(-z, 0.0) + jnp.log1p(jnp.exp(-jnp.abs(z))))


def _inproj(x, sc, sh, g, w_pad, bf_pad, cosf, sinf):
    b, s, d = x.shape
    tm = min(512, s)
    act = lambda w: jax.ShapeDtypeStruct((b, s, w), BF16)
    row = lambda w: pl.BlockSpec((1, tm, w), lambda i, j: (i, j, 0))
    vec = pl.BlockSpec((1, 1, d), lambda i, j: (i, 0, 0))
    return pl.pallas_call(
        _inproj_body,
        grid=(b, s // tm),
        in_specs=[row(d), vec, vec,
                  pl.BlockSpec((1, d), lambda i, j: (0, 0)),
                  pl.BlockSpec(w_pad.shape, lambda i, j: (0, 0)),
                  pl.BlockSpec((1, LANES), lambda i, j: (0, 0)),
                  row(LANES), row(LANES)],
        out_specs=[row(512)] * 6 + [row(LANES)],
        out_shape=[act(512)] * 6 + [jax.ShapeDtypeStruct((b, s, LANES), F32)],
        compiler_params=_cparams(("arbitrary", "arbitrary")),
        name="in_proj",
    )(x, sc, sh, g, w_pad, bf_pad, cosf, sinf)


BIAS_LANES = 6


def _split3(c):
    hi = c.astype(BF16).astype(F32)
    r = c - hi
    mid = r.astype(BF16).astype(F32)
    return hi, mid, r - mid


def _cum_body(fl_ref, fq_ref, fk_ref):
    x = fl_ref[0]
    s = x.shape[0]
    row = lax.broadcasted_iota(I32, x.shape, 0)
    d = 1
    while d < s:
        x = x + jnp.where(row >= d, pltpu.roll(x, d, 0), 0.0)
        d *= 2
    pieces = jnp.concatenate(_split3(x * LOG2E), axis=1).astype(BF16)
    r = lax.broadcasted_iota(I32, (3 * LANES, LANES), 0)
    c = lax.broadcasted_iota(I32, (3 * LANES, LANES), 1)
    head, piece = r & (LANES - 1), r >> (LANES.bit_length() - 1)
    to_q = ((c == head * BIAS_LANES + piece) & (head < FOX_HEADS)).astype(BF16)
    to_k = ((c == head * BIAS_LANES + piece + 3) & (head < FOX_HEADS)).astype(BF16)
    lane = lax.broadcasted_iota(I32, (1, LANES), 1)
    ones_q = jnp.zeros((1, LANES), F32)
    ones_k = jnp.zeros((1, LANES), F32)
    for h in range(FOX_HEADS):
        ones_k = jnp.where((lane >= h * BIAS_LANES) & (lane < h * BIAS_LANES + 3), 1.0, ones_k)
        ones_q = jnp.where((lane >= h * BIAS_LANES + 3) & (lane < (h + 1) * BIAS_LANES), 1.0, ones_q)
    fq_ref[0] = jnp.dot(pieces, to_q, preferred_element_type=F32) + ones_q
    fk_ref[0] = ones_k - jnp.dot(pieces, to_k, preferred_element_type=F32)


def _forget_bias_lanes(flog):
    b, s, _ = flog.shape
    spec = pl.BlockSpec((1, s, LANES), lambda i: (i, 0, 0))
    out = jax.ShapeDtypeStruct((b, s, LANES), F32)
    return pl.pallas_call(
        _cum_body,
        grid=(b,),
        in_specs=[spec],
        out_specs=[spec, spec],
        out_shape=[out, out],
        compiler_params=_cparams(("arbitrary",)),
        name="forget_cumsum",
    )(flog)


def _causal_sweep(t, tq, tk, tile, state):
    ratio = tq // tk
    assert ratio in (1, 2) and ratio * tk == tq
    n_full = t * ratio

    def steps(k, base, trips, st):
        def body(i, st):
            for u in range(k):
                st = tile(base + k * i + u, st, False, 0)
            return st
        return lax.fori_loop(0, trips, body, st)

    state = steps(4, 0, n_full // 4, state)
    state = steps(2, (n_full // 4) * 4, (n_full // 2) % 2, state)
    if ratio == 1:
        state = steps(1, (n_full // 2) * 2, n_full % 2, state)
    state = tile(n_full, state, True, 0)
    return state, (tile(n_full + 1, state, True, tk) if ratio == 2 else None)


def _finish_rows(state, lower, tk, finish):
    if lower is None:
        return finish(state)
    top = finish(jax.tree.map(lambda a: a[:tk], state))
    return jnp.concatenate([top, finish(lower)], axis=0)


def _diff_body(q_ref, k_ref, v_ref, lq1_ref, lk1_ref, lq2_ref, lk2_ref, g_ref, o_ref, *, tq, tk, lam_init):
    t = pl.program_id(2)
    q = q_ref[0]
    lane = lax.broadcasted_iota(I32, q.shape, 1)
    zero = jnp.zeros_like(q)
    qs = (jnp.where(lane < HEAD_DIM, q, zero), jnp.where(lane >= HEAD_DIM, q, zero))

    def tile(j, state, masked, lo):
        off = pl.multiple_of(j * tk, tk)
        kt = k_ref[0, pl.ds(off, tk), :]
        vt = v_ref[0, pl.ds(off, tk), :]
        out = []
        for mp in range(2):
            s = lax.dot_general(qs[mp][lo:], kt, _NT, preferred_element_type=F32)
            if masked:
                r = lax.broadcasted_iota(I32, s.shape, 0)
                c = lax.broadcasted_iota(I32, s.shape, 1)
                shift = CHUNK.bit_length() - 1
                s = jnp.where((c >> shift) <= (r >> shift), s, NEG_BIG)
            m, l, acc = state[mp]
            m_new = jnp.maximum(m[lo:], jnp.max(s, axis=1, keepdims=True))
            alpha = jnp.exp2(m[lo:] - m_new)
            p = jnp.exp2(s - m_new[:, :1])
            l_new = alpha * l[lo:] + jnp.sum(p, axis=1, keepdims=True)
            acc_new = alpha * acc[lo:] + jnp.dot(p.astype(BF16), vt, preferred_element_type=F32)
            out.append((m_new, l_new, acc_new))
        return tuple(out)

    lam = (jnp.exp(jnp.sum(lq1_ref[...] * lk1_ref[...], axis=1, keepdims=True))
           - jnp.exp(jnp.sum(lq2_ref[...] * lk2_ref[...], axis=1, keepdims=True)) + lam_init)

    def finish(st):
        (_, l0, a0), (_, l1, a1) = st
        out = a0 / l0 - lam * (a1 / l1)
        return (_rms(out) * g_ref[...] * (1.0 - lam_init)).astype(o_ref.dtype)

    init = (jnp.full((tq, LANES), NEG_BIG, F32), jnp.zeros((tq, LANES), F32), jnp.zeros((tq, LANES), F32))
    o_ref[0] = _finish_rows(*_causal_sweep(t, tq, tk, tile, (init, init)), tk, finish)


def _diff_attention(dq, dk, dv, lq1, lk1, lq2, lk2, subln, lam_init):
    b, s, _ = dq.shape
    tq, tk = _attn_tiles(s)
    qspec = pl.BlockSpec((1, tq, LANES), lambda i, h, t: (i, t, h))
    kvspec = pl.BlockSpec((1, s, LANES), lambda i, h, t: (i, 0, h))
    lspec = pl.BlockSpec((1, HEAD_DIM), lambda i, h, t: (0, 0))
    return pl.pallas_call(
        functools.partial(_diff_body, tq=tq, tk=tk, lam_init=lam_init),
        grid=(b, DIFF_HEADS, s // tq),
        in_specs=[qspec, kvspec, kvspec, lspec, lspec, lspec, lspec,
                  pl.BlockSpec((1, LANES), lambda i, h, t: (0, 0))],
        out_specs=qspec,
        out_shape=jax.ShapeDtypeStruct((b, s, D_DIFF), BF16),
        compiler_params=_cparams(("arbitrary", "arbitrary", "arbitrary")),
        name="diff_attention",
    )(dq, dk, dv, lq1, lk1, lq2, lk2, subln)


def _fox_operand(x_f32, bias, hh, h, lane):
    xh = x_f32 if hh == 0 else pltpu.roll(x_f32, HEAD_DIM, 1)
    bh = pltpu.roll(bias, HEAD_DIM - BIAS_LANES * h, 1)
    return jnp.where(lane < HEAD_DIM, xh, jnp.where(lane < HEAD_DIM + BIAS_LANES, bh, 0.0)).astype(BF16)


def _fox_body(q_ref, k_ref, v_ref, cq_ref, ck_ref, o_ref, ka_ref, va_ref, *, tq, tk):
    pair = pl.program_id(1)
    t = pl.program_id(2)
    heads = (2 * pair, 2 * pair + 1)
    lane = lax.broadcasted_iota(I32, (tq, LANES), 1)
    klane = lax.broadcasted_iota(I32, (tk, LANES), 1)

    @pl.when(t == 0)
    def _():
        def chunk(ci, carry):
            rows = pl.ds(pl.multiple_of(ci * tk, tk), tk)
            kf = k_ref[0, rows, :].astype(F32)
            vf = v_ref[0, rows, :].astype(F32)
            bias = ck_ref[0, rows, :]
            for hh in range(2):
                ka_ref[hh, rows, :] = _fox_operand(kf, bias, hh, heads[hh], klane)
                vh = vf if hh == 0 else pltpu.roll(vf, HEAD_DIM, 1)
                va_ref[hh, rows, :] = jnp.where(klane < HEAD_DIM, vh, 1.0).astype(BF16)
            return carry

        lax.fori_loop(0, k_ref.shape[1] // tk, chunk, 0)

    qf = q_ref[0].astype(F32)
    qs = [_fox_operand(qf, cq_ref[0], hh, heads[hh], lane) for hh in range(2)]

    def tile(j, state, masked, lo):
        off = pl.multiple_of(j * tk, tk)
        out = []
        for hh in range(2):
            s = lax.dot_general(qs[hh][lo:], ka_ref[hh, pl.ds(off, tk), :], _NT, preferred_element_type=F32)
            if masked:
                r = lax.broadcasted_iota(I32, s.shape, 0)
                c = lax.broadcasted_iota(I32, s.shape, 1)
                s = jnp.where(c <= r, s, NEG_BIG)
            m, acc = state[hh]
            m_new = jnp.maximum(m[lo:], jnp.max(s, axis=1, keepdims=True))
            p = jnp.exp2(s - m_new[:, :1]).astype(BF16)
            acc_new = jnp.exp2(m[lo:] - m_new) * acc[lo:] + jnp.dot(p, va_ref[hh, pl.ds(off, tk), :],
                                                                    preferred_element_type=F32)
            out.append((m_new, acc_new))
        return tuple(out)

    def finish(st):
        (_, a0), (_, a1) = st
        o0 = a0 / pltpu.roll(a0, HEAD_DIM, 1)
        o1 = a1 / pltpu.roll(a1, HEAD_DIM, 1)
        first = lax.broadcasted_iota(I32, a0.shape, 1) < HEAD_DIM
        return jnp.where(first, o0, pltpu.roll(o1, HEAD_DIM, 1)).astype(o_ref.dtype)

    init = (jnp.full((tq, LANES), NEG_BIG, F32), jnp.zeros((tq, LANES), F32))
    o_ref[0] = _finish_rows(*_causal_sweep(t, tq, tk, tile, (init, init)), tk, finish)


def _fox_attention(fq, fk, fv, bias_q, bias_k):
    b, s, _ = fq.shape
    tq, tk = _attn_tiles(s)
    qspec = pl.BlockSpec((1, tq, LANES), lambda i, p, t: (i, t, p))
    kvspec = pl.BlockSpec((1, s, LANES), lambda i, p, t: (i, 0, p))
    return pl.pallas_call(
        functools.partial(_fox_body, tq=tq, tk=tk),
        grid=(b, FOX_HEADS // 2, s // tq),
        in_specs=[qspec, kvspec, kvspec,
                  pl.BlockSpec((1, tq, LANES), lambda i, p, t: (i, t, 0)),
                  pl.BlockSpec((1, s, LANES), lambda i, p, t: (i, 0, 0))],
        out_specs=qspec,
        out_shape=jax.ShapeDtypeStruct((b, s, D_FOX), BF16),
        scratch_shapes=[pltpu.VMEM((2, s, LANES), BF16), pltpu.VMEM((2, s, LANES), BF16)],
        compiler_params=_cparams(("arbitrary", "arbitrary", "arbitrary")),
        name="fox_attention",
    )(fq, fk, fv, bias_q, bias_k)


def _outproj_body(x_ref, a_ref, b_ref, w_ref, g1_ref, sc_ref, sh_ref, ng_ref, rw_ref, rb_ref,
                  x1_ref, h2_ref, lg_ref):
    mix = (jnp.dot(a_ref[0], w_ref[:D_DIFF, :], preferred_element_type=F32)
           + jnp.dot(b_ref[0], w_ref[D_DIFF:, :], preferred_element_type=F32))
    x1 = x_ref[0] + g1_ref[0] * mix
    x1_ref[0] = x1
    h2 = _rms(x1) * ng_ref[...] * (1.0 + sc_ref[0]) + sh_ref[0]
    h_hi = h2.astype(BF16)
    h2_ref[...] = h_hi
    h_lo = (h2 - h_hi.astype(F32)).astype(BF16)
    rw = rw_ref[...]
    w_hi = rw.astype(BF16)
    w_lo = (rw - w_hi.astype(F32)).astype(BF16)
    nt = lambda a, b: lax.dot_general(a, b, _NT, preferred_element_type=F32)
    lg_ref[...] = nt(w_hi, h_hi) + (nt(w_hi, h_lo) + nt(w_lo, h_hi)) + rb_ref[...]


def _outproj(x, a_out, b_out, w_out, g1, sc2, sh2, ng, rwt, rb):
    b, s, d = x.shape
    n = b * s
    tm = min(512, s)
    nt = s // tm
    row = lambda w: pl.BlockSpec((1, tm, w), lambda i, j: (i, j, 0))
    vec = pl.BlockSpec((1, 1, d), lambda i, j: (i, 0, 0))
    const = lambda shp: pl.BlockSpec(shp, lambda i, j: (0,) * len(shp))
    return pl.pallas_call(
        _outproj_body,
        grid=(b, nt),
        in_specs=[row(d), row(D_DIFF), row(D_FOX), const(w_out.shape), vec, vec, vec,
                  const((1, d)), const(rwt.shape), const(rb.shape)],
        out_specs=[row(d),
                   pl.BlockSpec((tm, d), lambda i, j: (i * nt + j, 0)),
                   pl.BlockSpec((N_EXPERTS, tm), lambda i, j: (0, i * nt + j))],
        out_shape=[jax.ShapeDtypeStruct((b, s, d), F32),
                   jax.ShapeDtypeStruct((n, d), BF16),
                   jax.ShapeDtypeStruct((N_EXPERTS, n), F32)],
        compiler_params=_cparams(("arbitrary", "arbitrary")),
        name="out_proj",
    )(x, a_out, b_out, w_out, g1, sc2, sh2, ng, rwt, rb)


def _strict_lower(n):
    r = lax.broadcasted_iota(I32, (n, n), 0)
    c = lax.broadcasted_iota(I32, (n, n), 1)
    return (c < r).astype(F32)


def _align_up(x_f32, m):
    shift = m.bit_length() - 1
    return (((x_f32.astype(I32) + (m - 1)) >> shift) << shift).astype(F32)


def _route_body(lg_ref, pos_ref, gate_ref, cnt_ref):
    i = pl.program_id(0)

    @pl.when(i == 0)
    def _():
        cnt_ref[...] = jnp.zeros_like(cnt_ref)

    work = lg_ref[...]
    td = work.shape[1]
    eidx = lax.broadcasted_iota(I32, work.shape, 0)
    vals, hots = [], []
    for k in range(TOP_K):
        m = jnp.max(work, axis=0, keepdims=True)
        sel = jnp.min(jnp.where(work == m, eidx, N_EXPERTS), axis=0, keepdims=True)
        hot = eidx == sel
        vals.append(m)
        hots.append(hot)
        work = jnp.where(hot, -jnp.inf, work)
    ex = [jnp.exp(v - vals[0]) for v in vals]
    den = ex[0] + ex[1] + ex[2] + ex[3]
    for k in range(TOP_K):
        gate_ref[k:k + 1, :] = ex[k] / den

    chosen = hots[0] | hots[1] | hots[2] | hots[3]
    r = lax.broadcasted_iota(I32, (td, td), 0)
    c = lax.broadcasted_iota(I32, (td, td), 1)
    before = (r < c).astype(BF16)
    earlier = jnp.dot(chosen.astype(BF16), before, preferred_element_type=F32)
    cnt = jnp.sum(chosen.astype(F32), axis=1, keepdims=True)
    run = _align_up(cnt, RUN_ALIGN)
    start = jnp.dot(_strict_lower(N_EXPERTS), jnp.broadcast_to(run, (N_EXPERTS, LANES)),
                    preferred_element_type=F32, precision=HIGHEST)[:, 0:1]
    slot = start + earlier
    for k in range(TOP_K):
        pos_ref[k:k + 1, :] = jnp.sum(jnp.where(hots[k], slot, 0.0), axis=0, keepdims=True).astype(I32)
    lane = lax.broadcasted_iota(I32, cnt_ref.shape, 1)
    cnt_ref[...] = jnp.where(lane == i, run, cnt_ref[...])


def _route(logits_t):
    e, n = logits_t.shape
    td = min(ROUTE_TILE, n)
    ntp = _round_up(n // td, LANES)
    tok = pl.BlockSpec((TOP_K, td), lambda i: (0, i))
    return pl.pallas_call(
        _route_body,
        grid=(n // td,),
        in_specs=[pl.BlockSpec((e, td), lambda i: (0, i))],
        out_specs=[tok, tok, pl.BlockSpec((e, ntp), lambda i: (0, 0))],
        out_shape=[jax.ShapeDtypeStruct((TOP_K, n), I32),
                   jax.ShapeDtypeStruct((TOP_K, n), F32),
                   jax.ShapeDtypeStruct((e, ntp), F32)],
        compiler_params=_cparams(("arbitrary",)),
        name="route_topk",
    )(logits_t)


def _layout_body(run_ref, base_ref, be_ref, nu_ref, tail_ref):
    run = run_ref[...]
    ntp = run.shape[1]
    total = jnp.sum(run, axis=1, keepdims=True)
    region = _align_up(total, EXPERT_TILE)
    pstart = jnp.dot(_strict_lower(N_EXPERTS), jnp.broadcast_to(region, (N_EXPERTS, LANES)),
                     preferred_element_type=F32, precision=HIGHEST)[:, 0:1]
    ti = lax.broadcasted_iota(I32, (ntp, ntp), 0)
    tj = lax.broadcasted_iota(I32, (ntp, ntp), 1)
    within = jnp.dot(run, (ti < tj).astype(F32), preferred_element_type=F32, precision=HIGHEST)
    base_ref[...] = (pstart + within).astype(I32)
    pend = pstart + region
    blk = (lax.broadcasted_iota(I32, (N_EXPERTS, be_ref.shape[1]), 1) * EXPERT_TILE).astype(F32)
    be = jnp.sum((pend <= blk).astype(I32), axis=0, keepdims=True)
    be_ref[...] = jnp.minimum(be, N_EXPERTS - 1)
    used = jnp.sum(region, axis=0, keepdims=True).astype(I32) >> (EXPERT_TILE.bit_length() - 1)
    nu_ref[...] = jnp.broadcast_to(used, nu_ref.shape)
    lane = lax.broadcasted_iota(I32, tail_ref.shape, 1)
    tail_ref[...] = jnp.where(lane == 0, pstart + total, jnp.where(lane == 1, region - total, 0.0)).astype(I32)


def _layout(run_t, n_blocks):
    e, ntp = run_t.shape
    nbp = _round_up(n_blocks, LANES)
    full = lambda shp: pl.BlockSpec(shp, lambda: (0,) * len(shp))
    return pl.pallas_call(
        _layout_body,
        in_specs=[full((e, ntp))],
        out_specs=[full((e, ntp)), full((1, nbp)), full((1, LANES)), full((e, LANES))],
        out_shape=[jax.ShapeDtypeStruct((e, ntp), I32),
                   jax.ShapeDtypeStruct((1, nbp), I32),
                   jax.ShapeDtypeStruct((1, LANES), I32),
                   jax.ShapeDtypeStruct((e, LANES), I32)],
        name="expert_layout",
    )(run_t)


_RUN_LEVELS = tuple(1 << b for b in range(ROUTE_TILE.bit_length() - 1, RUN_ALIGN.bit_length() - 2, -1))


_SHORT_RUN = 64


def _for_each_run(run_tbl, base_tbl, tile, make_copy, start):
    def split(cnt, off, base, levels):
        done = jnp.int32(0)
        for lvl in levels:
            bit = cnt & lvl

            @pl.when(bit != 0)
            def _(done=done, lvl=lvl):
                cp = make_copy(pl.multiple_of(off + done, RUN_ALIGN), pl.multiple_of(base + done, RUN_ALIGN), lvl)
                if start:
                    cp.start()
                else:
                    cp.wait()
            done = done + bit

    def body(e, off):
        cnt = run_tbl[tile * N_EXPERTS + e]
        base = base_tbl[tile * N_EXPERTS + e]

        @pl.when(cnt < _SHORT_RUN)
        def _():
            split(cnt, off, base, tuple(lvl for lvl in _RUN_LEVELS if lvl < _SHORT_RUN))

        @pl.when(cnt >= _SHORT_RUN)
        def _():
            split(cnt, off, base, _RUN_LEVELS)
        return off + cnt

    lax.fori_loop(0, N_EXPERTS, body, jnp.int32(0))


def _wait_for_tile(run_tbl, tile, make_copy, sb):
    total = lax.fori_loop(0, N_EXPERTS, lambda e, acc: acc + run_tbl[tile * N_EXPERTS + e], jnp.int32(0))
    lvl = 1 << (sb.bit_length() - 1)
    while lvl >= RUN_ALIGN:
        @pl.when((total & lvl) != 0)
        def _(lvl=lvl):
            make_copy(0, 0, lvl).wait()
        lvl //= 2


def _slot_hits(pos, sb):
    j = lax.broadcasted_iota(I32, (sb, pos.shape[1]), 0)
    return [j == pos[k:k + 1, :] for k in range(TOP_K)]


def _sum_hits(hits, values):
    out = jnp.where(hits[0], values[0], 0.0)
    for k in range(1, TOP_K):
        out = out + jnp.where(hits[k], values[k], 0.0)
    return out


def _sorted_rows(td):
    return _round_up(TOP_K * td + N_EXPERTS * (RUN_ALIGN - 1), LANES)


def _dispatch_body(run_tbl, base_tbl, tail_len, tail_start, h_ref, pos_ref, gate_ref, xbuf_hbm,
                   sorted_ref, zero_ref, sem, zsem, *, d):
    i = pl.program_id(0)
    last = pl.num_programs(0) - 1
    slot = i % 2
    sb = sorted_ref.shape[1]

    def zero_copies(off, base, rows):
        del off
        return pltpu.make_async_copy(zero_ref.at[pl.ds(0, rows)], xbuf_hbm.at[pl.ds(base, rows)], zsem)

    @pl.when(i == 0)
    def _():
        zero_ref[...] = jnp.zeros_like(zero_ref)
        _for_each_run(tail_len, tail_start, 0, zero_copies, True)

    hits = _slot_hits(pos_ref[...], sb)
    perm = _sum_hits(hits, (1.0,) * TOP_K).astype(BF16)
    sorted_ref[slot, :, :d] = jnp.dot(perm, h_ref[...], preferred_element_type=F32)
    gates = gate_ref[...]
    gsel = _sum_hits(hits, [gates[k:k + 1, :] for k in range(TOP_K)])
    sorted_ref[slot, :, d:] = jnp.broadcast_to(jnp.sum(gsel, axis=1, keepdims=True), (sb, LANES))

    def copies(slot):
        return lambda off, base, rows: pltpu.make_async_copy(
            sorted_ref.at[slot, pl.ds(off, rows)], xbuf_hbm.at[pl.ds(base, rows)], sem.at[slot])

    _for_each_run(run_tbl, base_tbl, i, copies(slot), True)

    @pl.when(i > 0)
    def _():
        _wait_for_tile(run_tbl, i - 1, copies(1 - slot), sb)

    @pl.when(i == last)
    def _():
        _wait_for_tile(run_tbl, i, copies(slot), sb)
        _for_each_run(tail_len, tail_start, 0, zero_copies, False)


def _dispatch(run_tbl, base_tbl, tail_len, tail_start, h2, pos_t, gate_t, rows):
    n, d = h2.shape
    td = min(ROUTE_TILE, n)
    sb = _sorted_rows(td)
    dx = d + LANES
    assert EXPERT_TILE <= 2 * _RUN_LEVELS[0]
    tok = pl.BlockSpec((TOP_K, td), lambda i, *_: (0, i))
    grid_spec = pltpu.PrefetchScalarGridSpec(
        num_scalar_prefetch=4,
        grid=(n // td,),
        in_specs=[pl.BlockSpec((td, d), lambda i, *_: (i, 0)), tok, tok],
        out_specs=pl.BlockSpec(memory_space=pl.ANY),
        scratch_shapes=[pltpu.VMEM((2, sb, dx), F32), pltpu.VMEM((_RUN_LEVELS[0], dx), F32),
                        pltpu.SemaphoreType.DMA((2,)), pltpu.SemaphoreType.DMA],
    )
    return pl.pallas_call(
        functools.partial(_dispatch_body, d=d),
        grid_spec=grid_spec,
        out_shape=jax.ShapeDtypeStruct((rows, dx), F32),
        compiler_params=_cparams(("arbitrary",)),
        name="dispatch_rows",
    )(run_tbl, base_tbl, tail_len, tail_start, h2, pos_t, gate_t)


def _expert_body(be_ref, nu_ref, x_ref, w1_hbm, b1_ref, w2_hbm, b2_ref, y_ref,
                 w1f_ref, w2f_ref, w1b_ref, w2b_ref, sem, *, layer):
    i = pl.program_id(0)
    n = pl.num_programs(0)
    d_ff, d = w2b_ref.shape

    def fetch(blk):
        e = layer * N_EXPERTS + be_ref[blk]
        return (pltpu.make_async_copy(w1_hbm.at[e], w1f_ref, sem.at[0]),
                pltpu.make_async_copy(w2_hbm.at[e], w2f_ref, sem.at[1]))

    @pl.when(i == 0)
    def _():
        for cp in fetch(0):
            cp.start()

    @pl.when((i == 0) | (be_ref[i] != be_ref[jnp.maximum(i - 1, 0)]))
    def _():
        for cp in fetch(i):
            cp.wait()
        w1b_ref[...] = w1f_ref[...].astype(BF16)
        w2b_ref[...] = w2f_ref[...].astype(BF16)
        nxt = lax.while_loop(lambda j: (j < n) & (be_ref[jnp.minimum(j, n - 1)] == be_ref[i]),
                             lambda j: j + 1, i + 1)

        @pl.when(nxt < n)
        def _():
            for cp in fetch(nxt):
                cp.start()

    @pl.when(i < nu_ref[0])
    def _():
        gu = jnp.dot(x_ref[:, :d].astype(BF16), w1b_ref[...], preferred_element_type=F32) + b1_ref[0]
        gate = jnp.minimum(gu[:, :d_ff], SWIGLU_LIMIT)
        up = jnp.clip(gu[:, d_ff:], -SWIGLU_LIMIT, SWIGLU_LIMIT)
        glu = gate * jax.nn.sigmoid(gate * SWIGLU_ALPHA)
        act = ((up + 1.0) * glu).astype(BF16)
        y = jnp.dot(act, w2b_ref[...], preferred_element_type=F32) + b2_ref[0]
        y_ref[...] = x_ref[:, d:d + 1] * y

    @pl.when(i >= nu_ref[0])
    def _():
        y_ref[...] = jnp.zeros_like(y_ref)


def _experts(layer, block_expert, n_used, xbuf, w1, b1, w2, b2):
    rows, dx = xbuf.shape
    de, d, f2 = w1.shape
    f = w2.shape[1]
    blk = lambda i, be, nu: (jnp.maximum(jnp.minimum(i, nu[0] - 1), 0), 0)
    wsel = lambda i, be, nu: (layer * N_EXPERTS + be[i], 0, 0)
    grid_spec = pltpu.PrefetchScalarGridSpec(
        num_scalar_prefetch=2,
        grid=(rows // EXPERT_TILE,),
        in_specs=[pl.BlockSpec((EXPERT_TILE, dx), blk),
                  pl.BlockSpec(memory_space=pl.ANY),
                  pl.BlockSpec((1, 1, f2), wsel),
                  pl.BlockSpec(memory_space=pl.ANY),
                  pl.BlockSpec((1, 1, d), wsel)],
        out_specs=pl.BlockSpec((EXPERT_TILE, d), lambda i, be, nu: (i, 0)),
        scratch_shapes=[pltpu.VMEM((d, f2), F32), pltpu.VMEM((f, d), F32),
                        pltpu.VMEM((d, f2), BF16), pltpu.VMEM((f, d), BF16),
                        pltpu.SemaphoreType.DMA((2,))],
    )
    return pl.pallas_call(
        functools.partial(_expert_body, layer=layer),
        grid_spec=grid_spec,
        out_shape=jax.ShapeDtypeStruct((rows, d), F32),
        compiler_params=_cparams(("arbitrary",)),
        name="expert_swiglu",
    )(block_expert, n_used, xbuf, w1, b1.reshape(de, 1, f2), w2, b2.reshape(de, 1, d))


def _combine_body(run_tbl, base_tbl, x1_ref, g2_ref, pos_ref, fg_ref, ybuf_hbm, o_ref, ys_ref, sem, *, final):
    i = pl.program_id(0)
    last = pl.num_programs(0) - 1
    slot = i % 2
    sb = ys_ref.shape[1]

    def copies(slot):
        return lambda off, base, rows: pltpu.make_async_copy(
            ybuf_hbm.at[pl.ds(base, rows)], ys_ref.at[slot, pl.ds(off, rows)], sem.at[slot])

    @pl.when(i == 0)
    def _():
        ys_ref[...] = jnp.zeros_like(ys_ref)
        _for_each_run(run_tbl, base_tbl, 0, copies(0), True)

    @pl.when(i < last)
    def _():
        _for_each_run(run_tbl, base_tbl, i + 1, copies(1 - slot), True)

    _wait_for_tile(run_tbl, i, copies(slot), sb)

    perm = _sum_hits(_slot_hits(pos_ref[...], sb), (1.0,) * TOP_K).astype(BF16)
    moe = lax.dot_general(perm, ys_ref[slot].astype(BF16), _TN, preferred_element_type=F32)
    out = x1_ref[...] + g2_ref[0] * moe
    if final:
        out = _rms(out) * fg_ref[...]
    o_ref[...] = out


def _combine(run_tbl, base_tbl, x1, g2, pos_t, final_g, ybuf, seq, final):
    n, d = x1.shape
    td = min(ROUTE_TILE, n)
    sb = _sorted_rows(td)
    grid_spec = pltpu.PrefetchScalarGridSpec(
        num_scalar_prefetch=2,
        grid=(n // td,),
        in_specs=[pl.BlockSpec((td, d), lambda i, *_: (i, 0)),
                  pl.BlockSpec((1, 1, d), lambda i, *_: ((i * td) // seq, 0, 0)),
                  pl.BlockSpec((TOP_K, td), lambda i, *_: (0, i)),
                  pl.BlockSpec((1, d), lambda i, *_: (0, 0)),
                  pl.BlockSpec(memory_space=pl.ANY)],
        out_specs=pl.BlockSpec((td, d), lambda i, *_: (i, 0)),
        scratch_shapes=[pltpu.VMEM((2, sb, d), F32), pltpu.SemaphoreType.DMA((2,))],
    )
    return pl.pallas_call(
        functools.partial(_combine_body, final=final),
        grid_spec=grid_spec,
        out_shape=jax.ShapeDtypeStruct((n, d), F32),
        compiler_params=_cparams(("arbitrary",)),
        name="combine_rows",
    )(run_tbl, base_tbl, x1, g2, pos_t, final_g, ybuf)


def _moe(layer, x1, h2, logits_t, g2, w1, b1, w2, b2, final_g, seq, final):
    n, d = h2.shape
    td = min(ROUTE_TILE, n)
    n_tiles = n // td
    assert seq % td == 0
    rows = (_round_up(n * TOP_K + n_tiles * N_EXPERTS * (RUN_ALIGN - 1), EXPERT_TILE)
            + N_EXPERTS * EXPERT_TILE)
    n_blocks = rows // EXPERT_TILE
    pos_t, gate_t, run_t = _route(logits_t)
    base_t, be, nu, tail = _layout(run_t, n_blocks)
    run_tbl = run_t[:, :n_tiles].T.astype(I32).reshape(-1)
    base_tbl = base_t[:, :n_tiles].T.reshape(-1)
    xbuf = _dispatch(run_tbl, base_tbl, tail[:, 1], tail[:, 0], h2, pos_t, gate_t, rows)
    ybuf = _experts(layer, be[0, :n_blocks], nu[0, :1], xbuf, w1, b1, w2, b2)
    return _combine(run_tbl, base_tbl, x1, g2, pos_t, final_g, ybuf, seq, final)


def kernel(x, c, positions, ada_w, ada_b, norm_mix, norm_ffn, w_in, b_forget, lambda_q1, lambda_k1,
           lambda_q2, lambda_k2, diff_subln, w_out, router_w, router_b, exp_w1, exp_b1, exp_w2, exp_b2,
           final_norm):
    b, s, d = x.shape
    depth = ada_w.shape[0]
    mod = _ada_mod(c, ada_w, ada_b)
    cosf, sinf = _rope_tables(positions)
    n_in = w_in.shape[2]
    pad_cols = 3072 + LANES - n_in
    flat = lambda a: a.reshape((a.shape[0] * a.shape[1],) + a.shape[2:])
    w1_all, b1_all, w2_all, b2_all = flat(exp_w1), flat(exp_b1), flat(exp_w2), flat(exp_b2)
    for l in range(depth):
        lam_init = 0.8 - 0.6 * math.exp(-0.3 * l)
        sh1, sc1, g1, sh2, sc2, g2 = [mod[l, :, i * d:(i + 1) * d].reshape(b, 1, d) for i in range(6)]
        w_pad = jnp.pad(w_in[l], ((0, 0), (0, pad_cols))).astype(BF16)
        bf_pad = jnp.pad(b_forget[l], (0, LANES - FOX_HEADS)).reshape(1, LANES)
        dq, dk, dv, fq, fk, fv, flog = _inproj(x, sc1, sh1, norm_mix[l].reshape(1, d), w_pad, bf_pad,
                                               cosf, sinf)
        bias_q, bias_k = _forget_bias_lanes(flog)
        a_out = _diff_attention(dq, dk, dv, lambda_q1[l].reshape(1, -1), lambda_k1[l].reshape(1, -1),
                                lambda_q2[l].reshape(1, -1), lambda_k2[l].reshape(1, -1),
                                diff_subln[l].reshape(1, -1), lam_init)
        b_out = _fox_attention(fq, fk, fv, bias_q, bias_k)
        x1, h2, logits_t = _outproj(x, a_out, b_out, w_out[l].astype(BF16), g1, sc2, sh2,
                                    norm_ffn[l].reshape(1, d), router_w[l].T,
                                    router_b[l].reshape(-1, 1))
        x = _moe(l, x1.reshape(b * s, d), h2, logits_t, g2, w1_all, b1_all, w2_all, b2_all,
                 final_norm.reshape(1, d), s, l == depth - 1).reshape(b, s, d)
    return x
```

```python
import functools
import math

import jax
import jax.numpy as jnp
from jax import lax
from jax.experimental import pallas as pl
from jax.experimental.pallas import tpu as pltpu

F32 = jnp.float32
BF16 = jnp.bfloat16
I32 = jnp.int32
HIGHEST = lax.Precision.HIGHEST

HEAD_DIM = 64
LANES = 128
SUBLANES = 8
CHUNK = 64
DIFF_HEADS = 4
FOX_HEADS = 8
D_DIFF = DIFF_HEADS * 2 * HEAD_DIM
D_FOX = FOX_HEADS * HEAD_DIM
ROPE_THETA = 500000.0
ROPE_DIM = HEAD_DIM // 4
ROPE_HALF = ROPE_DIM // 2
N_EXPERTS = 32
TOP_K = 4
SWIGLU_LIMIT = 7.0
SWIGLU_ALPHA = 1.702
NORM_EPS = 1e-6
NEG_BIG = -1e30
LOG2E = math.log2(math.e)
ATTN_Q_TILE = 1024
ATTN_K_TILE = 512
EXPERT_TILE = 512
ROUTE_TILE = 256
ROUTE_TILES_PER_STEP = 4
RUN_ALIGN = SUBLANES
VMEM_LIMIT = 56 * 1024 * 1024

_NT = (((1,), (1,)), ((), ()))
_TN = (((0,), (0,)), ((), ()))


def _cparams(sem, vmem=None):
    return pltpu.CompilerParams(dimension_semantics=sem, vmem_limit_bytes=vmem or VMEM_LIMIT)


def _rms(x):
    return x * lax.rsqrt(jnp.mean(x * x, axis=-1, keepdims=True) + NORM_EPS)


def _round_up(x, m):
    return (x + m - 1) // m * m


def _attn_tiles(s):
    tq = min(ATTN_Q_TILE, s)
    return tq, min(ATTN_K_TILE, tq)


def _ada_body(c_ref, w_ref, b_ref, o_ref):
    c = c_ref[...]
    ca = c * jax.nn.sigmoid(c)
    o_ref[0] = jnp.dot(ca, w_ref[0], preferred_element_type=F32, precision=HIGHEST) + b_ref[0]


def _ada_mod(c, ada_w, ada_b):
    depth, d, n6 = ada_w.shape
    b = c.shape[0]
    tn = 1536
    return pl.pallas_call(
        _ada_body,
        grid=(depth, n6 // tn),
        in_specs=[pl.BlockSpec((b, d), lambda l, j: (0, 0)),
                  pl.BlockSpec((1, d, tn), lambda l, j: (l, 0, j)),
                  pl.BlockSpec((1, 1, tn), lambda l, j: (l, 0, j))],
        out_specs=pl.BlockSpec((1, b, tn), lambda l, j: (l, 0, j)),
        out_shape=jax.ShapeDtypeStruct((depth, b, n6), F32),
        compiler_params=_cparams(("arbitrary", "arbitrary")),
        name="ada_mod",
    )(c, ada_w, ada_b.reshape(depth, 1, n6))


def _rope_body(pos_ref, invf_ref, cos_ref, sin_ref):
    ang = pos_ref[0].astype(F32) * invf_ref[...]
    j = lax.broadcasted_iota(I32, ang.shape, 1) & (HEAD_DIM - 1)
    c = jnp.cos(ang)
    s = jnp.sin(ang)
    cos_ref[0] = jnp.where(j < ROPE_DIM, c, 1.0)
    sin_ref[0] = jnp.where(j < ROPE_HALF, -s, jnp.where(j < ROPE_DIM, s, 0.0))


def _rope_tables(positions):
    b, s = positions.shape
    ts = min(512, s)
    inv_freq = ROPE_THETA ** (-jnp.arange(0, ROPE_DIM, 2, dtype=F32) / ROPE_DIM)
    invf = inv_freq[jnp.arange(LANES) % ROPE_HALF].reshape(1, LANES)
    out = jax.ShapeDtypeStruct((b, s, LANES), F32)
    return pl.pallas_call(
        _rope_body,
        grid=(b, s // ts),
        in_specs=[pl.BlockSpec((1, ts, 1), lambda i, j: (i, j, 0)),
                  pl.BlockSpec((1, LANES), lambda i, j: (0, 0))],
        out_specs=[pl.BlockSpec((1, ts, LANES), lambda i, j: (i, j, 0))] * 2,
        out_shape=[out, out],
        compiler_params=_cparams(("arbitrary", "arbitrary")),
        name="rope_tables",
    )(positions.reshape(b, s, 1), invf)


def _inproj_body(x_ref, sc_ref, sh_ref, g_ref, w_ref, bf_ref, cos_ref, sin_ref,
                 dq_ref, dk_ref, dv_ref, fq_ref, fk_ref, fv_ref, fl_ref):
    h = _rms(x_ref[0]) * g_ref[...] * (1.0 + sc_ref[0]) + sh_ref[0]
    hb = h.astype(BF16)
    cosf = cos_ref[0]
    sinf = sin_ref[0]
    lane = lax.broadcasted_iota(I32, cosf.shape, 1)
    first = (lane & (HEAD_DIM - 1)) < ROPE_HALF

    def rope(p):
        nxt = pltpu.roll(p, LANES - ROPE_HALF, 1)
        prv = pltpu.roll(p, ROPE_HALF, 1)
        return p * cosf + jnp.where(first, nxt, prv) * sinf

    def proj(c0):
        return jnp.dot(hb, w_ref[:, c0:c0 + 512], preferred_element_type=F32)

    scale = HEAD_DIM ** -0.5 * LOG2E
    p = proj(0)
    for c in range(4):
        sl = slice(c * LANES, (c + 1) * LANES)
        dq_ref[0, :, sl] = (rope(p[:, sl]) * scale).astype(BF16)
    p = proj(512)
    for c in range(4):
        sl = slice(c * LANES, (c + 1) * LANES)
        dk_ref[0, :, sl] = rope(p[:, sl]).astype(BF16)
    dv_ref[0] = proj(1024).astype(BF16)
    fq_ref[0] = (proj(1536) * scale).astype(BF16)
    fk_ref[0] = proj(2048).astype(BF16)
    fv_ref[0] = proj(2560).astype(BF16)
    z = jnp.dot(hb, w_ref[:, 3072:3072 + LANES], preferred_element_type=F32) + bf_ref[...]
    fl_ref[0] = -(jnp.maximum(-z, 0.0) + jnp.log1p(jnp.exp(-jnp.abs(z))))


def _inproj(x, sc, sh, g, w_pad, bf_pad, cosf, sinf):
    b, s, d = x.shape
    tm = min(512, s)
    act = lambda w: jax.ShapeDtypeStruct((b, s, w), BF16)
    row = lambda w: pl.BlockSpec((1, tm, w), lambda i, j: (i, j, 0))
    vec = pl.BlockSpec((1, 1, d), lambda i, j: (i, 0, 0))
    return pl.pallas_call(
        _inproj_body,
        grid=(b, s // tm),
        in_specs=[row(d), vec, vec,
                  pl.BlockSpec((1, d), lambda i, j: (0, 0)),
                  pl.BlockSpec(w_pad.shape, lambda i, j: (0, 0)),
                  pl.BlockSpec((1, LANES), lambda i, j: (0, 0)),
                  row(LANES), row(LANES)],
        out_specs=[row(512)] * 6 + [row(LANES)],
        out_shape=[act(512)] * 6 + [jax.ShapeDtypeStruct((b, s, LANES), F32)],
        compiler_params=_cparams(("arbitrary", "arbitrary")),
        name="in_proj",
    )(x, sc, sh, g, w_pad, bf_pad, cosf, sinf)


BIAS_LANES = 6


def _split3(c):
    hi = c.astype(BF16).astype(F32)
    r = c - hi
    mid = r.astype(BF16).astype(F32)
    return hi, mid, r - mid


def _cum_body(fl_ref, fq_ref, fk_ref):
    x = fl_ref[0]
    s = x.shape[0]
    row = lax.broadcasted_iota(I32, x.shape, 0)
    d = 1
    while d < s:
        x = x + jnp.where(row >= d, pltpu.roll(x, d, 0), 0.0)
        d *= 2
    pieces = jnp.concatenate(_split3(x * LOG2E), axis=1).astype(BF16)
    r = lax.broadcasted_iota(I32, (3 * LANES, LANES), 0)
    c = lax.broadcasted_iota(I32, (3 * LANES, LANES), 1)
    head, piece = r & (LANES - 1), r >> (LANES.bit_length() - 1)
    to_q = ((c == head * BIAS_LANES + piece) & (head < FOX_HEADS)).astype(BF16)
    to_k = ((c == head * BIAS_LANES + piece + 3) & (head < FOX_HEADS)).astype(BF16)
    lane = lax.broadcasted_iota(I32, (1, LANES), 1)
    ones_q = jnp.zeros((1, LANES), F32)
    ones_k = jnp.zeros((1, LANES), F32)
    for h in range(FOX_HEADS):
        ones_k = jnp.where((lane >= h * BIAS_LANES) & (lane < h * BIAS_LANES + 3), 1.0, ones_k)
        ones_q = jnp.where((lane >= h * BIAS_LANES + 3) & (lane < (h + 1) * BIAS_LANES), 1.0, ones_q)
    fq_ref[0] = jnp.dot(pieces, to_q, preferred_element_type=F32) + ones_q
    fk_ref[0] = ones_k - jnp.dot(pieces, to_k, preferred_element_type=F32)


def _forget_bias_lanes(flog):
    b, s, _ = flog.shape
    spec = pl.BlockSpec((1, s, LANES), lambda i: (i, 0, 0))
    out = jax.ShapeDtypeStruct((b, s, LANES), F32)
    return pl.pallas_call(
        _cum_body,
        grid=(b,),
        in_specs=[spec],
        out_specs=[spec, spec],
        out_shape=[out, out],
        compiler_params=_cparams(("arbitrary",)),
        name="forget_cumsum",
    )(flog)


def _causal_sweep(t, tq, tk, tile, state):
    ratio = tq // tk
    assert ratio in (1, 2) and ratio * tk == tq
    n_full = t * ratio

    def steps(k, base, trips, st):
        def body(i, st):
            for u in range(k):
                st = tile(base + k * i + u, st, False, 0)
            return st
        return lax.fori_loop(0, trips, body, st)

    state = steps(4, 0, n_full // 4, state)
    state = steps(2, (n_full // 4) * 4, (n_full // 2) % 2, state)
    if ratio == 1:
        state = steps(1, (n_full // 2) * 2, n_full % 2, state)
    state = tile(n_full, state, True, 0)
    return state, (tile(n_full + 1, state, True, tk) if ratio == 2 else None)


def _finish_rows(state, lower, tk, finish):
    if lower is None:
        return finish(state)
    top = finish(jax.tree.map(lambda a: a[:tk], state))
    return jnp.concatenate([top, finish(lower)], axis=0)


def _diff_body(q_ref, k_ref, v_ref, lq1_ref, lk1_ref, lq2_ref, lk2_ref, g_ref, o_ref, *, tq, tk, lam_init):
    t = pl.program_id(2)
    q = q_ref[0]
    lane = lax.broadcasted_iota(I32, q.shape, 1)
    zero = jnp.zeros_like(q)
    qs = (jnp.where(lane < HEAD_DIM, q, zero), jnp.where(lane >= HEAD_DIM, q, zero))

    def tile(j, state, masked, lo):
        off = pl.multiple_of(j * tk, tk)
        kt = k_ref[0, pl.ds(off, tk), :]
        vt = v_ref[0, pl.ds(off, tk), :]
        out = []
        for mp in range(2):
            s = lax.dot_general(qs[mp][lo:], kt, _NT, preferred_element_type=F32)
            if masked:
                r = lax.broadcasted_iota(I32, s.shape, 0)
                c = lax.broadcasted_iota(I32, s.shape, 1)
                shift = CHUNK.bit_length() - 1
                s = jnp.where((c >> shift) <= (r >> shift), s, NEG_BIG)
            m, l, acc = state[mp]
            m_new = jnp.maximum(m[lo:], jnp.max(s, axis=1, keepdims=True))
            alpha = jnp.exp2(m[lo:] - m_new)
            p = jnp.exp2(s - m_new[:, :1])
            l_new = alpha * l[lo:] + jnp.sum(p, axis=1, keepdims=True)
            acc_new = alpha * acc[lo:] + jnp.dot(p.astype(BF16), vt, preferred_element_type=F32)
            out.append((m_new, l_new, acc_new))
        return tuple(out)

    lam = (jnp.exp(jnp.sum(lq1_ref[...] * lk1_ref[...], axis=1, keepdims=True))
           - jnp.exp(jnp.sum(lq2_ref[...] * lk2_ref[...], axis=1, keepdims=True)) + lam_init)

    def finish(st):
        (_, l0, a0), (_, l1, a1) = st
        out = a0 / l0 - lam * (a1 / l1)
        return (_rms(out) * g_ref[...] * (1.0 - lam_init)).astype(o_ref.dtype)

    init = (jnp.full((tq, LANES), NEG_BIG, F32), jnp.zeros((tq, LANES), F32), jnp.zeros((tq, LANES), F32))
    o_ref[0] = _finish_rows(*_causal_sweep(t, tq, tk, tile, (init, init)), tk, finish)


def _diff_attention(dq, dk, dv, lq1, lk1, lq2, lk2, subln, lam_init):
    b, s, _ = dq.shape
    tq, tk = _attn_tiles(s)
    qspec = pl.BlockSpec((1, tq, LANES), lambda i, h, t: (i, t, h))
    kvspec = pl.BlockSpec((1, s, LANES), lambda i, h, t: (i, 0, h))
    lspec = pl.BlockSpec((1, HEAD_DIM), lambda i, h, t: (0, 0))
    return pl.pallas_call(
        functools.partial(_diff_body, tq=tq, tk=tk, lam_init=lam_init),
        grid=(b, DIFF_HEADS, s // tq),
        in_specs=[qspec, kvspec, kvspec, lspec, lspec, lspec, lspec,
                  pl.BlockSpec((1, LANES), lambda i, h, t: (0, 0))],
        out_specs=qspec,
        out_shape=jax.ShapeDtypeStruct((b, s, D_DIFF), BF16),
        compiler_params=_cparams(("arbitrary", "arbitrary", "arbitrary")),
        name="diff_attention",
    )(dq, dk, dv, lq1, lk1, lq2, lk2, subln)


def _fox_operand(x_f32, bias, hh, h, lane):
    xh = x_f32 if hh == 0 else pltpu.roll(x_f32, HEAD_DIM, 1)
    bh = pltpu.roll(bias, HEAD_DIM - BIAS_LANES * h, 1)
    return jnp.where(lane < HEAD_DIM, xh, jnp.where(lane < HEAD_DIM + BIAS_LANES, bh, 0.0)).astype(BF16)


def _fox_body(q_ref, k_ref, v_ref, cq_ref, ck_ref, o_ref, ka_ref, va_ref, *, tq, tk):
    pair = pl.program_id(1)
    t = pl.program_id(2)
    heads = (2 * pair, 2 * pair + 1)
    lane = lax.broadcasted_iota(I32, (tq, LANES), 1)
    klane = lax.broadcasted_iota(I32, (tk, LANES), 1)

    @pl.when(t == 0)
    def _():
        def chunk(ci, carry):
            rows = pl.ds(pl.multiple_of(ci * tk, tk), tk)
            kf = k_ref[0, rows, :].astype(F32)
            vf = v_ref[0, rows, :].astype(F32)
            bias = ck_ref[0, rows, :]
            for hh in range(2):
                ka_ref[hh, rows, :] = _fox_operand(kf, bias, hh, heads[hh], klane)
                vh = vf if hh == 0 else pltpu.roll(vf, HEAD_DIM, 1)
                va_ref[hh, rows, :] = jnp.where(klane < HEAD_DIM, vh, 1.0).astype(BF16)
            return carry

        lax.fori_loop(0, k_ref.shape[1] // tk, chunk, 0)

    qf = q_ref[0].astype(F32)
    qs = [_fox_operand(qf, cq_ref[0], hh, heads[hh], lane) for hh in range(2)]

    def tile(j, state, masked, lo):
        off = pl.multiple_of(j * tk, tk)
        out = []
        for hh in range(2):
            s = lax.dot_general(qs[hh][lo:], ka_ref[hh, pl.ds(off, tk), :], _NT, preferred_element_type=F32)
            if masked:
                r = lax.broadcasted_iota(I32, s.shape, 0)
                c = lax.broadcasted_iota(I32, s.shape, 1)
                s = jnp.where(c <= r, s, NEG_BIG)
            m, acc = state[hh]
            m_new = jnp.maximum(m[lo:], jnp.max(s, axis=1, keepdims=True))
            p = jnp.exp2(s - m_new[:, :1]).astype(BF16)
            acc_new = jnp.exp2(m[lo:] - m_new) * acc[lo:] + jnp.dot(p, va_ref[hh, pl.ds(off, tk), :],
                                                                    preferred_element_type=F32)
            out.append((m_new, acc_new))
        return tuple(out)

    def finish(st):
        (_, a0), (_, a1) = st
        o0 = a0 / pltpu.roll(a0, HEAD_DIM, 1)
        o1 = a1 / pltpu.roll(a1, HEAD_DIM, 1)
        first = lax.broadcasted_iota(I32, a0.shape, 1) < HEAD_DIM
        return jnp.where(first, o0, pltpu.roll(o1, HEAD_DIM, 1)).astype(o_ref.dtype)

    init = (jnp.full((tq, LANES), NEG_BIG, F32), jnp.zeros((tq, LANES), F32))
    o_ref[0] = _finish_rows(*_causal_sweep(t, tq, tk, tile, (init, init)), tk, finish)


def _fox_attention(fq, fk, fv, bias_q, bias_k):
    b, s, _ = fq.shape
    tq, tk = _attn_tiles(s)
    qspec = pl.BlockSpec((1, tq, LANES), lambda i, p, t: (i, t, p))
    kvspec = pl.BlockSpec((1, s, LANES), lambda i, p, t: (i, 0, p))
    return pl.pallas_call(
        functools.partial(_fox_body, tq=tq, tk=tk),
        grid=(b, FOX_HEADS // 2, s // tq),
        in_specs=[qspec, kvspec, kvspec,
                  pl.BlockSpec((1, tq, LANES), lambda i, p, t: (i, t, 0)),
                  pl.BlockSpec((1, s, LANES), lambda i, p, t: (i, 0, 0))],
        out_specs=qspec,
        out_shape=jax.ShapeDtypeStruct((b, s, D_FOX), BF16),
        scratch_shapes=[pltpu.VMEM((2, s, LANES), BF16), pltpu.VMEM((2, s, LANES), BF16)],
        compiler_params=_cparams(("arbitrary", "arbitrary", "arbitrary")),
        name="fox_attention",
    )(fq, fk, fv, bias_q, bias_k)


def _outproj_body(x_ref, a_ref, b_ref, w_ref, g1_ref, sc_ref, sh_ref, ng_ref, rw_ref, rb_ref,
                  x1_ref, h2_ref, lg_ref):
    mix = (jnp.dot(a_ref[0], w_ref[:D_DIFF, :], preferred_element_type=F32)
           + jnp.dot(b_ref[0], w_ref[D_DIFF:, :], preferred_element_type=F32))
    x1 = x_ref[0] + g1_ref[0] * mix
    x1_ref[0] = x1
    h2 = _rms(x1) * ng_ref[...] * (1.0 + sc_ref[0]) + sh_ref[0]
    h_hi = h2.astype(BF16)
    h2_ref[...] = h_hi
    h_lo = (h2 - h_hi.astype(F32)).astype(BF16)
    rw = rw_ref[...]
    w_hi = rw.astype(BF16)
    w_lo = (rw - w_hi.astype(F32)).astype(BF16)
    nt = lambda a, b: lax.dot_general(a, b, _NT, preferred_element_type=F32)
    lg_ref[...] = nt(w_hi, h_hi) + (nt(w_hi, h_lo) + nt(w_lo, h_hi)) + rb_ref[...]


def _outproj(x, a_out, b_out, w_out, g1, sc2, sh2, ng, rwt, rb):
    b, s, d = x.shape
    n = b * s
    tm = min(512, s)
    nt = s // tm
    row = lambda w: pl.BlockSpec((1, tm, w), lambda i, j: (i, j, 0))
    vec = pl.BlockSpec((1, 1, d), lambda i, j: (i, 0, 0))
    const = lambda shp: pl.BlockSpec(shp, lambda i, j: (0,) * len(shp))
    return pl.pallas_call(
        _outproj_body,
        grid=(b, nt),
        in_specs=[row(d), row(D_DIFF), row(D_FOX), const(w_out.shape), vec, vec, vec,
                  const((1, d)), const(rwt.shape), const(rb.shape)],
        out_specs=[row(d),
                   pl.BlockSpec((tm, d), lambda i, j: (i * nt + j, 0)),
                   pl.BlockSpec((N_EXPERTS, tm), lambda i, j: (0, i * nt + j))],
        out_shape=[jax.ShapeDtypeStruct((b, s, d), F32),
                   jax.ShapeDtypeStruct((n, d), BF16),
                   jax.ShapeDtypeStruct((N_EXPERTS, n), F32)],
        compiler_params=_cparams(("arbitrary", "arbitrary")),
        name="out_proj",
    )(x, a_out, b_out, w_out, g1, sc2, sh2, ng, rwt, rb)


def _strict_lower(n):
    r = lax.broadcasted_iota(I32, (n, n), 0)
    c = lax.broadcasted_iota(I32, (n, n), 1)
    return (c < r).astype(F32)


def _align_up(x_f32, m):
    shift = m.bit_length() - 1
    return (((x_f32.astype(I32) + (m - 1)) >> shift) << shift).astype(F32)


def _route_body(lg_ref, pos_ref, gate_ref, cnt_ref, *, td):
    @pl.when(pl.program_id(0) == 0)
    def _():
        cnt_ref[...] = jnp.zeros_like(cnt_ref)

    per_step = lg_ref.shape[1] // td
    for u in range(per_step):
        cols = slice(u * td, (u + 1) * td)
        _route_tile(pl.program_id(0) * per_step + u, lg_ref[:, cols], pos_ref.at[:, cols], gate_ref.at[:, cols],
                    cnt_ref)


def _route_tile(i, work, pos_ref, gate_ref, cnt_ref):
    td = work.shape[1]
    eidx = lax.broadcasted_iota(I32, work.shape, 0)
    vals, hots = [], []
    for k in range(TOP_K):
        m = jnp.max(work, axis=0, keepdims=True)
        sel = jnp.min(jnp.where(work == m, eidx, N_EXPERTS), axis=0, keepdims=True)
        hot = eidx == sel
        vals.append(m)
        hots.append(hot)
        work = jnp.where(hot, -jnp.inf, work)
    ex = [jnp.exp(v - vals[0]) for v in vals]
    den = ex[0] + ex[1] + ex[2] + ex[3]
    for k in range(TOP_K):
        gate_ref[k:k + 1, :] = ex[k] / den

    chosen = hots[0] | hots[1] | hots[2] | hots[3]
    r = lax.broadcasted_iota(I32, (td, td), 0)
    c = lax.broadcasted_iota(I32, (td, td), 1)
    before = (r < c).astype(BF16)
    earlier = jnp.dot(chosen.astype(BF16), before, preferred_element_type=F32)
    cnt = jnp.sum(chosen.astype(F32), axis=1, keepdims=True)
    run = _align_up(cnt, RUN_ALIGN)
    start = jnp.dot(_strict_lower(N_EXPERTS), jnp.broadcast_to(run, (N_EXPERTS, LANES)),
                    preferred_element_type=F32, precision=HIGHEST)[:, 0:1]
    slot = start + earlier
    for k in range(TOP_K):
        pos_ref[k:k + 1, :] = jnp.sum(jnp.where(hots[k], slot, 0.0), axis=0, keepdims=True).astype(I32)
    lane = lax.broadcasted_iota(I32, cnt_ref.shape, 1)
    cnt_ref[...] = jnp.where(lane == i, run, cnt_ref[...])


def _route(logits_t):
    e, n = logits_t.shape
    td = min(ROUTE_TILE, n)
    ntp = _round_up(n // td, LANES)
    step = math.gcd(n, ROUTE_TILES_PER_STEP * td)
    tok = pl.BlockSpec((TOP_K, step), lambda i: (0, i))
    return pl.pallas_call(
        functools.partial(_route_body, td=td),
        grid=(n // step,),
        in_specs=[pl.BlockSpec((e, step), lambda i: (0, i))],
        out_specs=[tok, tok, pl.BlockSpec((e, ntp), lambda i: (0, 0))],
        out_shape=[jax.ShapeDtypeStruct((TOP_K, n), I32),
                   jax.ShapeDtypeStruct((TOP_K, n), F32),
                   jax.ShapeDtypeStruct((e, ntp), F32)],
        compiler_params=_cparams(("arbitrary",)),
        name="route_topk",
    )(logits_t)


def _layout_body(run_ref, base_ref, be_ref, nu_ref, tail_ref):
    run = run_ref[...]
    ntp = run.shape[1]
    total = jnp.sum(run, axis=1, keepdims=True)
    region = _align_up(total, EXPERT_TILE)
    pstart = jnp.dot(_strict_lower(N_EXPERTS), jnp.broadcast_to(region, (N_EXPERTS, LANES)),
                     preferred_element_type=F32, precision=HIGHEST)[:, 0:1]
    ti = lax.broadcasted_iota(I32, (ntp, ntp), 0)
    tj = lax.broadcasted_iota(I32, (ntp, ntp), 1)
    within = jnp.dot(run, (ti < tj).astype(F32), preferred_element_type=F32, precision=HIGHEST)
    base_ref[...] = (pstart + within).astype(I32)
    pend = pstart + region
    blk = (lax.broadcasted_iota(I32, (N_EXPERTS, be_ref.shape[1]), 1) * EXPERT_TILE).astype(F32)
    be = jnp.sum((pend <= blk).astype(I32), axis=0, keepdims=True)
    be_ref[...] = jnp.minimum(be, N_EXPERTS - 1)
    used = jnp.sum(region, axis=0, keepdims=True).astype(I32) >> (EXPERT_TILE.bit_length() - 1)
    nu_ref[...] = jnp.broadcast_to(used, nu_ref.shape)
    lane = lax.broadcasted_iota(I32, tail_ref.shape, 1)
    tail_ref[...] = jnp.where(lane == 0, pstart + total, jnp.where(lane == 1, region - total, 0.0)).astype(I32)


def _layout(run_t, n_blocks):
    e, ntp = run_t.shape
    nbp = _round_up(n_blocks, LANES)
    full = lambda shp: pl.BlockSpec(shp, lambda: (0,) * len(shp))
    return pl.pallas_call(
        _layout_body,
        in_specs=[full((e, ntp))],
        out_specs=[full((e, ntp)), full((1, nbp)), full((1, LANES)), full((e, LANES))],
        out_shape=[jax.ShapeDtypeStruct((e, ntp), I32),
                   jax.ShapeDtypeStruct((1, nbp), I32),
                   jax.ShapeDtypeStruct((1, LANES), I32),
                   jax.ShapeDtypeStruct((e, LANES), I32)],
        name="expert_layout",
    )(run_t)


_RUN_LEVELS = tuple(1 << b for b in range(ROUTE_TILE.bit_length() - 1, RUN_ALIGN.bit_length() - 2, -1))


def _for_each_run(run_tbl, base_tbl, tile, make_copy, start):
    def body(e, off):
        cnt = run_tbl[tile * N_EXPERTS + e]
        base = base_tbl[tile * N_EXPERTS + e]
        done = jnp.int32(0)
        for lvl in _RUN_LEVELS:
            bit = cnt & lvl

            @pl.when(bit != 0)
            def _(done=done, lvl=lvl):
                cp = make_copy(pl.multiple_of(off + done, RUN_ALIGN), pl.multiple_of(base + done, RUN_ALIGN), lvl)
                if start:
                    cp.start()
                else:
                    cp.wait()
            done = done + bit
        return off + cnt

    lax.fori_loop(0, N_EXPERTS, body, jnp.int32(0))


def _wait_for_tile(run_tbl, tile, make_copy, sb):
    total = lax.fori_loop(0, N_EXPERTS, lambda e, acc: acc + run_tbl[tile * N_EXPERTS + e], jnp.int32(0))
    lvl = 1 << (sb.bit_length() - 1)
    while lvl >= RUN_ALIGN:
        @pl.when((total & lvl) != 0)
        def _(lvl=lvl):
            make_copy(0, 0, lvl).wait()
        lvl //= 2


def _slot_onehot(pos, sb):
    j = lax.broadcasted_iota(I32, (sb, pos.shape[1]), 0)
    hits = [j == pos[k:k + 1, :] for k in range(TOP_K)]
    return hits, (hits[0] | hits[1] | hits[2] | hits[3])


def _sorted_rows(td):
    return _round_up(TOP_K * td + N_EXPERTS * (RUN_ALIGN - 1), LANES)


def _dispatch_body(run_tbl, base_tbl, tail_len, tail_start, h_ref, pos_ref, gate_ref, xbuf_hbm,
                   sorted_ref, zero_ref, sem, zsem, *, d):
    i = pl.program_id(0)
    last = pl.num_programs(0) - 1
    slot = i % 2
    sb = sorted_ref.shape[1]

    def zero_copies(off, base, rows):
        del off
        return pltpu.make_async_copy(zero_ref.at[pl.ds(0, rows)], xbuf_hbm.at[pl.ds(base, rows)], zsem)

    @pl.when(i == 0)
    def _():
        zero_ref[...] = jnp.zeros_like(zero_ref)
        _for_each_run(tail_len, tail_start, 0, zero_copies, True)

    hits, any_hit = _slot_onehot(pos_ref[...], sb)
    perm = jnp.where(any_hit, 1.0, 0.0).astype(BF16)
    sorted_ref[slot, :, :d] = jnp.dot(perm, h_ref[...], preferred_element_type=F32)
    gates = gate_ref[...]
    gsel = jnp.where(hits[0], gates[0:1, :], 0.0)
    for k in range(1, TOP_K):
        gsel = gsel + jnp.where(hits[k], gates[k:k + 1, :], 0.0)
    sorted_ref[slot, :, d:] = jnp.broadcast_to(jnp.sum(gsel, axis=1, keepdims=True), (sb, LANES))

    def copies(slot):
        return lambda off, base, rows: pltpu.make_async_copy(
            sorted_ref.at[slot, pl.ds(off, rows)], xbuf_hbm.at[pl.ds(base, rows)], sem.at[slot])

    _for_each_run(run_tbl, base_tbl, i, copies(slot), True)

    @pl.when(i > 0)
    def _():
        _wait_for_tile(run_tbl, i - 1, copies(1 - slot), sb)

    @pl.when(i == last)
    def _():
        _wait_for_tile(run_tbl, i, copies(slot), sb)
        _for_each_run(tail_len, tail_start, 0, zero_copies, False)


def _dispatch(run_tbl, base_tbl, tail_len, tail_start, h2, pos_t, gate_t, rows):
    n, d = h2.shape
    td = min(ROUTE_TILE, n)
    sb = _sorted_rows(td)
    dx = d + LANES
    assert EXPERT_TILE <= 2 * _RUN_LEVELS[0]
    tok = pl.BlockSpec((TOP_K, td), lambda i, *_: (0, i))
    grid_spec = pltpu.PrefetchScalarGridSpec(
        num_scalar_prefetch=4,
        grid=(n // td,),
        in_specs=[pl.BlockSpec((td, d), lambda i, *_: (i, 0)), tok, tok],
        out_specs=pl.BlockSpec(memory_space=pl.ANY),
        scratch_shapes=[pltpu.VMEM((2, sb, dx), F32), pltpu.VMEM((_RUN_LEVELS[0], dx), F32),
                        pltpu.SemaphoreType.DMA((2,)), pltpu.SemaphoreType.DMA],
    )
    return pl.pallas_call(
        functools.partial(_dispatch_body, d=d),
        grid_spec=grid_spec,
        out_shape=jax.ShapeDtypeStruct((rows, dx), F32),
        compiler_params=_cparams(("arbitrary",)),
        name="dispatch_rows",
    )(run_tbl, base_tbl, tail_len, tail_start, h2, pos_t, gate_t)


def _expert_body(be_ref, nu_ref, x_ref, w1_hbm, b1_ref, w2_hbm, b2_ref, y_ref,
                 w1f_ref, w2f_ref, w1b_ref, w2b_ref, sem, *, layer):
    i = pl.program_id(0)
    n = pl.num_programs(0)
    d_ff, d = w2b_ref.shape

    def fetch(blk):
        e = layer * N_EXPERTS + be_ref[blk]
        return (pltpu.make_async_copy(w1_hbm.at[e], w1f_ref, sem.at[0]),
                pltpu.make_async_copy(w2_hbm.at[e], w2f_ref, sem.at[1]))

    @pl.when(i == 0)
    def _():
        for cp in fetch(0):
            cp.start()

    @pl.when((i == 0) | (be_ref[i] != be_ref[jnp.maximum(i - 1, 0)]))
    def _():
        for cp in fetch(i):
            cp.wait()
        w1b_ref[...] = w1f_ref[...].astype(BF16)
        w2b_ref[...] = w2f_ref[...].astype(BF16)
        nxt = lax.while_loop(lambda j: (j < n) & (be_ref[jnp.minimum(j, n - 1)] == be_ref[i]),
                             lambda j: j + 1, i + 1)

        @pl.when(nxt < n)
        def _():
            for cp in fetch(nxt):
                cp.start()

    @pl.when(i < nu_ref[0])
    def _():
        gu = jnp.dot(x_ref[:, :d].astype(BF16), w1b_ref[...], preferred_element_type=F32) + b1_ref[0]
        gate = jnp.minimum(gu[:, :d_ff], SWIGLU_LIMIT)
        up = jnp.clip(gu[:, d_ff:], -SWIGLU_LIMIT, SWIGLU_LIMIT)
        glu = gate * jax.nn.sigmoid(gate * SWIGLU_ALPHA)
        act = ((up + 1.0) * glu).astype(BF16)
        y = jnp.dot(act, w2b_ref[...], preferred_element_type=F32) + b2_ref[0]
        y_ref[...] = x_ref[:, d:d + 1] * y

    @pl.when(i >= nu_ref[0])
    def _():
        y_ref[...] = jnp.zeros_like(y_ref)


def _experts(layer, block_expert, n_used, xbuf, w1, b1, w2, b2):
    rows, dx = xbuf.shape
    de, d, f2 = w1.shape
    f = w2.shape[1]
    blk = lambda i, be, nu: (jnp.maximum(jnp.minimum(i, nu[0] - 1), 0), 0)
    wsel = lambda i, be, nu: (layer * N_EXPERTS + be[i], 0, 0)
    grid_spec = pltpu.PrefetchScalarGridSpec(
        num_scalar_prefetch=2,
        grid=(rows // EXPERT_TILE,),
        in_specs=[pl.BlockSpec((EXPERT_TILE, dx), blk),
                  pl.BlockSpec(memory_space=pl.ANY),
                  pl.BlockSpec((1, 1, f2), wsel),
                  pl.BlockSpec(memory_space=pl.ANY),
                  pl.BlockSpec((1, 1, d), wsel)],
        out_specs=pl.BlockSpec((EXPERT_TILE, d), lambda i, be, nu: (i, 0)),
        scratch_shapes=[pltpu.VMEM((d, f2), F32), pltpu.VMEM((f, d), F32),
                        pltpu.VMEM((d, f2), BF16), pltpu.VMEM((f, d), BF16),
                        pltpu.SemaphoreType.DMA((2,))],
    )
    return pl.pallas_call(
        functools.partial(_expert_body, layer=layer),
        grid_spec=grid_spec,
        out_shape=jax.ShapeDtypeStruct((rows, d), F32),
        compiler_params=_cparams(("arbitrary",)),
        name="expert_swiglu",
    )(block_expert, n_used, xbuf, w1, b1.reshape(de, 1, f2), w2, b2.reshape(de, 1, d))


def _combine_body(run_tbl, base_tbl, x1_ref, g2_ref, pos_ref, fg_ref, ybuf_hbm, o_ref, ys_ref, sem, *, final):
    i = pl.program_id(0)
    last = pl.num_programs(0) - 1
    slot = i % 2
    sb = ys_ref.shape[1]

    def copies(slot):
        return lambda off, base, rows: pltpu.make_async_copy(
            ybuf_hbm.at[pl.ds(base, rows)], ys_ref.at[slot, pl.ds(off, rows)], sem.at[slot])

    @pl.when(i == 0)
    def _():
        ys_ref[...] = jnp.zeros_like(ys_ref)
        _for_each_run(run_tbl, base_tbl, 0, copies(0), True)

    @pl.when(i < last)
    def _():
        _for_each_run(run_tbl, base_tbl, i + 1, copies(1 - slot), True)

    _wait_for_tile(run_tbl, i, copies(slot), sb)

    _, any_hit = _slot_onehot(pos_ref[...], sb)
    perm = jnp.where(any_hit, 1.0, 0.0).astype(BF16)
    moe = lax.dot_general(perm, ys_ref[slot].astype(BF16), _TN, preferred_element_type=F32)
    out = x1_ref[...] + g2_ref[0] * moe
    if final:
        out = _rms(out) * fg_ref[...]
    o_ref[...] = out


def _combine(run_tbl, base_tbl, x1, g2, pos_t, final_g, ybuf, seq, final):
    n, d = x1.shape
    td = min(ROUTE_TILE, n)
    sb = _sorted_rows(td)
    grid_spec = pltpu.PrefetchScalarGridSpec(
        num_scalar_prefetch=2,
        grid=(n // td,),
        in_specs=[pl.BlockSpec((td, d), lambda i, *_: (i, 0)),
                  pl.BlockSpec((1, 1, d), lambda i, *_: ((i * td) // seq, 0, 0)),
                  pl.BlockSpec((TOP_K, td), lambda i, *_: (0, i)),
                  pl.BlockSpec((1, d), lambda i, *_: (0, 0)),
                  pl.BlockSpec(memory_space=pl.ANY)],
        out_specs=pl.BlockSpec((td, d), lambda i, *_: (i, 0)),
        scratch_shapes=[pltpu.VMEM((2, sb, d), F32), pltpu.SemaphoreType.DMA((2,))],
    )
    return pl.pallas_call(
        functools.partial(_combine_body, final=final),
        grid_spec=grid_spec,
        out_shape=jax.ShapeDtypeStruct((n, d), F32),
        compiler_params=_cparams(("arbitrary",)),
        name="combine_rows",
    )(run_tbl, base_tbl, x1, g2, pos_t, final_g, ybuf)


def _moe(layer, x1, h2, logits_t, g2, w1, b1, w2, b2, final_g, seq, final):
    n, d = h2.shape
    td = min(ROUTE_TILE, n)
    n_tiles = n // td
    assert seq % td == 0
    rows = (_round_up(n * TOP_K + n_tiles * N_EXPERTS * (RUN_ALIGN - 1), EXPERT_TILE)
            + N_EXPERTS * EXPERT_TILE)
    n_blocks = rows // EXPERT_TILE
    pos_t, gate_t, run_t = _route(logits_t)
    base_t, be, nu, tail = _layout(run_t, n_blocks)
    run_tbl = run_t[:, :n_tiles].T.astype(I32).reshape(-1)
    base_tbl = base_t[:, :n_tiles].T.reshape(-1)
    xbuf = _dispatch(run_tbl, base_tbl, tail[:, 1], tail[:, 0], h2, pos_t, gate_t, rows)
    ybuf = _experts(layer, be[0, :n_blocks], nu[0, :1], xbuf, w1, b1, w2, b2)
    return _combine(run_tbl, base_tbl, x1, g2, pos_t, final_g, ybuf, seq, final)


def kernel(x, c, positions, ada_w, ada_b, norm_mix, norm_ffn, w_in, b_forget, lambda_q1, lambda_k1,
           lambda_q2, lambda_k2, diff_subln, w_out, router_w, router_b, exp_w1, exp_b1, exp_w2, exp_b2,
           final_norm):
    b, s, d = x.shape
    depth = ada_w.shape[0]
    mod = _ada_mod(c, ada_w, ada_b)
    cosf, sinf = _rope_tables(positions)
    n_in = w_in.shape[2]
    pad_cols = 3072 + LANES - n_in
    flat = lambda a: a.reshape((a.shape[0] * a.shape[1],) + a.shape[2:])
    w1_all, b1_all, w2_all, b2_all = flat(exp_w1), flat(exp_b1), flat(exp_w2), flat(exp_b2)
    for l in range(depth):
        lam_init = 0.8 - 0.6 * math.exp(-0.3 * l)
        sh1, sc1, g1, sh2, sc2, g2 = [mod[l, :, i * d:(i + 1) * d].reshape(b, 1, d) for i in range(6)]
        w_pad = jnp.pad(w_in[l], ((0, 0), (0, pad_cols))).astype(BF16)
        bf_pad = jnp.pad(b_forget[l], (0, LANES - FOX_HEADS)).reshape(1, LANES)
        dq, dk, dv, fq, fk, fv, flog = _inproj(x, sc1, sh1, norm_mix[l].reshape(1, d), w_pad, bf_pad,
                                               cosf, sinf)
        bias_q, bias_k = _forget_bias_lanes(flog)
        a_out = _diff_attention(dq, dk, dv, lambda_q1[l].reshape(1, -1), lambda_k1[l].reshape(1, -1),
                                lambda_q2[l].reshape(1, -1), lambda_k2[l].reshape(1, -1),
                                diff_subln[l].reshape(1, -1), lam_init)
        b_out = _fox_attention(fq, fk, fv, bias_q, bias_k)
        x1, h2, logits_t = _outproj(x, a_out, b_out, w_out[l].astype(BF16), g1, sc2, sh2,
                                    norm_ffn[l].reshape(1, d), router_w[l].T,
                                    router_b[l].reshape(-1, 1))
        x = _moe(l, x1.reshape(b * s, d), h2, logits_t, g2, w1_all, b1_all, w2_all, b2_all,
                 final_norm.reshape(1, d), s, l == depth - 1).reshape(b, s, d)
    return x
```

```python
import functools
import math

import jax
import jax.numpy as jnp
from jax import lax
from jax.experimental import pallas as pl
from jax.experimental.pallas import tpu as pltpu

F32 = jnp.float32
BF16 = jnp.bfloat16
I32 = jnp.int32
HIGHEST = lax.Precision.HIGHEST

HEAD_DIM = 64
LANES = 128
SUBLANES = 8
CHUNK = 64
DIFF_HEADS = 4
FOX_HEADS = 8
D_DIFF = DIFF_HEADS * 2 * HEAD_DIM
D_FOX = FOX_HEADS * HEAD_DIM
PROJ_GROUPS = 6
PROJ_MAIN_COLS = 3 * D_DIFF + 3 * D_FOX
ROPE_THETA = 500000.0
ROPE_DIM = HEAD_DIM // 4
ROPE_HALF = ROPE_DIM // 2
N_EXPERTS = 32
TOP_K = 4
SWIGLU_LIMIT = 7.0
SWIGLU_ALPHA = 1.702
NORM_EPS = 1e-6
NEG_BIG = -1e30
LOG2E = math.log2(math.e)
ATTN_Q_TILE = 1024
ATTN_K_TILE = 512
EXPERT_TILE = 512
ROUTE_TILE = 256
ROUTE_TILES_PER_STEP = 4
RUN_ALIGN = SUBLANES
ROW_TILE = 512
ADA_COL_TILE = 1536
VMEM_LIMIT = 56 * 1024 * 1024

_NT = (((1,), (1,)), ((), ()))
_TN = (((0,), (0,)), ((), ()))


def _cparams(sem, vmem=None):
    return pltpu.CompilerParams(dimension_semantics=sem, vmem_limit_bytes=vmem or VMEM_LIMIT)


def _rms(x):
    return x * lax.rsqrt(jnp.mean(x * x, axis=-1, keepdims=True) + NORM_EPS)


def _round_up(x, m):
    return (x + m - 1) // m * m


def _attn_tiles(s):
    tq = min(ATTN_Q_TILE, s)
    return tq, min(ATTN_K_TILE, tq)


def _ada_body(c_ref, w_ref, b_ref, o_ref):
    c = c_ref[...]
    ca = c * jax.nn.sigmoid(c)
    o_ref[0] = jnp.dot(ca, w_ref[0], preferred_element_type=F32, precision=HIGHEST) + b_ref[0]


def _ada_mod(c, ada_w, ada_b):
    depth, d, n6 = ada_w.shape
    b = c.shape[0]
    tn = ADA_COL_TILE
    assert n6 % tn == 0
    return pl.pallas_call(
        _ada_body,
        grid=(depth, n6 // tn),
        in_specs=[pl.BlockSpec((b, d), lambda l, j: (0, 0)),
                  pl.BlockSpec((1, d, tn), lambda l, j: (l, 0, j)),
                  pl.BlockSpec((1, 1, tn), lambda l, j: (l, 0, j))],
        out_specs=pl.BlockSpec((1, b, tn), lambda l, j: (l, 0, j)),
        out_shape=jax.ShapeDtypeStruct((depth, b, n6), F32),
        compiler_params=_cparams(("arbitrary", "arbitrary")),
        name="ada_mod",
    )(c, ada_w, ada_b.reshape(depth, 1, n6))


def _rope_body(pos_ref, invf_ref, cos_ref, sin_ref):
    ang = pos_ref[0].astype(F32) * invf_ref[...]
    j = lax.broadcasted_iota(I32, ang.shape, 1) & (HEAD_DIM - 1)
    c = jnp.cos(ang)
    s = jnp.sin(ang)
    cos_ref[0] = jnp.where(j < ROPE_DIM, c, 1.0)
    sin_ref[0] = jnp.where(j < ROPE_HALF, -s, jnp.where(j < ROPE_DIM, s, 0.0))


def _rope_tables(positions):
    b, s = positions.shape
    ts = min(ROW_TILE, s)
    inv_freq = ROPE_THETA ** (-jnp.arange(0, ROPE_DIM, 2, dtype=F32) / ROPE_DIM)
    invf = inv_freq[jnp.arange(LANES) % ROPE_HALF].reshape(1, LANES)
    out = jax.ShapeDtypeStruct((b, s, LANES), F32)
    return pl.pallas_call(
        _rope_body,
        grid=(b, s // ts),
        in_specs=[pl.BlockSpec((1, ts, 1), lambda i, j: (i, j, 0)),
                  pl.BlockSpec((1, LANES), lambda i, j: (0, 0))],
        out_specs=[pl.BlockSpec((1, ts, LANES), lambda i, j: (i, j, 0))] * 2,
        out_shape=[out, out],
        compiler_params=_cparams(("arbitrary", "arbitrary")),
        name="rope_tables",
    )(positions.reshape(b, s, 1), invf)


def _inproj_body(x_ref, sc_ref, sh_ref, g_ref, w_ref, bf_ref, cos_ref, sin_ref,
                 dq_ref, dk_ref, dv_ref, fq_ref, fk_ref, fv_ref, fl_ref):
    h = _rms(x_ref[0]) * g_ref[...] * (1.0 + sc_ref[0]) + sh_ref[0]
    hb = h.astype(BF16)
    cosf = cos_ref[0]
    sinf = sin_ref[0]
    lane = lax.broadcasted_iota(I32, cosf.shape, 1)
    first = (lane & (HEAD_DIM - 1)) < ROPE_HALF

    def rope(p):
        nxt = pltpu.roll(p, LANES - ROPE_HALF, 1)
        prv = pltpu.roll(p, ROPE_HALF, 1)
        return p * cosf + jnp.where(first, nxt, prv) * sinf

    def proj(group):
        return jnp.dot(hb, w_ref[:, group * D_DIFF:(group + 1) * D_DIFF], preferred_element_type=F32)

    scale = HEAD_DIM ** -0.5 * LOG2E
    p = proj(0)
    for c in range(D_DIFF // LANES):
        sl = slice(c * LANES, (c + 1) * LANES)
        dq_ref[0, :, sl] = (rope(p[:, sl]) * scale).astype(BF16)
    p = proj(1)
    for c in range(D_DIFF // LANES):
        sl = slice(c * LANES, (c + 1) * LANES)
        dk_ref[0, :, sl] = rope(p[:, sl]).astype(BF16)
    dv_ref[0] = proj(2).astype(BF16)
    fq_ref[0] = (proj(3) * scale).astype(BF16)
    fk_ref[0] = proj(4).astype(BF16)
    fv_ref[0] = proj(5).astype(BF16)
    z = (jnp.dot(hb, w_ref[:, PROJ_MAIN_COLS:PROJ_MAIN_COLS + LANES], preferred_element_type=F32)
         + bf_ref[...])
    fl_ref[0] = -(jnp.maximum(-z, 0.0) + jnp.log1p(jnp.exp(-jnp.abs(z))))


def _inproj(x, sc, sh, g, w_pad, bf_pad, cosf, sinf):
    b, s, d = x.shape
    tm = min(ROW_TILE, s)
    assert D_DIFF == D_FOX
    act = lambda w: jax.ShapeDtypeStruct((b, s, w), BF16)
    row = lambda w: pl.BlockSpec((1, tm, w), lambda i, j: (i, j, 0))
    vec = pl.BlockSpec((1, 1, d), lambda i, j: (i, 0, 0))
    return pl.pallas_call(
        _inproj_body,
        grid=(b, s // tm),
        in_specs=[row(d), vec, vec,
                  pl.BlockSpec((1, d), lambda i, j: (0, 0)),
                  pl.BlockSpec(w_pad.shape, lambda i, j: (0, 0)),
                  pl.BlockSpec((1, LANES), lambda i, j: (0, 0)),
                  row(LANES), row(LANES)],
        out_specs=[row(D_DIFF)] * PROJ_GROUPS + [row(LANES)],
        out_shape=[act(D_DIFF)] * PROJ_GROUPS + [jax.ShapeDtypeStruct((b, s, LANES), F32)],
        compiler_params=_cparams(("arbitrary", "arbitrary")),
        name="in_proj",
    )(x, sc, sh, g, w_pad, bf_pad, cosf, sinf)


BIAS_LANES = 6


def _split3(c):
    hi = c.astype(BF16).astype(F32)
    r = c - hi
    mid = r.astype(BF16).astype(F32)
    return hi, mid, r - mid


def _cum_body(fl_ref, fq_ref, fk_ref):
    x = fl_ref[0]
    s = x.shape[0]
    row = lax.broadcasted_iota(I32, x.shape, 0)
    d = 1
    while d < s:
        x = x + jnp.where(row >= d, pltpu.roll(x, d, 0), 0.0)
        d *= 2
    pieces = jnp.concatenate(_split3(x * LOG2E), axis=1).astype(BF16)
    r = lax.broadcasted_iota(I32, (3 * LANES, LANES), 0)
    c = lax.broadcasted_iota(I32, (3 * LANES, LANES), 1)
    head, piece = r & (LANES - 1), r >> (LANES.bit_length() - 1)
    to_q = ((c == head * BIAS_LANES + piece) & (head < FOX_HEADS)).astype(BF16)
    to_k = ((c == head * BIAS_LANES + piece + 3) & (head < FOX_HEADS)).astype(BF16)
    lane = lax.broadcasted_iota(I32, (1, LANES), 1)
    ones_q = jnp.zeros((1, LANES), F32)
    ones_k = jnp.zeros((1, LANES), F32)
    for h in range(FOX_HEADS):
        ones_k = jnp.where((lane >= h * BIAS_LANES) & (lane < h * BIAS_LANES + 3), 1.0, ones_k)
        ones_q = jnp.where((lane >= h * BIAS_LANES + 3) & (lane < (h + 1) * BIAS_LANES), 1.0, ones_q)
    fq_ref[0] = jnp.dot(pieces, to_q, preferred_element_type=F32) + ones_q
    fk_ref[0] = ones_k - jnp.dot(pieces, to_k, preferred_element_type=F32)


def _forget_bias_lanes(flog):
    b, s, _ = flog.shape
    spec = pl.BlockSpec((1, s, LANES), lambda i: (i, 0, 0))
    out = jax.ShapeDtypeStruct((b, s, LANES), F32)
    return pl.pallas_call(
        _cum_body,
        grid=(b,),
        in_specs=[spec],
        out_specs=[spec, spec],
        out_shape=[out, out],
        compiler_params=_cparams(("arbitrary",)),
        name="forget_cumsum",
    )(flog)


def _causal_sweep(t, tq, tk, tile, state):
    ratio = tq // tk
    assert ratio in (1, 2) and ratio * tk == tq
    n_full = t * ratio

    def steps(k, base, trips, st):
        def body(i, st):
            for u in range(k):
                st = tile(base + k * i + u, st, False, 0)
            return st
        return lax.fori_loop(0, trips, body, st)

    state = steps(4, 0, n_full // 4, state)
    state = steps(2, (n_full // 4) * 4, (n_full // 2) % 2, state)
    if ratio == 1:
        state = steps(1, (n_full // 2) * 2, n_full % 2, state)
    state = tile(n_full, state, True, 0)
    return state, (tile(n_full + 1, state, True, tk) if ratio == 2 else None)


def _finish_rows(state, lower, tk, finish):
    if lower is None:
        return finish(state)
    top = finish(jax.tree.map(lambda a: a[:tk], state))
    return jnp.concatenate([top, finish(lower)], axis=0)


def _diff_body(q_ref, k_ref, v_ref, lq1_ref, lk1_ref, lq2_ref, lk2_ref, g_ref, o_ref, *, tq, tk, lam_init):
    t = pl.program_id(2)
    q = q_ref[0]
    lane = lax.broadcasted_iota(I32, q.shape, 1)
    zero = jnp.zeros_like(q)
    qs = (jnp.where(lane < HEAD_DIM, q, zero), jnp.where(lane >= HEAD_DIM, q, zero))

    def tile(j, state, masked, lo):
        off = pl.multiple_of(j * tk, tk)
        kt = k_ref[0, pl.ds(off, tk), :]
        vt = v_ref[0, pl.ds(off, tk), :]
        out = []
        for mp in range(2):
            s = lax.dot_general(qs[mp][lo:], kt, _NT, preferred_element_type=F32)
            if masked:
                r = lax.broadcasted_iota(I32, s.shape, 0)
                c = lax.broadcasted_iota(I32, s.shape, 1)
                shift = CHUNK.bit_length() - 1
                s = jnp.where((c >> shift) <= (r >> shift), s, NEG_BIG)
            m, l, acc = state[mp]
            m_new = jnp.maximum(m[lo:], jnp.max(s, axis=1, keepdims=True))
            alpha = jnp.exp2(m[lo:] - m_new)
            p = jnp.exp2(s - m_new[:, :1])
            l_new = alpha * l[lo:] + jnp.sum(p, axis=1, keepdims=True)
            acc_new = alpha * acc[lo:] + jnp.dot(p.astype(BF16), vt, preferred_element_type=F32)
            out.append((m_new, l_new, acc_new))
        return tuple(out)

    lam = (jnp.exp(jnp.sum(lq1_ref[...] * lk1_ref[...], axis=1, keepdims=True))
           - jnp.exp(jnp.sum(lq2_ref[...] * lk2_ref[...], axis=1, keepdims=True)) + lam_init)

    def finish(st):
        (_, l0, a0), (_, l1, a1) = st
        out = a0 / l0 - lam * (a1 / l1)
        return (_rms(out) * g_ref[...] * (1.0 - lam_init)).astype(o_ref.dtype)

    init = (jnp.full((tq, LANES), NEG_BIG, F32), jnp.zeros((tq, LANES), F32), jnp.zeros((tq, LANES), F32))
    o_ref[0] = _finish_rows(*_causal_sweep(t, tq, tk, tile, (init, init)), tk, finish)


def _diff_attention(dq, dk, dv, lq1, lk1, lq2, lk2, subln, lam_init):
    b, s, _ = dq.shape
    tq, tk = _attn_tiles(s)
    qspec = pl.BlockSpec((1, tq, LANES), lambda i, h, t: (i, t, h))
    kvspec = pl.BlockSpec((1, s, LANES), lambda i, h, t: (i, 0, h))
    lspec = pl.BlockSpec((1, HEAD_DIM), lambda i, h, t: (0, 0))
    return pl.pallas_call(
        functools.partial(_diff_body, tq=tq, tk=tk, lam_init=lam_init),
        grid=(b, DIFF_HEADS, s // tq),
        in_specs=[qspec, kvspec, kvspec, lspec, lspec, lspec, lspec,
                  pl.BlockSpec((1, LANES), lambda i, h, t: (0, 0))],
        out_specs=qspec,
        out_shape=jax.ShapeDtypeStruct((b, s, D_DIFF), BF16),
        compiler_params=_cparams(("arbitrary", "arbitrary", "arbitrary")),
        name="diff_attention",
    )(dq, dk, dv, lq1, lk1, lq2, lk2, subln)


def _fox_operand(x_f32, bias, hh, h, lane):
    xh = x_f32 if hh == 0 else pltpu.roll(x_f32, HEAD_DIM, 1)
    bh = pltpu.roll(bias, HEAD_DIM - BIAS_LANES * h, 1)
    return jnp.where(lane < HEAD_DIM, xh, jnp.where(lane < HEAD_DIM + BIAS_LANES, bh, 0.0)).astype(BF16)


def _fox_body(q_ref, k_ref, v_ref, cq_ref, ck_ref, o_ref, ka_ref, va_ref, *, tq, tk):
    pair = pl.program_id(1)
    t = pl.program_id(2)
    heads = (2 * pair, 2 * pair + 1)
    lane = lax.broadcasted_iota(I32, (tq, LANES), 1)
    klane = lax.broadcasted_iota(I32, (tk, LANES), 1)

    @pl.when(t == 0)
    def _():
        def chunk(ci, carry):
            rows = pl.ds(pl.multiple_of(ci * tk, tk), tk)
            kf = k_ref[0, rows, :].astype(F32)
            vf = v_ref[0, rows, :].astype(F32)
            bias = ck_ref[0, rows, :]
            for hh in range(2):
                ka_ref[hh, rows, :] = _fox_operand(kf, bias, hh, heads[hh], klane)
                vh = vf if hh == 0 else pltpu.roll(vf, HEAD_DIM, 1)
                va_ref[hh, rows, :] = jnp.where(klane < HEAD_DIM, vh, 1.0).astype(BF16)
            return carry

        lax.fori_loop(0, k_ref.shape[1] // tk, chunk, 0)

    qf = q_ref[0].astype(F32)
    qs = [_fox_operand(qf, cq_ref[0], hh, heads[hh], lane) for hh in range(2)]

    def tile(j, state, masked, lo):
        off = pl.multiple_of(j * tk, tk)
        out = []
        for hh in range(2):
            s = lax.dot_general(qs[hh][lo:], ka_ref[hh, pl.ds(off, tk), :], _NT, preferred_element_type=F32)
            if masked:
                r = lax.broadcasted_iota(I32, s.shape, 0)
                c = lax.broadcasted_iota(I32, s.shape, 1)
                s = jnp.where(c <= r, s, NEG_BIG)
            m, acc = state[hh]
            m_new = jnp.maximum(m[lo:], jnp.max(s, axis=1, keepdims=True))
            p = jnp.exp2(s - m_new[:, :1]).astype(BF16)
            acc_new = jnp.exp2(m[lo:] - m_new) * acc[lo:] + jnp.dot(p, va_ref[hh, pl.ds(off, tk), :],
                                                                    preferred_element_type=F32)
            out.append((m_new, acc_new))
        return tuple(out)

    def finish(st):
        (_, a0), (_, a1) = st
        o0 = a0 / pltpu.roll(a0, HEAD_DIM, 1)
        o1 = a1 / pltpu.roll(a1, HEAD_DIM, 1)
        first = lax.broadcasted_iota(I32, a0.shape, 1) < HEAD_DIM
        return jnp.where(first, o0, pltpu.roll(o1, HEAD_DIM, 1)).astype(o_ref.dtype)

    init = (jnp.full((tq, LANES), NEG_BIG, F32), jnp.zeros((tq, LANES), F32))
    o_ref[0] = _finish_rows(*_causal_sweep(t, tq, tk, tile, (init, init)), tk, finish)


def _fox_attention(fq, fk, fv, bias_q, bias_k):
    b, s, _ = fq.shape
    tq, tk = _attn_tiles(s)
    qspec = pl.BlockSpec((1, tq, LANES), lambda i, p, t: (i, t, p))
    kvspec = pl.BlockSpec((1, s, LANES), lambda i, p, t: (i, 0, p))
    return pl.pallas_call(
        functools.partial(_fox_body, tq=tq, tk=tk),
        grid=(b, FOX_HEADS // 2, s // tq),
        in_specs=[qspec, kvspec, kvspec,
                  pl.BlockSpec((1, tq, LANES), lambda i, p, t: (i, t, 0)),
                  pl.BlockSpec((1, s, LANES), lambda i, p, t: (i, 0, 0))],
        out_specs=qspec,
        out_shape=jax.ShapeDtypeStruct((b, s, D_FOX), BF16),
        scratch_shapes=[pltpu.VMEM((2, s, LANES), BF16), pltpu.VMEM((2, s, LANES), BF16)],
        compiler_params=_cparams(("arbitrary", "arbitrary", "arbitrary")),
        name="fox_attention",
    )(fq, fk, fv, bias_q, bias_k)


def _outproj_body(x_ref, a_ref, b_ref, w_ref, g1_ref, sc_ref, sh_ref, ng_ref, rw_ref, rb_ref,
                  x1_ref, h2_ref, lg_ref):
    mix = (jnp.dot(a_ref[0], w_ref[:D_DIFF, :], preferred_element_type=F32)
           + jnp.dot(b_ref[0], w_ref[D_DIFF:, :], preferred_element_type=F32))
    x1 = x_ref[0] + g1_ref[0] * mix
    x1_ref[0] = x1
    h2 = _rms(x1) * ng_ref[...] * (1.0 + sc_ref[0]) + sh_ref[0]
    h_hi = h2.astype(BF16)
    h2_ref[...] = h_hi
    h_lo = (h2 - h_hi.astype(F32)).astype(BF16)
    rw = rw_ref[...]
    w_hi = rw.astype(BF16)
    w_lo = (rw - w_hi.astype(F32)).astype(BF16)
    nt = lambda a, b: lax.dot_general(a, b, _NT, preferred_element_type=F32)
    lg_ref[...] = nt(w_hi, h_hi) + (nt(w_hi, h_lo) + nt(w_lo, h_hi)) + rb_ref[...]


def _outproj(x, a_out, b_out, w_out, g1, sc2, sh2, ng, rwt, rb):
    b, s, d = x.shape
    n = b * s
    tm = min(ROW_TILE, s)
    nt = s // tm
    row = lambda w: pl.BlockSpec((1, tm, w), lambda i, j: (i, j, 0))
    vec = pl.BlockSpec((1, 1, d), lambda i, j: (i, 0, 0))
    const = lambda shp: pl.BlockSpec(shp, lambda i, j: (0,) * len(shp))
    return pl.pallas_call(
        _outproj_body,
        grid=(b, nt),
        in_specs=[row(d), row(D_DIFF), row(D_FOX), const(w_out.shape), vec, vec, vec,
                  const((1, d)), const(rwt.shape), const(rb.shape)],
        out_specs=[row(d),
                   pl.BlockSpec((tm, d), lambda i, j: (i * nt + j, 0)),
                   pl.BlockSpec((N_EXPERTS, tm), lambda i, j: (0, i * nt + j))],
        out_shape=[jax.ShapeDtypeStruct((b, s, d), F32),
                   jax.ShapeDtypeStruct((n, d), BF16),
                   jax.ShapeDtypeStruct((N_EXPERTS, n), F32)],
        compiler_params=_cparams(("arbitrary", "arbitrary")),
        name="out_proj",
    )(x, a_out, b_out, w_out, g1, sc2, sh2, ng, rwt, rb)


def _strict_lower(n):
    r = lax.broadcasted_iota(I32, (n, n), 0)
    c = lax.broadcasted_iota(I32, (n, n), 1)
    return (c < r).astype(F32)


def _align_up(x_f32, m):
    shift = m.bit_length() - 1
    return (((x_f32.astype(I32) + (m - 1)) >> shift) << shift).astype(F32)


def _route_body(lg_ref, pos_ref, gate_ref, cnt_ref, *, td):
    @pl.when(pl.program_id(0) == 0)
    def _():
        cnt_ref[...] = jnp.zeros_like(cnt_ref)

    per_step = lg_ref.shape[1] // td
    for u in range(per_step):
        cols = slice(u * td, (u + 1) * td)
        _route_tile(pl.program_id(0) * per_step + u, lg_ref[:, cols], pos_ref.at[:, cols], gate_ref.at[:, cols],
                    cnt_ref)


def _route_tile(i, work, pos_ref, gate_ref, cnt_ref):
    td = work.shape[1]
    eidx = lax.broadcasted_iota(I32, work.shape, 0)
    vals, hots = [], []
    for k in range(TOP_K):
        m = jnp.max(work, axis=0, keepdims=True)
        sel = jnp.min(jnp.where(work == m, eidx, N_EXPERTS), axis=0, keepdims=True)
        hot = eidx == sel
        vals.append(m)
        hots.append(hot)
        work = jnp.where(hot, -jnp.inf, work)
    ex = [jnp.exp(v - vals[0]) for v in vals]
    den = ex[0] + ex[1] + ex[2] + ex[3]
    for k in range(TOP_K):
        gate_ref[k:k + 1, :] = ex[k] / den

    chosen = hots[0] | hots[1] | hots[2] | hots[3]
    r = lax.broadcasted_iota(I32, (td, td), 0)
    c = lax.broadcasted_iota(I32, (td, td), 1)
    before = (r < c).astype(BF16)
    earlier = jnp.dot(chosen.astype(BF16), before, preferred_element_type=F32)
    cnt = jnp.sum(chosen.astype(F32), axis=1, keepdims=True)
    run = _align_up(cnt, RUN_ALIGN)
    start = jnp.dot(_strict_lower(N_EXPERTS), jnp.broadcast_to(run, (N_EXPERTS, LANES)),
                    preferred_element_type=F32, precision=HIGHEST)[:, 0:1]
    slot = start + earlier
    for k in range(TOP_K):
        pos_ref[k:k + 1, :] = jnp.sum(jnp.where(hots[k], slot, 0.0), axis=0, keepdims=True).astype(I32)
    lane = lax.broadcasted_iota(I32, cnt_ref.shape, 1)
    cnt_ref[...] = jnp.where(lane == i, run, cnt_ref[...])


def _route(logits_t):
    e, n = logits_t.shape
    td = min(ROUTE_TILE, n)
    ntp = _round_up(n // td, LANES)
    step = math.gcd(n, ROUTE_TILES_PER_STEP * td)
    tok = pl.BlockSpec((TOP_K, step), lambda i: (0, i))
    return pl.pallas_call(
        functools.partial(_route_body, td=td),
        grid=(n // step,),
        in_specs=[pl.BlockSpec((e, step), lambda i: (0, i))],
        out_specs=[tok, tok, pl.BlockSpec((e, ntp), lambda i: (0, 0))],
        out_shape=[jax.ShapeDtypeStruct((TOP_K, n), I32),
                   jax.ShapeDtypeStruct((TOP_K, n), F32),
                   jax.ShapeDtypeStruct((e, ntp), F32)],
        compiler_params=_cparams(("arbitrary",)),
        name="route_topk",
    )(logits_t)


def _layout_body(run_ref, base_ref, be_ref, nu_ref, tail_ref):
    run = run_ref[...]
    ntp = run.shape[1]
    total = jnp.sum(run, axis=1, keepdims=True)
    region = _align_up(total, EXPERT_TILE)
    pstart = jnp.dot(_strict_lower(N_EXPERTS), jnp.broadcast_to(region, (N_EXPERTS, LANES)),
                     preferred_element_type=F32, precision=HIGHEST)[:, 0:1]
    ti = lax.broadcasted_iota(I32, (ntp, ntp), 0)
    tj = lax.broadcasted_iota(I32, (ntp, ntp), 1)
    within = jnp.dot(run, (ti < tj).astype(F32), preferred_element_type=F32, precision=HIGHEST)
    base_ref[...] = (pstart + within).astype(I32)
    pend = pstart + region
    blk = (lax.broadcasted_iota(I32, (N_EXPERTS, be_ref.shape[1]), 1) * EXPERT_TILE).astype(F32)
    be = jnp.sum((pend <= blk).astype(I32), axis=0, keepdims=True)
    be_ref[...] = jnp.minimum(be, N_EXPERTS - 1)
    used = jnp.sum(region, axis=0, keepdims=True).astype(I32) >> (EXPERT_TILE.bit_length() - 1)
    nu_ref[...] = jnp.broadcast_to(used, nu_ref.shape)
    lane = lax.broadcasted_iota(I32, tail_ref.shape, 1)
    tail_ref[...] = jnp.where(lane == 0, pstart + total, jnp.where(lane == 1, region - total, 0.0)).astype(I32)


def _layout(run_t, n_blocks):
    e, ntp = run_t.shape
    nbp = _round_up(n_blocks, LANES)
    full = lambda shp: pl.BlockSpec(shp, lambda: (0,) * len(shp))
    return pl.pallas_call(
        _layout_body,
        in_specs=[full((e, ntp))],
        out_specs=[full((e, ntp)), full((1, nbp)), full((1, LANES)), full((e, LANES))],
        out_shape=[jax.ShapeDtypeStruct((e, ntp), I32),
                   jax.ShapeDtypeStruct((1, nbp), I32),
                   jax.ShapeDtypeStruct((1, LANES), I32),
                   jax.ShapeDtypeStruct((e, LANES), I32)],
        name="expert_layout",
    )(run_t)


_RUN_LEVELS = tuple(1 << b for b in range(ROUTE_TILE.bit_length() - 1, RUN_ALIGN.bit_length() - 2, -1))


def _for_each_run(run_tbl, base_tbl, tile, make_copy, start):
    def body(e, off):
        cnt = run_tbl[tile * N_EXPERTS + e]
        base = base_tbl[tile * N_EXPERTS + e]
        done = jnp.int32(0)
        for lvl in _RUN_LEVELS:
            bit = cnt & lvl

            @pl.when(bit != 0)
            def _(done=done, lvl=lvl):
                cp = make_copy(pl.multiple_of(off + done, RUN_ALIGN), pl.multiple_of(base + done, RUN_ALIGN), lvl)
                if start:
                    cp.start()
                else:
                    cp.wait()
            done = done + bit
        return off + cnt

    lax.fori_loop(0, N_EXPERTS, body, jnp.int32(0))


def _wait_for_tile(run_tbl, tile, make_copy, sb):
    total = lax.fori_loop(0, N_EXPERTS, lambda e, acc: acc + run_tbl[tile * N_EXPERTS + e], jnp.int32(0))
    lvl = 1 << (sb.bit_length() - 1)
    while lvl >= RUN_ALIGN:
        @pl.when((total & lvl) != 0)
        def _(lvl=lvl):
            make_copy(0, 0, lvl).wait()
        lvl //= 2


def _slot_onehot(pos, sb):
    j = lax.broadcasted_iota(I32, (sb, pos.shape[1]), 0)
    hits = [j == pos[k:k + 1, :] for k in range(TOP_K)]
    return hits, (hits[0] | hits[1] | hits[2] | hits[3])


def _sorted_rows(td):
    return _round_up(TOP_K * td + N_EXPERTS * (RUN_ALIGN - 1), LANES)


def _dispatch_body(run_tbl, base_tbl, tail_len, tail_start, h_ref, pos_ref, gate_ref, xbuf_hbm,
                   sorted_ref, zero_ref, sem, zsem, *, d):
    i = pl.program_id(0)
    last = pl.num_programs(0) - 1
    slot = i % 2
    sb = sorted_ref.shape[1]

    def zero_copies(off, base, rows):
        del off
        return pltpu.make_async_copy(zero_ref.at[pl.ds(0, rows)], xbuf_hbm.at[pl.ds(base, rows)], zsem)

    @pl.when(i == 0)
    def _():
        zero_ref[...] = jnp.zeros_like(zero_ref)
        _for_each_run(tail_len, tail_start, 0, zero_copies, True)

    hits, any_hit = _slot_onehot(pos_ref[...], sb)
    perm = jnp.where(any_hit, 1.0, 0.0).astype(BF16)
    sorted_ref[slot, :, :d] = jnp.dot(perm, h_ref[...], preferred_element_type=F32)
    gates = gate_ref[...]
    gsel = jnp.where(hits[0], gates[0:1, :], 0.0)
    for k in range(1, TOP_K):
        gsel = gsel + jnp.where(hits[k], gates[k:k + 1, :], 0.0)
    sorted_ref[slot, :, d:] = jnp.broadcast_to(jnp.sum(gsel, axis=1, keepdims=True), (sb, LANES))

    def copies(slot):
        return lambda off, base, rows: pltpu.make_async_copy(
            sorted_ref.at[slot, pl.ds(off, rows)], xbuf_hbm.at[pl.ds(base, rows)], sem.at[slot])

    _for_each_run(run_tbl, base_tbl, i, copies(slot), True)

    @pl.when(i > 0)
    def _():
        _wait_for_tile(run_tbl, i - 1, copies(1 - slot), sb)

    @pl.when(i == last)
    def _():
        _wait_for_tile(run_tbl, i, copies(slot), sb)
        _for_each_run(tail_len, tail_start, 0, zero_copies, False)


def _dispatch(run_tbl, base_tbl, tail_len, tail_start, h2, pos_t, gate_t, rows):
    n, d = h2.shape
    td = min(ROUTE_TILE, n)
    sb = _sorted_rows(td)
    dx = d + LANES
    assert EXPERT_TILE <= 2 * _RUN_LEVELS[0]
    tok = pl.BlockSpec((TOP_K, td), lambda i, *_: (0, i))
    grid_spec = pltpu.PrefetchScalarGridSpec(
        num_scalar_prefetch=4,
        grid=(n // td,),
        in_specs=[pl.BlockSpec((td, d), lambda i, *_: (i, 0)), tok, tok],
        out_specs=pl.BlockSpec(memory_space=pl.ANY),
        scratch_shapes=[pltpu.VMEM((2, sb, dx), F32), pltpu.VMEM((_RUN_LEVELS[0], dx), F32),
                        pltpu.SemaphoreType.DMA((2,)), pltpu.SemaphoreType.DMA],
    )
    return pl.pallas_call(
        functools.partial(_dispatch_body, d=d),
        grid_spec=grid_spec,
        out_shape=jax.ShapeDtypeStruct((rows, dx), F32),
        compiler_params=_cparams(("arbitrary",)),
        name="dispatch_rows",
    )(run_tbl, base_tbl, tail_len, tail_start, h2, pos_t, gate_t)


def _expert_body(be_ref, nu_ref, x_ref, w1_hbm, b1_ref, w2_hbm, b2_ref, y_ref,
                 w1f_ref, w2f_ref, w1b_ref, w2b_ref, sem, *, layer):
    i = pl.program_id(0)
    n = pl.num_programs(0)
    d_ff, d = w2b_ref.shape

    def fetch(blk):
        e = layer * N_EXPERTS + be_ref[blk]
        return (pltpu.make_async_copy(w1_hbm.at[e], w1f_ref, sem.at[0]),
                pltpu.make_async_copy(w2_hbm.at[e], w2f_ref, sem.at[1]))

    @pl.when(i == 0)
    def _():
        for cp in fetch(0):
            cp.start()

    @pl.when((i == 0) | (be_ref[i] != be_ref[jnp.maximum(i - 1, 0)]))
    def _():
        for cp in fetch(i):
            cp.wait()
        w1b_ref[...] = w1f_ref[...].astype(BF16)
        w2b_ref[...] = w2f_ref[...].astype(BF16)
        nxt = lax.while_loop(lambda j: (j < n) & (be_ref[jnp.minimum(j, n - 1)] == be_ref[i]),
                             lambda j: j + 1, i + 1)

        @pl.when(nxt < n)
        def _():
            for cp in fetch(nxt):
                cp.start()

    @pl.when(i < nu_ref[0])
    def _():
        gu = jnp.dot(x_ref[:, :d].astype(BF16), w1b_ref[...], preferred_element_type=F32) + b1_ref[0]
        gate = jnp.minimum(gu[:, :d_ff], SWIGLU_LIMIT)
        up = jnp.clip(gu[:, d_ff:], -SWIGLU_LIMIT, SWIGLU_LIMIT)
        glu = gate * jax.nn.sigmoid(gate * SWIGLU_ALPHA)
        act = ((up + 1.0) * glu).astype(BF16)
        y = jnp.dot(act, w2b_ref[...], preferred_element_type=F32) + b2_ref[0]
        y_ref[...] = x_ref[:, d:d + 1] * y

    @pl.when(i >= nu_ref[0])
    def _():
        y_ref[...] = jnp.zeros_like(y_ref)


def _experts(layer, block_expert, n_used, xbuf, w1, b1, w2, b2):
    rows, dx = xbuf.shape
    de, d, f2 = w1.shape
    f = w2.shape[1]
    blk = lambda i, be, nu: (jnp.maximum(jnp.minimum(i, nu[0] - 1), 0), 0)
    wsel = lambda i, be, nu: (layer * N_EXPERTS + be[i], 0, 0)
    grid_spec = pltpu.PrefetchScalarGridSpec(
        num_scalar_prefetch=2,
        grid=(rows // EXPERT_TILE,),
        in_specs=[pl.BlockSpec((EXPERT_TILE, dx), blk),
                  pl.BlockSpec(memory_space=pl.ANY),
                  pl.BlockSpec((1, 1, f2), wsel),
                  pl.BlockSpec(memory_space=pl.ANY),
                  pl.BlockSpec((1, 1, d), wsel)],
        out_specs=pl.BlockSpec((EXPERT_TILE, d), lambda i, be, nu: (i, 0)),
        scratch_shapes=[pltpu.VMEM((d, f2), F32), pltpu.VMEM((f, d), F32),
                        pltpu.VMEM((d, f2), BF16), pltpu.VMEM((f, d), BF16),
                        pltpu.SemaphoreType.DMA((2,))],
    )
    return pl.pallas_call(
        functools.partial(_expert_body, layer=layer),
        grid_spec=grid_spec,
        out_shape=jax.ShapeDtypeStruct((rows, d), F32),
        compiler_params=_cparams(("arbitrary",)),
        name="expert_swiglu",
    )(block_expert, n_used, xbuf, w1, b1.reshape(de, 1, f2), w2, b2.reshape(de, 1, d))


def _combine_body(run_tbl, base_tbl, x1_ref, g2_ref, pos_ref, fg_ref, ybuf_hbm, o_ref, ys_ref, sem, *, final):
    i = pl.program_id(0)
    last = pl.num_programs(0) - 1
    slot = i % 2
    sb = ys_ref.shape[1]

    def copies(slot):
        return lambda off, base, rows: pltpu.make_async_copy(
            ybuf_hbm.at[pl.ds(base, rows)], ys_ref.at[slot, pl.ds(off, rows)], sem.at[slot])

    @pl.when(i == 0)
    def _():
        ys_ref[...] = jnp.zeros_like(ys_ref)
        _for_each_run(run_tbl, base_tbl, 0, copies(0), True)

    @pl.when(i < last)
    def _():
        _for_each_run(run_tbl, base_tbl, i + 1, copies(1 - slot), True)

    _wait_for_tile(run_tbl, i, copies(slot), sb)

    _, any_hit = _slot_onehot(pos_ref[...], sb)
    perm = jnp.where(any_hit, 1.0, 0.0).astype(BF16)
    moe = lax.dot_general(perm, ys_ref[slot].astype(BF16), _TN, preferred_element_type=F32)
    out = x1_ref[...] + g2_ref[0] * moe
    if final:
        out = _rms(out) * fg_ref[...]
    o_ref[...] = out


def _combine(run_tbl, base_tbl, x1, g2, pos_t, final_g, ybuf, seq, final):
    n, d = x1.shape
    td = min(ROUTE_TILE, n)
    sb = _sorted_rows(td)
    grid_spec = pltpu.PrefetchScalarGridSpec(
        num_scalar_prefetch=2,
        grid=(n // td,),
        in_specs=[pl.BlockSpec((td, d), lambda i, *_: (i, 0)),
                  pl.BlockSpec((1, 1, d), lambda i, *_: ((i * td) // seq, 0, 0)),
                  pl.BlockSpec((TOP_K, td), lambda i, *_: (0, i)),
                  pl.BlockSpec((1, d), lambda i, *_: (0, 0)),
                  pl.BlockSpec(memory_space=pl.ANY)],
        out_specs=pl.BlockSpec((td, d), lambda i, *_: (i, 0)),
        scratch_shapes=[pltpu.VMEM((2, sb, d), F32), pltpu.SemaphoreType.DMA((2,))],
    )
    return pl.pallas_call(
        functools.partial(_combine_body, final=final),
        grid_spec=grid_spec,
        out_shape=jax.ShapeDtypeStruct((n, d), F32),
        compiler_params=_cparams(("arbitrary",)),
        name="combine_rows",
    )(run_tbl, base_tbl, x1, g2, pos_t, final_g, ybuf)


def _moe(layer, x1, h2, logits_t, g2, w1, b1, w2, b2, final_g, seq, final):
    n, d = h2.shape
    td = min(ROUTE_TILE, n)
    n_tiles = n // td
    assert seq % td == 0
    rows = (_round_up(n * TOP_K + n_tiles * N_EXPERTS * (RUN_ALIGN - 1), EXPERT_TILE)
            + N_EXPERTS * EXPERT_TILE)
    n_blocks = rows // EXPERT_TILE
    pos_t, gate_t, run_t = _route(logits_t)
    base_t, be, nu, tail = _layout(run_t, n_blocks)
    run_tbl = run_t[:, :n_tiles].T.astype(I32).reshape(-1)
    base_tbl = base_t[:, :n_tiles].T.reshape(-1)
    xbuf = _dispatch(run_tbl, base_tbl, tail[:, 1], tail[:, 0], h2, pos_t, gate_t, rows)
    ybuf = _experts(layer, be[0, :n_blocks], nu[0, :1], xbuf, w1, b1, w2, b2)
    return _combine(run_tbl, base_tbl, x1, g2, pos_t, final_g, ybuf, seq, final)


def kernel(x, c, positions, ada_w, ada_b, norm_mix, norm_ffn, w_in, b_forget, lambda_q1, lambda_k1,
           lambda_q2, lambda_k2, diff_subln, w_out, router_w, router_b, exp_w1, exp_b1, exp_w2, exp_b2,
           final_norm):
    b, s, d = x.shape
    depth = ada_w.shape[0]
    mod = _ada_mod(c, ada_w, ada_b)
    cosf, sinf = _rope_tables(positions)
    n_in = w_in.shape[2]
    assert n_in == PROJ_MAIN_COLS + FOX_HEADS
    pad_cols = PROJ_MAIN_COLS + LANES - n_in
    flat = lambda a: a.reshape((a.shape[0] * a.shape[1],) + a.shape[2:])
    w1_all, b1_all, w2_all, b2_all = flat(exp_w1), flat(exp_b1), flat(exp_w2), flat(exp_b2)
    for l in range(depth):
        lam_init = 0.8 - 0.6 * math.exp(-0.3 * l)
        sh1, sc1, g1, sh2, sc2, g2 = [mod[l, :, i * d:(i + 1) * d].reshape(b, 1, d) for i in range(6)]
        w_pad = jnp.pad(w_in[l], ((0, 0), (0, pad_cols))).astype(BF16)
        bf_pad = jnp.pad(b_forget[l], (0, LANES - FOX_HEADS)).reshape(1, LANES)
        dq, dk, dv, fq, fk, fv, flog = _inproj(x, sc1, sh1, norm_mix[l].reshape(1, d), w_pad, bf_pad,
                                               cosf, sinf)
        bias_q, bias_k = _forget_bias_lanes(flog)
        a_out = _diff_attention(dq, dk, dv, lambda_q1[l].reshape(1, -1), lambda_k1[l].reshape(1, -1),
                                lambda_q2[l].reshape(1, -1), lambda_k2[l].reshape(1, -1),
                                diff_subln[l].reshape(1, -1), lam_init)
        b_out = _fox_attention(fq, fk, fv, bias_q, bias_k)
        x1, h2, logits_t = _outproj(x, a_out, b_out, w_out[l].astype(BF16), g1, sc2, sh2,
                                    norm_ffn[l].reshape(1, d), router_w[l].T,
                                    router_b[l].reshape(-1, 1))
        x = _moe(l, x1.reshape(b * s, d), h2, logits_t, g2, w1_all, b1_all, w2_all, b2_all,
                 final_norm.reshape(1, d), s, l == depth - 1).reshape(b, s, d)
    return x
```

```python
import functools
import math

import jax
import jax.numpy as jnp
from jax import lax
from jax.experimental import pallas as pl
from jax.experimental.pallas import tpu as pltpu

F32 = jnp.float32
BF16 = jnp.bfloat16
I32 = jnp.int32
HIGHEST = lax.Precision.HIGHEST

HEAD_DIM = 64
LANES = 128
SUBLANES = 8
CHUNK = 64
DIFF_HEADS = 4
FOX_HEADS = 8
D_DIFF = DIFF_HEADS * 2 * HEAD_DIM
D_FOX = FOX_HEADS * HEAD_DIM
PROJ_GROUPS = 6
PROJ_MAIN_COLS = 3 * D_DIFF + 3 * D_FOX
ROPE_THETA = 500000.0
ROPE_DIM = HEAD_DIM // 4
ROPE_HALF = ROPE_DIM // 2
N_EXPERTS = 32
TOP_K = 4
SWIGLU_LIMIT = 7.0
SWIGLU_ALPHA = 1.702
NORM_EPS = 1e-6
NEG_BIG = -1e30
LOG2E = math.log2(math.e)
ATTN_Q_TILE = 1024
ATTN_K_TILE = 512
EXPERT_TILE = 512
ROUTE_TILE = 256
ROUTE_TILES_PER_STEP = 4
RUN_ALIGN = SUBLANES
ROW_TILE = 1024
ADA_COL_TILE = 1536
VMEM_LIMIT = 56 * 1024 * 1024

_NT = (((1,), (1,)), ((), ()))
_TN = (((0,), (0,)), ((), ()))


def _cparams(sem, vmem=None):
    return pltpu.CompilerParams(dimension_semantics=sem, vmem_limit_bytes=vmem or VMEM_LIMIT)


def _rms(x):
    return x * lax.rsqrt(jnp.mean(x * x, axis=-1, keepdims=True) + NORM_EPS)


def _round_up(x, m):
    return (x + m - 1) // m * m


def _attn_tiles(s):
    tq = min(ATTN_Q_TILE, s)
    return tq, min(ATTN_K_TILE, tq)


def _ada_body(c_ref, w_ref, b_ref, o_ref):
    c = c_ref[...]
    ca = c * jax.nn.sigmoid(c)
    o_ref[0] = jnp.dot(ca, w_ref[0], preferred_element_type=F32, precision=HIGHEST) + b_ref[0]


def _ada_mod(c, ada_w, ada_b):
    depth, d, n6 = ada_w.shape
    b = c.shape[0]
    tn = ADA_COL_TILE
    assert n6 % tn == 0
    return pl.pallas_call(
        _ada_body,
        grid=(depth, n6 // tn),
        in_specs=[pl.BlockSpec((b, d), lambda l, j: (0, 0)),
                  pl.BlockSpec((1, d, tn), lambda l, j: (l, 0, j)),
                  pl.BlockSpec((1, 1, tn), lambda l, j: (l, 0, j))],
        out_specs=pl.BlockSpec((1, b, tn), lambda l, j: (l, 0, j)),
        out_shape=jax.ShapeDtypeStruct((depth, b, n6), F32),
        compiler_params=_cparams(("arbitrary", "arbitrary")),
        name="ada_mod",
    )(c, ada_w, ada_b.reshape(depth, 1, n6))


def _rope_body(pos_ref, invf_ref, cos_ref, sin_ref):
    ang = pos_ref[0].astype(F32) * invf_ref[...]
    j = lax.broadcasted_iota(I32, ang.shape, 1) & (HEAD_DIM - 1)
    c = jnp.cos(ang)
    s = jnp.sin(ang)
    cos_ref[0] = jnp.where(j < ROPE_DIM, c, 1.0)
    sin_ref[0] = jnp.where(j < ROPE_HALF, -s, jnp.where(j < ROPE_DIM, s, 0.0))


def _rope_tables(positions):
    b, s = positions.shape
    ts = min(ROW_TILE, s)
    inv_freq = ROPE_THETA ** (-jnp.arange(0, ROPE_DIM, 2, dtype=F32) / ROPE_DIM)
    invf = inv_freq[jnp.arange(LANES) % ROPE_HALF].reshape(1, LANES)
    out = jax.ShapeDtypeStruct((b, s, LANES), F32)
    return pl.pallas_call(
        _rope_body,
        grid=(b, s // ts),
        in_specs=[pl.BlockSpec((1, ts, 1), lambda i, j: (i, j, 0)),
                  pl.BlockSpec((1, LANES), lambda i, j: (0, 0))],
        out_specs=[pl.BlockSpec((1, ts, LANES), lambda i, j: (i, j, 0))] * 2,
        out_shape=[out, out],
        compiler_params=_cparams(("arbitrary", "arbitrary")),
        name="rope_tables",
    )(positions.reshape(b, s, 1), invf)


def _inproj_body(x_ref, sc_ref, sh_ref, g_ref, w_ref, bf_ref, cos_ref, sin_ref,
                 dq_ref, dk_ref, dv_ref, fq_ref, fk_ref, fv_ref, fl_ref):
    h = _rms(x_ref[0]) * g_ref[...] * (1.0 + sc_ref[0]) + sh_ref[0]
    hb = h.astype(BF16)
    cosf = cos_ref[0]
    sinf = sin_ref[0]
    lane = lax.broadcasted_iota(I32, cosf.shape, 1)
    first = (lane & (HEAD_DIM - 1)) < ROPE_HALF

    def rope(p):
        nxt = pltpu.roll(p, LANES - ROPE_HALF, 1)
        prv = pltpu.roll(p, ROPE_HALF, 1)
        return p * cosf + jnp.where(first, nxt, prv) * sinf

    def proj(group):
        return jnp.dot(hb, w_ref[:, group * D_DIFF:(group + 1) * D_DIFF], preferred_element_type=F32)

    scale = HEAD_DIM ** -0.5 * LOG2E
    p = proj(0)
    for c in range(D_DIFF // LANES):
        sl = slice(c * LANES, (c + 1) * LANES)
        dq_ref[0, :, sl] = (rope(p[:, sl]) * scale).astype(BF16)
    p = proj(1)
    for c in range(D_DIFF // LANES):
        sl = slice(c * LANES, (c + 1) * LANES)
        dk_ref[0, :, sl] = rope(p[:, sl]).astype(BF16)
    dv_ref[0] = proj(2).astype(BF16)
    fq_ref[0] = (proj(3) * scale).astype(BF16)
    fk_ref[0] = proj(4).astype(BF16)
    fv_ref[0] = proj(5).astype(BF16)
    z = (jnp.dot(hb, w_ref[:, PROJ_MAIN_COLS:PROJ_MAIN_COLS + LANES], preferred_element_type=F32)
         + bf_ref[...])
    fl_ref[0] = -(jnp.maximum(-z, 0.0) + jnp.log1p(jnp.exp(-jnp.abs(z))))


def _inproj(x, sc, sh, g, w_pad, bf_pad, cosf, sinf):
    b, s, d = x.shape
    tm = min(ROW_TILE, s)
    assert D_DIFF == D_FOX
    act = lambda w: jax.ShapeDtypeStruct((b, s, w), BF16)
    row = lambda w: pl.BlockSpec((1, tm, w), lambda i, j: (i, j, 0))
    vec = pl.BlockSpec((1, 1, d), lambda i, j: (i, 0, 0))
    return pl.pallas_call(
        _inproj_body,
        grid=(b, s // tm),
        in_specs=[row(d), vec, vec,
                  pl.BlockSpec((1, d), lambda i, j: (0, 0)),
                  pl.BlockSpec(w_pad.shape, lambda i, j: (0, 0)),
                  pl.BlockSpec((1, LANES), lambda i, j: (0, 0)),
                  row(LANES), row(LANES)],
        out_specs=[row(D_DIFF)] * PROJ_GROUPS + [row(LANES)],
        out_shape=[act(D_DIFF)] * PROJ_GROUPS + [jax.ShapeDtypeStruct((b, s, LANES), F32)],
        compiler_params=_cparams(("arbitrary", "arbitrary")),
        name="in_proj",
    )(x, sc, sh, g, w_pad, bf_pad, cosf, sinf)


BIAS_LANES = 6


def _split3(c):
    hi = c.astype(BF16).astype(F32)
    r = c - hi
    mid = r.astype(BF16).astype(F32)
    return hi, mid, r - mid


def _cum_body(fl_ref, fq_ref, fk_ref):
    x = fl_ref[0]
    s = x.shape[0]
    row = lax.broadcasted_iota(I32, x.shape, 0)
    d = 1
    while d < s:
        x = x + jnp.where(row >= d, pltpu.roll(x, d, 0), 0.0)
        d *= 2
    pieces = jnp.concatenate(_split3(x * LOG2E), axis=1).astype(BF16)
    r = lax.broadcasted_iota(I32, (3 * LANES, LANES), 0)
    c = lax.broadcasted_iota(I32, (3 * LANES, LANES), 1)
    head, piece = r & (LANES - 1), r >> (LANES.bit_length() - 1)
    to_q = ((c == head * BIAS_LANES + piece) & (head < FOX_HEADS)).astype(BF16)
    to_k = ((c == head * BIAS_LANES + piece + 3) & (head < FOX_HEADS)).astype(BF16)
    lane = lax.broadcasted_iota(I32, (1, LANES), 1)
    ones_q = jnp.zeros((1, LANES), F32)
    ones_k = jnp.zeros((1, LANES), F32)
    for h in range(FOX_HEADS):
        ones_k = jnp.where((lane >= h * BIAS_LANES) & (lane < h * BIAS_LANES + 3), 1.0, ones_k)
        ones_q = jnp.where((lane >= h * BIAS_LANES + 3) & (lane < (h + 1) * BIAS_LANES), 1.0, ones_q)
    fq_ref[0] = jnp.dot(pieces, to_q, preferred_element_type=F32) + ones_q
    fk_ref[0] = ones_k - jnp.dot(pieces, to_k, preferred_element_type=F32)


def _forget_bias_lanes(flog):
    b, s, _ = flog.shape
    spec = pl.BlockSpec((1, s, LANES), lambda i: (i, 0, 0))
    out = jax.ShapeDtypeStruct((b, s, LANES), F32)
    return pl.pallas_call(
        _cum_body,
        grid=(b,),
        in_specs=[spec],
        out_specs=[spec, spec],
        out_shape=[out, out],
        compiler_params=_cparams(("arbitrary",)),
        name="forget_cumsum",
    )(flog)


def _causal_sweep(t, tq, tk, tile, state):
    ratio = tq // tk
    assert ratio in (1, 2) and ratio * tk == tq
    n_full = t * ratio

    def steps(k, base, trips, st):
        def body(i, st):
            for u in range(k):
                st = tile(base + k * i + u, st, False, 0)
            return st
        return lax.fori_loop(0, trips, body, st)

    state = steps(4, 0, n_full // 4, state)
    state = steps(2, (n_full // 4) * 4, (n_full // 2) % 2, state)
    if ratio == 1:
        state = steps(1, (n_full // 2) * 2, n_full % 2, state)
    state = tile(n_full, state, True, 0)
    return state, (tile(n_full + 1, state, True, tk) if ratio == 2 else None)


def _finish_rows(state, lower, tk, finish):
    if lower is None:
        return finish(state)
    top = finish(jax.tree.map(lambda a: a[:tk], state))
    return jnp.concatenate([top, finish(lower)], axis=0)


def _diff_body(q_ref, k_ref, v_ref, lq1_ref, lk1_ref, lq2_ref, lk2_ref, g_ref, o_ref, *, tq, tk, lam_init):
    t = pl.program_id(2)
    q = q_ref[0]
    lane = lax.broadcasted_iota(I32, q.shape, 1)
    zero = jnp.zeros_like(q)
    qs = (jnp.where(lane < HEAD_DIM, q, zero), jnp.where(lane >= HEAD_DIM, q, zero))

    def tile(j, state, masked, lo):
        off = pl.multiple_of(j * tk, tk)
        kt = k_ref[0, pl.ds(off, tk), :]
        vt = v_ref[0, pl.ds(off, tk), :]
        out = []
        for mp in range(2):
            s = lax.dot_general(qs[mp][lo:], kt, _NT, preferred_element_type=F32)
            if masked:
                r = lax.broadcasted_iota(I32, s.shape, 0)
                c = lax.broadcasted_iota(I32, s.shape, 1)
                shift = CHUNK.bit_length() - 1
                s = jnp.where((c >> shift) <= (r >> shift), s, NEG_BIG)
            m, l, acc = state[mp]
            m_new = jnp.maximum(m[lo:], jnp.max(s, axis=1, keepdims=True))
            alpha = jnp.exp2(m[lo:] - m_new)
            p = jnp.exp2(s - m_new[:, :1])
            l_new = alpha * l[lo:] + jnp.sum(p, axis=1, keepdims=True)
            acc_new = alpha * acc[lo:] + jnp.dot(p.astype(BF16), vt, preferred_element_type=F32)
            out.append((m_new, l_new, acc_new))
        return tuple(out)

    lam = (jnp.exp(jnp.sum(lq1_ref[...] * lk1_ref[...], axis=1, keepdims=True))
           - jnp.exp(jnp.sum(lq2_ref[...] * lk2_ref[...], axis=1, keepdims=True)) + lam_init)

    def finish(st):
        (_, l0, a0), (_, l1, a1) = st
        out = a0 / l0 - lam * (a1 / l1)
        return (_rms(out) * g_ref[...] * (1.0 - lam_init)).astype(o_ref.dtype)

    init = (jnp.full((tq, LANES), NEG_BIG, F32), jnp.zeros((tq, LANES), F32), jnp.zeros((tq, LANES), F32))
    o_ref[0] = _finish_rows(*_causal_sweep(t, tq, tk, tile, (init, init)), tk, finish)


def _diff_attention(dq, dk, dv, lq1, lk1, lq2, lk2, subln, lam_init):
    b, s, _ = dq.shape
    tq, tk = _attn_tiles(s)
    qspec = pl.BlockSpec((1, tq, LANES), lambda i, h, t: (i, t, h))
    kvspec = pl.BlockSpec((1, s, LANES), lambda i, h, t: (i, 0, h))
    lspec = pl.BlockSpec((1, HEAD_DIM), lambda i, h, t: (0, 0))
    return pl.pallas_call(
        functools.partial(_diff_body, tq=tq, tk=tk, lam_init=lam_init),
        grid=(b, DIFF_HEADS, s // tq),
        in_specs=[qspec, kvspec, kvspec, lspec, lspec, lspec, lspec,
                  pl.BlockSpec((1, LANES), lambda i, h, t: (0, 0))],
        out_specs=qspec,
        out_shape=jax.ShapeDtypeStruct((b, s, D_DIFF), BF16),
        compiler_params=_cparams(("arbitrary", "arbitrary", "arbitrary")),
        name="diff_attention",
    )(dq, dk, dv, lq1, lk1, lq2, lk2, subln)


def _fox_operand(x_f32, bias, hh, h, lane):
    xh = x_f32 if hh == 0 else pltpu.roll(x_f32, HEAD_DIM, 1)
    bh = pltpu.roll(bias, HEAD_DIM - BIAS_LANES * h, 1)
    return jnp.where(lane < HEAD_DIM, xh, jnp.where(lane < HEAD_DIM + BIAS_LANES, bh, 0.0)).astype(BF16)


def _fox_body(q_ref, k_ref, v_ref, cq_ref, ck_ref, o_ref, ka_ref, va_ref, *, tq, tk):
    pair = pl.program_id(1)
    t = pl.program_id(2)
    heads = (2 * pair, 2 * pair + 1)
    lane = lax.broadcasted_iota(I32, (tq, LANES), 1)
    klane = lax.broadcasted_iota(I32, (tk, LANES), 1)

    @pl.when(t == 0)
    def _():
        def chunk(ci, carry):
            rows = pl.ds(pl.multiple_of(ci * tk, tk), tk)
            kf = k_ref[0, rows, :].astype(F32)
            vf = v_ref[0, rows, :].astype(F32)
            bias = ck_ref[0, rows, :]
            for hh in range(2):
                ka_ref[hh, rows, :] = _fox_operand(kf, bias, hh, heads[hh], klane)
                vh = vf if hh == 0 else pltpu.roll(vf, HEAD_DIM, 1)
                va_ref[hh, rows, :] = jnp.where(klane < HEAD_DIM, vh, 1.0).astype(BF16)
            return carry

        lax.fori_loop(0, k_ref.shape[1] // tk, chunk, 0)

    qf = q_ref[0].astype(F32)
    qs = [_fox_operand(qf, cq_ref[0], hh, heads[hh], lane) for hh in range(2)]

    def tile(j, state, masked, lo):
        off = pl.multiple_of(j * tk, tk)
        out = []
        for hh in range(2):
            s = lax.dot_general(qs[hh][lo:], ka_ref[hh, pl.ds(off, tk), :], _NT, preferred_element_type=F32)
            if masked:
                r = lax.broadcasted_iota(I32, s.shape, 0)
                c = lax.broadcasted_iota(I32, s.shape, 1)
                s = jnp.where(c <= r, s, NEG_BIG)
            m, acc = state[hh]
            m_new = jnp.maximum(m[lo:], jnp.max(s, axis=1, keepdims=True))
            p = jnp.exp2(s - m_new[:, :1]).astype(BF16)
            acc_new = jnp.exp2(m[lo:] - m_new) * acc[lo:] + jnp.dot(p, va_ref[hh, pl.ds(off, tk), :],
                                                                    preferred_element_type=F32)
            out.append((m_new, acc_new))
        return tuple(out)

    def finish(st):
        (_, a0), (_, a1) = st
        o0 = a0 / pltpu.roll(a0, HEAD_DIM, 1)
        o1 = a1 / pltpu.roll(a1, HEAD_DIM, 1)
        first = lax.broadcasted_iota(I32, a0.shape, 1) < HEAD_DIM
        return jnp.where(first, o0, pltpu.roll(o1, HEAD_DIM, 1)).astype(o_ref.dtype)

    init = (jnp.full((tq, LANES), NEG_BIG, F32), jnp.zeros((tq, LANES), F32))
    o_ref[0] = _finish_rows(*_causal_sweep(t, tq, tk, tile, (init, init)), tk, finish)


def _fox_attention(fq, fk, fv, bias_q, bias_k):
    b, s, _ = fq.shape
    tq, tk = _attn_tiles(s)
    qspec = pl.BlockSpec((1, tq, LANES), lambda i, p, t: (i, t, p))
    kvspec = pl.BlockSpec((1, s, LANES), lambda i, p, t: (i, 0, p))
    return pl.pallas_call(
        functools.partial(_fox_body, tq=tq, tk=tk),
        grid=(b, FOX_HEADS // 2, s // tq),
        in_specs=[qspec, kvspec, kvspec,
                  pl.BlockSpec((1, tq, LANES), lambda i, p, t: (i, t, 0)),
                  pl.BlockSpec((1, s, LANES), lambda i, p, t: (i, 0, 0))],
        out_specs=qspec,
        out_shape=jax.ShapeDtypeStruct((b, s, D_FOX), BF16),
        scratch_shapes=[pltpu.VMEM((2, s, LANES), BF16), pltpu.VMEM((2, s, LANES), BF16)],
        compiler_params=_cparams(("arbitrary", "arbitrary", "arbitrary")),
        name="fox_attention",
    )(fq, fk, fv, bias_q, bias_k)


def _outproj_body(x_ref, a_ref, b_ref, w_ref, g1_ref, sc_ref, sh_ref, ng_ref, rw_ref, rb_ref,
                  x1_ref, h2_ref, lg_ref):
    mix = (jnp.dot(a_ref[0], w_ref[:D_DIFF, :], preferred_element_type=F32)
           + jnp.dot(b_ref[0], w_ref[D_DIFF:, :], preferred_element_type=F32))
    x1 = x_ref[0] + g1_ref[0] * mix
    x1_ref[0] = x1
    h2 = _rms(x1) * ng_ref[...] * (1.0 + sc_ref[0]) + sh_ref[0]
    h_hi = h2.astype(BF16)
    h2_ref[...] = h_hi
    h_lo = (h2 - h_hi.astype(F32)).astype(BF16)
    rw = rw_ref[...]
    w_hi = rw.astype(BF16)
    w_lo = (rw - w_hi.astype(F32)).astype(BF16)
    nt = lambda a, b: lax.dot_general(a, b, _NT, preferred_element_type=F32)
    lg_ref[...] = nt(w_hi, h_hi) + (nt(w_hi, h_lo) + nt(w_lo, h_hi)) + rb_ref[...]


def _outproj(x, a_out, b_out, w_out, g1, sc2, sh2, ng, rwt, rb):
    b, s, d = x.shape
    n = b * s
    tm = min(ROW_TILE, s)
    nt = s // tm
    row = lambda w: pl.BlockSpec((1, tm, w), lambda i, j: (i, j, 0))
    vec = pl.BlockSpec((1, 1, d), lambda i, j: (i, 0, 0))
    const = lambda shp: pl.BlockSpec(shp, lambda i, j: (0,) * len(shp))
    return pl.pallas_call(
        _outproj_body,
        grid=(b, nt),
        in_specs=[row(d), row(D_DIFF), row(D_FOX), const(w_out.shape), vec, vec, vec,
                  const((1, d)), const(rwt.shape), const(rb.shape)],
        out_specs=[row(d),
                   pl.BlockSpec((tm, d), lambda i, j: (i * nt + j, 0)),
                   pl.BlockSpec((N_EXPERTS, tm), lambda i, j: (0, i * nt + j))],
        out_shape=[jax.ShapeDtypeStruct((b, s, d), F32),
                   jax.ShapeDtypeStruct((n, d), BF16),
                   jax.ShapeDtypeStruct((N_EXPERTS, n), F32)],
        compiler_params=_cparams(("arbitrary", "arbitrary")),
        name="out_proj",
    )(x, a_out, b_out, w_out, g1, sc2, sh2, ng, rwt, rb)


def _strict_lower(n):
    r = lax.broadcasted_iota(I32, (n, n), 0)
    c = lax.broadcasted_iota(I32, (n, n), 1)
    return (c < r).astype(F32)


def _align_up(x_f32, m):
    shift = m.bit_length() - 1
    return (((x_f32.astype(I32) + (m - 1)) >> shift) << shift).astype(F32)


def _route_body(lg_ref, pos_ref, gate_ref, cnt_ref, *, td):
    @pl.when(pl.program_id(0) == 0)
    def _():
        cnt_ref[...] = jnp.zeros_like(cnt_ref)

    per_step = lg_ref.shape[1] // td
    for u in range(per_step):
        cols = slice(u * td, (u + 1) * td)
        _route_tile(pl.program_id(0) * per_step + u, lg_ref[:, cols], pos_ref.at[:, cols], gate_ref.at[:, cols],
                    cnt_ref)


def _route_tile(i, work, pos_ref, gate_ref, cnt_ref):
    td = work.shape[1]
    eidx = lax.broadcasted_iota(I32, work.shape, 0)
    vals, hots = [], []
    for k in range(TOP_K):
        m = jnp.max(work, axis=0, keepdims=True)
        sel = jnp.min(jnp.where(work == m, eidx, N_EXPERTS), axis=0, keepdims=True)
        hot = eidx == sel
        vals.append(m)
        hots.append(hot)
        work = jnp.where(hot, -jnp.inf, work)
    ex = [jnp.exp(v - vals[0]) for v in vals]
    den = ex[0] + ex[1] + ex[2] + ex[3]
    for k in range(TOP_K):
        gate_ref[k:k + 1, :] = ex[k] / den

    chosen = hots[0] | hots[1] | hots[2] | hots[3]
    r = lax.broadcasted_iota(I32, (td, td), 0)
    c = lax.broadcasted_iota(I32, (td, td), 1)
    before = (r < c).astype(BF16)
    earlier = jnp.dot(chosen.astype(BF16), before, preferred_element_type=F32)
    cnt = jnp.sum(chosen.astype(F32), axis=1, keepdims=True)
    run = _align_up(cnt, RUN_ALIGN)
    start = jnp.dot(_strict_lower(N_EXPERTS), jnp.broadcast_to(run, (N_EXPERTS, LANES)),
                    preferred_element_type=F32, precision=HIGHEST)[:, 0:1]
    slot = start + earlier
    for k in range(TOP_K):
        pos_ref[k:k + 1, :] = jnp.sum(jnp.where(hots[k], slot, 0.0), axis=0, keepdims=True).astype(I32)
    lane = lax.broadcasted_iota(I32, cnt_ref.shape, 1)
    cnt_ref[...] = jnp.where(lane == i, run, cnt_ref[...])


def _route(logits_t):
    e, n = logits_t.shape
    td = min(ROUTE_TILE, n)
    ntp = _round_up(n // td, LANES)
    step = math.gcd(n, ROUTE_TILES_PER_STEP * td)
    tok = pl.BlockSpec((TOP_K, step), lambda i: (0, i))
    return pl.pallas_call(
        functools.partial(_route_body, td=td),
        grid=(n // step,),
        in_specs=[pl.BlockSpec((e, step), lambda i: (0, i))],
        out_specs=[tok, tok, pl.BlockSpec((e, ntp), lambda i: (0, 0))],
        out_shape=[jax.ShapeDtypeStruct((TOP_K, n), I32),
                   jax.ShapeDtypeStruct((TOP_K, n), F32),
                   jax.ShapeDtypeStruct((e, ntp), F32)],
        compiler_params=_cparams(("arbitrary",)),
        name="route_topk",
    )(logits_t)


def _layout_body(run_ref, base_ref, be_ref, nu_ref, tail_ref):
    run = run_ref[...]
    ntp = run.shape[1]
    total = jnp.sum(run, axis=1, keepdims=True)
    region = _align_up(total, EXPERT_TILE)
    pstart = jnp.dot(_strict_lower(N_EXPERTS), jnp.broadcast_to(region, (N_EXPERTS, LANES)),
                     preferred_element_type=F32, precision=HIGHEST)[:, 0:1]
    ti = lax.broadcasted_iota(I32, (ntp, ntp), 0)
    tj = lax.broadcasted_iota(I32, (ntp, ntp), 1)
    within = jnp.dot(run, (ti < tj).astype(F32), preferred_element_type=F32, precision=HIGHEST)
    base_ref[...] = (pstart + within).astype(I32)
    pend = pstart + region
    blk = (lax.broadcasted_iota(I32, (N_EXPERTS, be_ref.shape[1]), 1) * EXPERT_TILE).astype(F32)
    be = jnp.sum((pend <= blk).astype(I32), axis=0, keepdims=True)
    be_ref[...] = jnp.minimum(be, N_EXPERTS - 1)
    used = jnp.sum(region, axis=0, keepdims=True).astype(I32) >> (EXPERT_TILE.bit_length() - 1)
    nu_ref[...] = jnp.broadcast_to(used, nu_ref.shape)
    lane = lax.broadcasted_iota(I32, tail_ref.shape, 1)
    tail_ref[...] = jnp.where(lane == 0, pstart + total, jnp.where(lane == 1, region - total, 0.0)).astype(I32)


def _layout(run_t, n_blocks):
    e, ntp = run_t.shape
    nbp = _round_up(n_blocks, LANES)
    full = lambda shp: pl.BlockSpec(shp, lambda: (0,) * len(shp))
    return pl.pallas_call(
        _layout_body,
        in_specs=[full((e, ntp))],
        out_specs=[full((e, ntp)), full((1, nbp)), full((1, LANES)), full((e, LANES))],
        out_shape=[jax.ShapeDtypeStruct((e, ntp), I32),
                   jax.ShapeDtypeStruct((1, nbp), I32),
                   jax.ShapeDtypeStruct((1, LANES), I32),
                   jax.ShapeDtypeStruct((e, LANES), I32)],
        name="expert_layout",
    )(run_t)


_RUN_LEVELS = tuple(1 << b for b in range(ROUTE_TILE.bit_length() - 1, RUN_ALIGN.bit_length() - 2, -1))


def _for_each_run(run_tbl, base_tbl, tile, make_copy, start):
    def body(e, off):
        cnt = run_tbl[tile * N_EXPERTS + e]
        base = base_tbl[tile * N_EXPERTS + e]
        done = jnp.int32(0)
        for lvl in _RUN_LEVELS:
            bit = cnt & lvl

            @pl.when(bit != 0)
            def _(done=done, lvl=lvl):
                cp = make_copy(pl.multiple_of(off + done, RUN_ALIGN), pl.multiple_of(base + done, RUN_ALIGN), lvl)
                if start:
                    cp.start()
                else:
                    cp.wait()
            done = done + bit
        return off + cnt

    lax.fori_loop(0, N_EXPERTS, body, jnp.int32(0))


def _wait_for_tile(run_tbl, tile, make_copy, sb):
    total = lax.fori_loop(0, N_EXPERTS, lambda e, acc: acc + run_tbl[tile * N_EXPERTS + e], jnp.int32(0))
    lvl = 1 << (sb.bit_length() - 1)
    while lvl >= RUN_ALIGN:
        @pl.when((total & lvl) != 0)
        def _(lvl=lvl):
            make_copy(0, 0, lvl).wait()
        lvl //= 2


def _slot_onehot(pos, sb):
    j = lax.broadcasted_iota(I32, (sb, pos.shape[1]), 0)
    hits = [j == pos[k:k + 1, :] for k in range(TOP_K)]
    return hits, (hits[0] | hits[1] | hits[2] | hits[3])


def _sorted_rows(td):
    return _round_up(TOP_K * td + N_EXPERTS * (RUN_ALIGN - 1), LANES)


def _dispatch_body(run_tbl, base_tbl, tail_len, tail_start, h_ref, pos_ref, gate_ref, xbuf_hbm,
                   sorted_ref, zero_ref, sem, zsem, *, d):
    i = pl.program_id(0)
    last = pl.num_programs(0) - 1
    slot = i % 2
    sb = sorted_ref.shape[1]

    def zero_copies(off, base, rows):
        del off
        return pltpu.make_async_copy(zero_ref.at[pl.ds(0, rows)], xbuf_hbm.at[pl.ds(base, rows)], zsem)

    @pl.when(i == 0)
    def _():
        zero_ref[...] = jnp.zeros_like(zero_ref)
        _for_each_run(tail_len, tail_start, 0, zero_copies, True)

    hits, any_hit = _slot_onehot(pos_ref[...], sb)
    perm = jnp.where(any_hit, 1.0, 0.0).astype(BF16)
    sorted_ref[slot, :, :d] = jnp.dot(perm, h_ref[...], preferred_element_type=F32)
    gates = gate_ref[...]
    gsel = jnp.where(hits[0], gates[0:1, :], 0.0)
    for k in range(1, TOP_K):
        gsel = gsel + jnp.where(hits[k], gates[k:k + 1, :], 0.0)
    sorted_ref[slot, :, d:] = jnp.broadcast_to(jnp.sum(gsel, axis=1, keepdims=True), (sb, LANES))

    def copies(slot):
        return lambda off, base, rows: pltpu.make_async_copy(
            sorted_ref.at[slot, pl.ds(off, rows)], xbuf_hbm.at[pl.ds(base, rows)], sem.at[slot])

    _for_each_run(run_tbl, base_tbl, i, copies(slot), True)

    @pl.when(i > 0)
    def _():
        _wait_for_tile(run_tbl, i - 1, copies(1 - slot), sb)

    @pl.when(i == last)
    def _():
        _wait_for_tile(run_tbl, i, copies(slot), sb)
        _for_each_run(tail_len, tail_start, 0, zero_copies, False)


def _dispatch(run_tbl, base_tbl, tail_len, tail_start, h2, pos_t, gate_t, rows):
    n, d = h2.shape
    td = min(ROUTE_TILE, n)
    sb = _sorted_rows(td)
    dx = d + LANES
    assert EXPERT_TILE <= 2 * _RUN_LEVELS[0]
    tok = pl.BlockSpec((TOP_K, td), lambda i, *_: (0, i))
    grid_spec = pltpu.PrefetchScalarGridSpec(
        num_scalar_prefetch=4,
        grid=(n // td,),
        in_specs=[pl.BlockSpec((td, d), lambda i, *_: (i, 0)), tok, tok],
        out_specs=pl.BlockSpec(memory_space=pl.ANY),
        scratch_shapes=[pltpu.VMEM((2, sb, dx), F32), pltpu.VMEM((_RUN_LEVELS[0], dx), F32),
                        pltpu.SemaphoreType.DMA((2,)), pltpu.SemaphoreType.DMA],
    )
    return pl.pallas_call(
        functools.partial(_dispatch_body, d=d),
        grid_spec=grid_spec,
        out_shape=jax.ShapeDtypeStruct((rows, dx), F32),
        compiler_params=_cparams(("arbitrary",)),
        name="dispatch_rows",
    )(run_tbl, base_tbl, tail_len, tail_start, h2, pos_t, gate_t)


def _expert_body(be_ref, nu_ref, x_ref, w1_hbm, b1_ref, w2_hbm, b2_ref, y_ref,
                 w1f_ref, w2f_ref, w1b_ref, w2b_ref, sem, *, layer):
    i = pl.program_id(0)
    n = pl.num_programs(0)
    d_ff, d = w2b_ref.shape

    def fetch(blk):
        e = layer * N_EXPERTS + be_ref[blk]
        return (pltpu.make_async_copy(w1_hbm.at[e], w1f_ref, sem.at[0]),
                pltpu.make_async_copy(w2_hbm.at[e], w2f_ref, sem.at[1]))

    @pl.when(i == 0)
    def _():
        for cp in fetch(0):
            cp.start()

    @pl.when((i == 0) | (be_ref[i] != be_ref[jnp.maximum(i - 1, 0)]))
    def _():
        for cp in fetch(i):
            cp.wait()
        w1b_ref[...] = w1f_ref[...].astype(BF16)
        w2b_ref[...] = w2f_ref[...].astype(BF16)
        nxt = lax.while_loop(lambda j: (j < n) & (be_ref[jnp.minimum(j, n - 1)] == be_ref[i]),
                             lambda j: j + 1, i + 1)

        @pl.when(nxt < n)
        def _():
            for cp in fetch(nxt):
                cp.start()

    @pl.when(i < nu_ref[0])
    def _():
        gu = jnp.dot(x_ref[:, :d].astype(BF16), w1b_ref[...], preferred_element_type=F32) + b1_ref[0]
        gate = jnp.minimum(gu[:, :d_ff], SWIGLU_LIMIT)
        up = jnp.clip(gu[:, d_ff:], -SWIGLU_LIMIT, SWIGLU_LIMIT)
        glu = gate * jax.nn.sigmoid(gate * SWIGLU_ALPHA)
        act = ((up + 1.0) * glu).astype(BF16)
        y = jnp.dot(act, w2b_ref[...], preferred_element_type=F32) + b2_ref[0]
        y_ref[...] = x_ref[:, d:d + 1] * y

    @pl.when(i >= nu_ref[0])
    def _():
        y_ref[...] = jnp.zeros_like(y_ref)


def _experts(layer, block_expert, n_used, xbuf, w1, b1, w2, b2):
    rows, dx = xbuf.shape
    de, d, f2 = w1.shape
    f = w2.shape[1]
    blk = lambda i, be, nu: (jnp.maximum(jnp.minimum(i, nu[0] - 1), 0), 0)
    wsel = lambda i, be, nu: (layer * N_EXPERTS + be[i], 0, 0)
    grid_spec = pltpu.PrefetchScalarGridSpec(
        num_scalar_prefetch=2,
        grid=(rows // EXPERT_TILE,),
        in_specs=[pl.BlockSpec((EXPERT_TILE, dx), blk),
                  pl.BlockSpec(memory_space=pl.ANY),
                  pl.BlockSpec((1, 1, f2), wsel),
                  pl.BlockSpec(memory_space=pl.ANY),
                  pl.BlockSpec((1, 1, d), wsel)],
        out_specs=pl.BlockSpec((EXPERT_TILE, d), lambda i, be, nu: (i, 0)),
        scratch_shapes=[pltpu.VMEM((d, f2), F32), pltpu.VMEM((f, d), F32),
                        pltpu.VMEM((d, f2), BF16), pltpu.VMEM((f, d), BF16),
                        pltpu.SemaphoreType.DMA((2,))],
    )
    return pl.pallas_call(
        functools.partial(_expert_body, layer=layer),
        grid_spec=grid_spec,
        out_shape=jax.ShapeDtypeStruct((rows, d), F32),
        compiler_params=_cparams(("arbitrary",)),
        name="expert_swiglu",
    )(block_expert, n_used, xbuf, w1, b1.reshape(de, 1, f2), w2, b2.reshape(de, 1, d))


def _combine_body(run_tbl, base_tbl, x1_ref, g2_ref, pos_ref, fg_ref, ybuf_hbm, o_ref, ys_ref, sem, *, final):
    i = pl.program_id(0)
    last = pl.num_programs(0) - 1
    slot = i % 2
    sb = ys_ref.shape[1]

    def copies(slot):
        return lambda off, base, rows: pltpu.make_async_copy(
            ybuf_hbm.at[pl.ds(base, rows)], ys_ref.at[slot, pl.ds(off, rows)], sem.at[slot])

    @pl.when(i == 0)
    def _():
        ys_ref[...] = jnp.zeros_like(ys_ref)
        _for_each_run(run_tbl, base_tbl, 0, copies(0), True)

    @pl.when(i < last)
    def _():
        _for_each_run(run_tbl, base_tbl, i + 1, copies(1 - slot), True)

    _wait_for_tile(run_tbl, i, copies(slot), sb)

    _, any_hit = _slot_onehot(pos_ref[...], sb)
    perm = jnp.where(any_hit, 1.0, 0.0).astype(BF16)
    moe = lax.dot_general(perm, ys_ref[slot].astype(BF16), _TN, preferred_element_type=F32)
    out = x1_ref[...] + g2_ref[0] * moe
    if final:
        out = _rms(out) * fg_ref[...]
    o_ref[...] = out


def _combine(run_tbl, base_tbl, x1, g2, pos_t, final_g, ybuf, seq, final):
    n, d = x1.shape
    td = min(ROUTE_TILE, n)
    sb = _sorted_rows(td)
    grid_spec = pltpu.PrefetchScalarGridSpec(
        num_scalar_prefetch=2,
        grid=(n // td,),
        in_specs=[pl.BlockSpec((td, d), lambda i, *_: (i, 0)),
                  pl.BlockSpec((1, 1, d), lambda i, *_: ((i * td) // seq, 0, 0)),
                  pl.BlockSpec((TOP_K, td), lambda i, *_: (0, i)),
                  pl.BlockSpec((1, d), lambda i, *_: (0, 0)),
                  pl.BlockSpec(memory_space=pl.ANY)],
        out_specs=pl.BlockSpec((td, d), lambda i, *_: (i, 0)),
        scratch_shapes=[pltpu.VMEM((2, sb, d), F32), pltpu.SemaphoreType.DMA((2,))],
    )
    return pl.pallas_call(
        functools.partial(_combine_body, final=final),
        grid_spec=grid_spec,
        out_shape=jax.ShapeDtypeStruct((n, d), F32),
        compiler_params=_cparams(("arbitrary",)),
        name="combine_rows",
    )(run_tbl, base_tbl, x1, g2, pos_t, final_g, ybuf)


def _moe(layer, x1, h2, logits_t, g2, w1, b1, w2, b2, final_g, seq, final):
    n, d = h2.shape
    td = min(ROUTE_TILE, n)
    n_tiles = n // td
    assert seq % td == 0
    rows = (_round_up(n * TOP_K + n_tiles * N_EXPERTS * (RUN_ALIGN - 1), EXPERT_TILE)
            + N_EXPERTS * EXPERT_TILE)
    n_blocks = rows // EXPERT_TILE
    pos_t, gate_t, run_t = _route(logits_t)
    base_t, be, nu, tail = _layout(run_t, n_blocks)
    run_tbl = run_t[:, :n_tiles].T.astype(I32).reshape(-1)
    base_tbl = base_t[:, :n_tiles].T.reshape(-1)
    xbuf = _dispatch(run_tbl, base_tbl, tail[:, 1], tail[:, 0], h2, pos_t, gate_t, rows)
    ybuf = _experts(layer, be[0, :n_blocks], nu[0, :1], xbuf, w1, b1, w2, b2)
    return _combine(run_tbl, base_tbl, x1, g2, pos_t, final_g, ybuf, seq, final)


def kernel(x, c, positions, ada_w, ada_b, norm_mix, norm_ffn, w_in, b_forget, lambda_q1, lambda_k1,
           lambda_q2, lambda_k2, diff_subln, w_out, router_w, router_b, exp_w1, exp_b1, exp_w2, exp_b2,
           final_norm):
    b, s, d = x.shape
    depth = ada_w.shape[0]
    mod = _ada_mod(c, ada_w, ada_b)
    cosf, sinf = _rope_tables(positions)
    n_in = w_in.shape[2]
    assert n_in == PROJ_MAIN_COLS + FOX_HEADS
    pad_cols = PROJ_MAIN_COLS + LANES - n_in
    flat = lambda a: a.reshape((a.shape[0] * a.shape[1],) + a.shape[2:])
    w1_all, b1_all, w2_all, b2_all = flat(exp_w1), flat(exp_b1), flat(exp_w2), flat(exp_b2)
    for l in range(depth):
        lam_init = 0.8 - 0.6 * math.exp(-0.3 * l)
        sh1, sc1, g1, sh2, sc2, g2 = [mod[l, :, i * d:(i + 1) * d].reshape(b, 1, d) for i in range(6)]
        w_pad = jnp.pad(w_in[l], ((0, 0), (0, pad_cols))).astype(BF16)
        bf_pad = jnp.pad(b_forget[l], (0, LANES - FOX_HEADS)).reshape(1, LANES)
        dq, dk, dv, fq, fk, fv, flog = _inproj(x, sc1, sh1, norm_mix[l].reshape(1, d), w_pad, bf_pad,
                                               cosf, sinf)
        bias_q, bias_k = _forget_bias_lanes(flog)
        a_out = _diff_attention(dq, dk, dv, lambda_q1[l].reshape(1, -1), lambda_k1[l].reshape(1, -1),
                                lambda_q2[l].reshape(1, -1), lambda_k2[l].reshape(1, -1),
                                diff_subln[l].reshape(1, -1), lam_init)
        b_out = _fox_attention(fq, fk, fv, bias_q, bias_k)
        x1, h2, logits_t = _outproj(x, a_out, b_out, w_out[l].astype(BF16), g1, sc2, sh2,
                                    norm_ffn[l].reshape(1, d), router_w[l].T,
                                    router_b[l].reshape(-1, 1))
        x = _moe(l, x1.reshape(b * s, d), h2, logits_t, g2, w1_all, b1_all, w2_all, b2_all,
                 final_norm.reshape(1, d), s, l == depth - 1).reshape(b, s, d)
    return x
```
